```python
import math
import jax
import jax.numpy as jnp
from jax import lax
import numpy as np

D_MODEL = 4096
BATCH = 4
SEQ = 4096
DEPTH = 2

CTX_LEN = 256
GRID_W = 64
HEAD_DIM = 128
Q_BLOCK = 128
ROPE_THETA = 10000.0
NORM_EPS = 1e-6
NEG_INF = -1e30
N_MOD = 6

A_HEADS = 16
A_KV_HEADS = 4
A_GROUP = A_HEADS // A_KV_HEADS
A_SCALE = HEAD_DIM ** -0.5
B_HEADS = 16
B_Q_RANK = 1536
B_KV_RANK = 512
B_NOPE_DIM = 128
B_ROPE_DIM = 64
B_V_DIM = 128
B_SCALE = (B_NOPE_DIM + B_ROPE_DIM) ** -0.5
C_HEADS = 16
C_KV_HEADS = 4
C_GROUP = C_HEADS // C_KV_HEADS
WINDOW = 128
C_SCALE = HEAD_DIM ** -0.5
D_HEADS = 16
D_QK_DIM = 64
D_V_DIM = 2 * D_QK_DIM
D_LAYER_INDEX = 1
D_LAMBDA_INIT = 0.8 - 0.6 * math.exp(-0.3 * D_LAYER_INDEX)
D_SCALE = D_QK_DIM ** -0.5
FFN_DIM = 11008
N_EXPERTS = 8
TOP_K = 2
EXPERT_DIM = 3584

IN0_WIDTHS = (A_HEADS * HEAD_DIM, A_KV_HEADS * HEAD_DIM, A_KV_HEADS * HEAD_DIM, B_Q_RANK, B_KV_RANK, B_ROPE_DIM)
IN1_WIDTHS = (C_HEADS * HEAD_DIM, C_KV_HEADS * HEAD_DIM, C_KV_HEADS * HEAD_DIM,
              D_HEADS * 2 * D_QK_DIM, D_HEADS * 2 * D_QK_DIM, D_HEADS * D_V_DIM)
OUT0_WIDTH = A_HEADS * HEAD_DIM + B_HEADS * B_V_DIM
OUT1_WIDTH = C_HEADS * HEAD_DIM + D_HEADS * D_V_DIM

kernel_name = "hybrid_diffusion_prefix_trunk"


def _rmsnorm(x, g):
    xf = x.astype(jnp.float32)
    y = xf * lax.rsqrt(jnp.mean(xf * xf, axis=-1, keepdims=True) + NORM_EPS)
    return (y * g.astype(jnp.float32)).astype(x.dtype)


def _split(x, widths):
    bounds = [int(b) for b in np.cumsum(widths)[:-1]]
    return jnp.split(x, bounds, axis=-1)


def _heads(t, *shape):
    return t.reshape(t.shape[:2] + shape)


def _concat_heads(*outs):
    return jnp.concatenate([o.reshape(o.shape[:2] + (-1,)) for o in outs], axis=-1)


def _modulation(cond, ada_w, ada_b):
    mod = jax.nn.silu(cond) @ ada_w + ada_b
    return [m[:, None, :] for m in jnp.split(mod, N_MOD, axis=-1)]


def _modulate(x, g, shift, scale):
    return _rmsnorm(x, g) * (1 + scale) + shift


def _axial_rope(n_tokens, rot_dim):
    n_rows = n_tokens // GRID_W
    rows = jnp.repeat(jnp.arange(n_rows, dtype=jnp.float32), GRID_W)
    cols = jnp.tile(jnp.arange(GRID_W, dtype=jnp.float32), n_rows)
    n_freq = rot_dim // 4
    inv_freq = jnp.power(ROPE_THETA, -jnp.arange(n_freq, dtype=jnp.float32) / n_freq)
    ang = jnp.concatenate([rows[:, None] * inv_freq, cols[:, None] * inv_freq], axis=-1)
    return jnp.cos(ang), jnp.sin(ang)


def _apply_rope(x, ropes):
    cos, sin = ropes[x.shape[-1]]
    shape = (1, cos.shape[0]) + (1,) * (x.ndim - 3) + (cos.shape[1],)
    cos, sin = cos.reshape(shape), sin.reshape(shape)
    x1, x2 = jnp.split(x.astype(jnp.float32), 2, axis=-1)
    return jnp.concatenate([x1 * cos - x2 * sin, x2 * cos + x1 * sin], axis=-1).astype(x.dtype)


def _sweep_query_blocks(block_fn, q):
    b, s = q.shape[:2]
    nb = s // Q_BLOCK
    qb = jnp.moveaxis(q.reshape((b, nb, Q_BLOCK) + q.shape[2:]), 1, 0)
    out = lax.map(lambda a: block_fn(*a), (jnp.arange(nb), qb))
    out = jnp.moveaxis(out, 0, 1)
    return out.reshape((b, s) + out.shape[3:])


def _softmax_attend(q, k, v, scale, mask=None, sink=None):
    s = jnp.einsum("bqhgd,bthd->bhgqt", q, k).astype(jnp.float32) * scale
    if mask is not None:
        s = jnp.where(mask, s, NEG_INF)
    if sink is not None:
        sk = jnp.broadcast_to(sink.astype(jnp.float32)[None, :, :, None, None], s.shape[:-1] + (1,))
        p = jax.nn.softmax(jnp.concatenate([s, sk], axis=-1), axis=-1)[..., :-1]
    else:
        p = jax.nn.softmax(s, axis=-1)
    return jnp.einsum("bhgqt,bthd->bqhgd", p.astype(v.dtype), v)


def _diff_attend(q, k, v, lam, scale):
    s = jnp.einsum("bqhcd,bthcd->bhcqt", q, k).astype(jnp.float32) * scale
    p = jax.nn.softmax(s, axis=-1)
    w = p[:, :, 0] - lam * p[:, :, 1]
    return jnp.einsum("bhqt,bthd->bqhd", w.astype(v.dtype), v)


def _window_attend(q, k, v, kc, vc, sink, scale):
    s_len = k.shape[1]
    pad = ((0, 0), (WINDOW, WINDOW), (0, 0), (0, 0))
    kp, vp = jnp.pad(k, pad), jnp.pad(v, pad)
    band = Q_BLOCK + 2 * WINDOW
    r = jnp.arange(Q_BLOCK)[:, None]
    col = jnp.arange(band)[None, :]
    in_window = jnp.abs(col - WINDOW - r) <= WINDOW
    ctx_ok = jnp.ones((Q_BLOCK, kc.shape[1]), dtype=bool)

    def block(i, qb):
        start = i * Q_BLOCK
        kb = lax.dynamic_slice_in_dim(kp, start, band, axis=1)
        vb = lax.dynamic_slice_in_dim(vp, start, band, axis=1)
        key_pos = start - WINDOW + col
        valid = in_window & (key_pos >= 0) & (key_pos < s_len)
        mask = jnp.concatenate([ctx_ok, valid], axis=1)
        return _softmax_attend(qb, jnp.concatenate([kc, kb], 1), jnp.concatenate([vc, vb], 1), scale, mask, sink)

    return _sweep_query_blocks(block, q)


def _mixer_ab(h, hc, params, ropes, ctx_queries):
    w_in, a_q_norm, a_k_norm, b_q_norm, b_kv_norm, w_uq, w_ukv, w_o = params

    def project(t, positioned):
        aq, ak, av, b_cq, b_ckv, b_kr = _split(t @ w_in, IN0_WIDTHS)
        aq = _rmsnorm(_heads(aq, A_KV_HEADS, A_GROUP, HEAD_DIM), a_q_norm)
        ak = _rmsnorm(_heads(ak, A_KV_HEADS, HEAD_DIM), a_k_norm)
        av = _heads(av, A_KV_HEADS, HEAD_DIM)
        bq = _heads(_rmsnorm(b_cq, b_q_norm) @ w_uq, B_HEADS, 1, B_NOPE_DIM + B_ROPE_DIM)
        bkv = _heads(_rmsnorm(b_ckv, b_kv_norm) @ w_ukv, B_HEADS, B_NOPE_DIM + B_V_DIM)
        bq_nope, bq_rope = jnp.split(bq, [B_NOPE_DIM], axis=-1)
        bk_nope, bv = jnp.split(bkv, [B_NOPE_DIM], axis=-1)
        bk_rope = b_kr[:, :, None, :]
        if positioned:
            aq, ak = _apply_rope(aq, ropes), _apply_rope(ak, ropes)
            bq_rope, bk_rope = _apply_rope(bq_rope, ropes), _apply_rope(bk_rope, ropes)
        bq = jnp.concatenate([bq_nope, bq_rope], axis=-1)
        bk = jnp.concatenate([bk_nope, jnp.broadcast_to(bk_rope, bk_nope.shape[:3] + (B_ROPE_DIM,))], axis=-1)
        return aq, ak, av, bq, bk, bv

    aq, ak, av, bq, bk, bv = project(h, True)
    aqc, akc, avc, bqc, bkc, bvc = project(hc, False)
    ak_all, av_all = jnp.concatenate([akc, ak], 1), jnp.concatenate([avc, av], 1)
    bk_all, bv_all = jnp.concatenate([bkc, bk], 1), jnp.concatenate([bvc, bv], 1)
    a_out = _sweep_query_blocks(lambda i, qb: _softmax_attend(qb, ak_all, av_all, A_SCALE), aq)
    b_out = _sweep_query_blocks(lambda i, qb: _softmax_attend(qb, bk_all, bv_all, B_SCALE), bq)
    y = _concat_heads(a_out, b_out) @ w_o
    yc = None
    if ctx_queries:
        yc = _concat_heads(_softmax_attend(aqc, akc, avc, A_SCALE), _softmax_attend(bqc, bkc, bvc, B_SCALE)) @ w_o
    return y, yc


def _mixer_cd(h, hc, params, ropes, ctx_queries):
    w_in, c_sink, d_lam_q1, d_lam_k1, d_lam_q2, d_lam_k2, d_subln, w_o = params
    sink = c_sink.reshape(C_KV_HEADS, C_GROUP)
    lam = (jnp.exp(jnp.sum(d_lam_q1.astype(jnp.float32) * d_lam_k1.astype(jnp.float32)))
           - jnp.exp(jnp.sum(d_lam_q2.astype(jnp.float32) * d_lam_k2.astype(jnp.float32))) + D_LAMBDA_INIT)

    def project(t, positioned):
        cq, ck, cv, dq, dk, dv = _split(t @ w_in, IN1_WIDTHS)
        cq = _heads(cq, C_KV_HEADS, C_GROUP, HEAD_DIM)
        ck = _heads(ck, C_KV_HEADS, HEAD_DIM)
        cv = _heads(cv, C_KV_HEADS, HEAD_DIM)
        dq = _heads(dq, D_HEADS, 2, D_QK_DIM)
        dk = _heads(dk, D_HEADS, 2, D_QK_DIM)
        dv = _heads(dv, D_HEADS, D_V_DIM)
        if positioned:
            cq, ck = _apply_rope(cq, ropes), _apply_rope(ck, ropes)
            dq, dk = _apply_rope(dq, ropes), _apply_rope(dk, ropes)
        return cq, ck, cv, dq, dk, dv

    def d_post(o):
        return _rmsnorm(o, d_subln) * (1 - D_LAMBDA_INIT)

    cq, ck, cv, dq, dk, dv = project(h, True)
    cqc, ckc, cvc, dqc, dkc, dvc = project(hc, False)
    c_out = _window_attend(cq, ck, cv, ckc, cvc, sink, C_SCALE)
    dk_all, dv_all = jnp.concatenate([dkc, dk], 1), jnp.concatenate([dvc, dv], 1)
    d_out = d_post(_sweep_query_blocks(lambda i, qb: _diff_attend(qb, dk_all, dv_all, lam, D_SCALE), dq))
    y = _concat_heads(c_out, d_out) @ w_o
    yc = None
    if ctx_queries:
        yc = _concat_heads(_softmax_attend(cqc, ckc, cvc, C_SCALE, sink=sink),
                           d_post(_diff_attend(dqc, dkc, dvc, lam, D_SCALE))) @ w_o
    return y, yc


def _swiglu(h, w_gate, w_up, w_down):
    return (jax.nn.silu(h @ w_gate) * (h @ w_up)) @ w_down


def _moe_swiglu(h, router_w, w_gate, w_up, w_down):
    shp = h.shape
    t = h.reshape(-1, shp[-1])
    logits = (t @ router_w).astype(jnp.float32)
    top_val, top_idx = lax.top_k(logits, TOP_K)
    gates = jax.nn.softmax(top_val, axis=-1)
    weights = jnp.sum(jax.nn.one_hot(top_idx, N_EXPERTS, dtype=jnp.float32) * gates[..., None], axis=1)
    out = jnp.zeros_like(t)
    for e in range(N_EXPERTS):
        out = out + weights[:, e:e + 1].astype(t.dtype) * _swiglu(t, w_gate[e], w_up[e], w_down[e])
    return out.reshape(shp)


def setup_inputs(seed: int = 0) -> dict:
    key = jax.random.key(seed)
    ks = jax.random.split(key, 40)
    counter = iter(range(40))

    def nk():
        return ks[next(counter)]

    def dense(shape, fan_in, gain=1.0):
        return gain * fan_in ** -0.5 * jax.random.normal(nk(), shape, jnp.float32)

    def norm_gain(n):
        return 1.0 + 0.01 * jax.random.normal(nk(), (n,), jnp.float32)

    def small(shape, s):
        return s * jax.random.normal(nk(), shape, jnp.float32)

    D = D_MODEL
    return {
        "x": jax.random.normal(nk(), (BATCH, SEQ, D), jnp.float32),
        "c": jax.random.normal(nk(), (BATCH, D), jnp.float32),
        "ctx": jax.random.normal(nk(), (BATCH, CTX_LEN, D), jnp.float32),
        "c_ctx": jax.random.normal(nk(), (D,), jnp.float32),
        "ada_w0": dense((D, N_MOD * D), D, 0.5),
        "ada_b0": small((N_MOD * D,), 0.01),
        "norm_mix0": norm_gain(D),
        "norm_ffn0": norm_gain(D),
        "w_in0": dense((D, sum(IN0_WIDTHS)), D),
        "a_q_norm": norm_gain(HEAD_DIM),
        "a_k_norm": norm_gain(HEAD_DIM),
        "b_q_norm": norm_gain(B_Q_RANK),
        "b_kv_norm": norm_gain(B_KV_RANK),
        "w_uq": dense((B_Q_RANK, B_HEADS * (B_NOPE_DIM + B_ROPE_DIM)), B_Q_RANK),
        "w_ukv": dense((B_KV_RANK, B_HEADS * (B_NOPE_DIM + B_V_DIM)), B_KV_RANK),
        "w_o0": dense((OUT0_WIDTH, D), OUT0_WIDTH),
        "ffn_w_gate": dense((D, FFN_DIM), D),
        "ffn_w_up": dense((D, FFN_DIM), D),
        "ffn_w_down": dense((FFN_DIM, D), FFN_DIM),
        "ada_w1": dense((D, N_MOD * D), D, 0.5),
        "ada_b1": small((N_MOD * D,), 0.01),
        "norm_mix1": norm_gain(D),
        "norm_ffn1": norm_gain(D),
        "w_in1": dense((D, sum(IN1_WIDTHS)), D),
        "c_sink": small((C_HEADS,), 0.5),
        "d_lam_q1": small((D_QK_DIM,), 0.1),
        "d_lam_k1": small((D_QK_DIM,), 0.1),
        "d_lam_q2": small((D_QK_DIM,), 0.1),
        "d_lam_k2": small((D_QK_DIM,), 0.1),
        "d_subln": norm_gain(D_V_DIM),
        "w_o1": dense((OUT1_WIDTH, D), OUT1_WIDTH),
        "router_w": dense((D, N_EXPERTS), D),
        "moe_w_gate": dense((N_EXPERTS, D, EXPERT_DIM), D),
        "moe_w_up": dense((N_EXPERTS, D, EXPERT_DIM), D),
        "moe_w_down": dense((N_EXPERTS, EXPERT_DIM, D), EXPERT_DIM),
        "final_norm": norm_gain(D),
    }


def reference(x, c, ctx, c_ctx,
              ada_w0, ada_b0, norm_mix0, norm_ffn0, w_in0, a_q_norm, a_k_norm, b_q_norm, b_kv_norm,
              w_uq, w_ukv, w_o0, ffn_w_gate, ffn_w_up, ffn_w_down,
              ada_w1, ada_b1, norm_mix1, norm_ffn1, w_in1, c_sink, d_lam_q1, d_lam_k1, d_lam_q2, d_lam_k2,
              d_subln, w_o1, router_w, moe_w_gate, moe_w_up, moe_w_down,
              final_norm):
    n_tokens = x.shape[1]
    ropes = {d: _axial_rope(n_tokens, d) for d in {HEAD_DIM, B_ROPE_DIM, D_QK_DIM}}
    layers = (
        (ada_w0, ada_b0, norm_mix0, norm_ffn0,
         (w_in0, a_q_norm, a_k_norm, b_q_norm, b_kv_norm, w_uq, w_ukv, w_o0),
         (ffn_w_gate, ffn_w_up, ffn_w_down)),
        (ada_w1, ada_b1, norm_mix1, norm_ffn1,
         (w_in1, c_sink, d_lam_q1, d_lam_k1, d_lam_q2, d_lam_k2, d_subln, w_o1),
         (router_w, moe_w_gate, moe_w_up, moe_w_down)),
    )
    mixers = (_mixer_ab, _mixer_cd)
    ffns = (_swiglu, _moe_swiglu)
    c_ctx_row = c_ctx[None, :]
    for layer in range(DEPTH):
        ada_w, ada_b, norm_mix, norm_ffn, mix_p, ffn_p = layers[layer]
        kind = layer % 2
        last = layer == DEPTH - 1
        sh1, sc1, g1, sh2, sc2, g2 = _modulation(c, ada_w, ada_b)
        csh1, csc1, cg1, csh2, csc2, cg2 = _modulation(c_ctx_row, ada_w, ada_b)
        y, yc = mixers[kind](_modulate(x, norm_mix, sh1, sc1), _modulate(ctx, norm_mix, csh1, csc1),
                             mix_p, ropes, not last)
        x = x + g1 * y
        x = x + g2 * ffns[kind](_modulate(x, norm_ffn, sh2, sc2), *ffn_p)
        if not last:
            ctx = ctx + cg1 * yc
            ctx = ctx + cg2 * ffns[kind](_modulate(ctx, norm_ffn, csh2, csc2), *ffn_p)
    return _rmsnorm(x, final_norm)
```

```python
import functools
import math
from typing import NamedTuple

import jax
import jax.numpy as jnp
from jax import lax
from jax.experimental import pallas as pl
from jax.experimental.pallas import tpu as pltpu

F32 = jnp.float32
BF16 = jnp.bfloat16

LANES = 128
HEAD_DIM = 128
ROPE_THETA = 10000.0
NORM_EPS = 1e-6
MASK_VALUE = -1e30
N_MOD = 6
D_LAYER_INDEX = 1
D_LAMBDA_INIT = 0.8 - 0.6 * math.exp(-0.3 * D_LAYER_INDEX)
VMEM_HEADROOM = 6 << 20


class Dims(NamedTuple):
    d_model: int
    batch: int
    seq: int
    ctx_len: int
    grid_w: int
    a_heads: int
    a_kv_heads: int
    b_heads: int
    b_q_rank: int
    b_kv_rank: int
    c_heads: int
    c_kv_heads: int
    window: int
    d_heads: int
    ffn_dim: int
    n_experts: int
    expert_dim: int


def _pick(n, prefs):
    for p in prefs:
        if n % p == 0:
            return p
    raise ValueError(f"no tile in {prefs} divides {n}")


def _params(sem, vmem_bytes):
    return pltpu.CompilerParams(dimension_semantics=sem, vmem_limit_bytes=int(vmem_bytes) + VMEM_HEADROOM)


def _nbytes(shape, dtype):
    return math.prod(shape) * jnp.dtype(dtype).itemsize


def _mod_kernel(c_ref, w_ref, b_ref, o_ref):
    c = c_ref[...]
    s = (c * jax.nn.sigmoid(c)).astype(BF16)
    o_ref[...] = jnp.dot(s, w_ref[...].astype(BF16), preferred_element_type=F32) + b_ref[...]


def _modulation(cond, ada_w, ada_b):
    rows, d = cond.shape
    n = ada_w.shape[1]
    tn = _pick(n, (512, 256, 128))
    vmem = 2 * (_nbytes((d, tn), F32) + _nbytes((rows, tn), F32) * 2) + _nbytes((rows, d), F32) * 2
    return pl.pallas_call(
        _mod_kernel,
        grid=(n // tn,),
        in_specs=[pl.BlockSpec((rows, d), lambda j: (0, 0)),
                  pl.BlockSpec((d, tn), lambda j: (0, j)),
                  pl.BlockSpec((1, tn), lambda j: (0, j))],
        out_specs=pl.BlockSpec((rows, tn), lambda j: (0, j)),
        out_shape=jax.ShapeDtypeStruct((rows, n), F32),
        compiler_params=_params(("parallel",), vmem),
        name="modulation",
    )(cond, ada_w, ada_b.reshape(1, n))


def _norm_rows(x, g):
    var = jnp.mean(x * x, axis=-1, keepdims=True)
    return x * lax.rsqrt(var + NORM_EPS) * g


def _norm_mod_kernel(x_ref, g_ref, sh_ref, sc_ref, o_ref):
    y = _norm_rows(x_ref[...], g_ref[...])
    o_ref[...] = (y * (1.0 + sc_ref[0]) + sh_ref[0]).astype(o_ref.dtype)


def _norm_mod_router_kernel(x_ref, g_ref, sh_ref, sc_ref, rhi_ref, rlo_ref, o_ref, of_ref, lg_ref):
    y = _norm_rows(x_ref[...], g_ref[...])
    h = y * (1.0 + sc_ref[0]) + sh_ref[0]
    of_ref[...] = h
    hi = h.astype(BF16)
    o_ref[...] = hi
    lo = (h - hi.astype(F32)).astype(BF16)
    lg_ref[...] = (jnp.dot(hi, rhi_ref[...], preferred_element_type=F32)
                   + jnp.dot(hi, rlo_ref[...], preferred_element_type=F32)
                   + jnp.dot(lo, rhi_ref[...], preferred_element_type=F32))


def _mod_index(tm, seq, batch):
    return lambda i: jnp.minimum((i * tm) // seq, batch)


def _norm_mod(x, g, shift, scale, *, rows, seq, batch, router=None):
    m, d = x.shape
    tm = _pick(math.gcd(rows, seq), (256, 128, 64, 32, 16))
    midx = _mod_index(tm, seq, batch)
    in_specs = [pl.BlockSpec((tm, d), lambda i: (i, 0)),
                pl.BlockSpec((1, d), lambda i: (0, 0)),
                pl.BlockSpec((1, 1, d), lambda i: (midx(i), 0, 0)),
                pl.BlockSpec((1, 1, d), lambda i: (midx(i), 0, 0))]
    vmem = 2 * (_nbytes((tm, d), F32) + _nbytes((tm, d), BF16)) + 6 * _nbytes((1, d), F32)
    if router is None:
        return pl.pallas_call(
            _norm_mod_kernel,
            grid=(rows // tm,),
            in_specs=in_specs,
            out_specs=pl.BlockSpec((tm, d), lambda i: (i, 0)),
            out_shape=jax.ShapeDtypeStruct((rows, d), BF16),
            compiler_params=_params(("parallel",), vmem),
            name="norm_modulate",
        )(x, g.reshape(1, d), shift, scale)
    rhi, rlo = router
    vmem += 2 * (_nbytes((tm, d), F32) + _nbytes((tm, LANES), F32)) + 4 * _nbytes((d, LANES), BF16)
    return pl.pallas_call(
        _norm_mod_router_kernel,
        grid=(rows // tm,),
        in_specs=in_specs + [pl.BlockSpec((d, LANES), lambda i: (0, 0)),
                             pl.BlockSpec((d, LANES), lambda i: (0, 0))],
        out_specs=[pl.BlockSpec((tm, d), lambda i: (i, 0)),
                   pl.BlockSpec((tm, d), lambda i: (i, 0)),
                   pl.BlockSpec((tm, LANES), lambda i: (i, 0))],
        out_shape=[jax.ShapeDtypeStruct((rows, d), BF16),
                   jax.ShapeDtypeStruct((rows, d), F32),
                   jax.ShapeDtypeStruct((rows, LANES), F32)],
        compiler_params=_params(("parallel",), vmem),
        name="norm_modulate_router",
    )(x, g.reshape(1, d), shift, scale, rhi, rlo)


def _mm_kernel(*refs, n_w, nk, epi, grouped):
    refs = list(refs)
    if grouped:
        used_ref = refs[1]
        refs = refs[2:]
    x_ref, w_refs = refs[0], refs[1:1 + n_w]
    pos = 1 + n_w
    if epi == "gres":
        res_ref, gate_ref = refs[pos], refs[pos + 1]
        pos += 2
    o_ref = refs[pos]
    acc_refs = refs[pos + 1:]
    k = pl.program_id(2)

    def epilogue(parts):
        if epi == "swiglu":
            g, u = parts
            out = g * jax.nn.sigmoid(g) * u
        elif epi == "gres":
            out = res_ref[...] + gate_ref[0] * parts[0]
        else:
            out = parts[0]
        o_ref[...] = out.astype(o_ref.dtype)

    def compute():
        x = x_ref[...]
        parts = [jnp.dot(x, (w[0] if grouped else w[...]), preferred_element_type=F32) for w in w_refs]
        if nk == 1:
            epilogue(parts)
            return

        @pl.when(k == 0)
        def _():
            for a, p in zip(acc_refs, parts):
                a[...] = p

        @pl.when(k > 0)
        def _():
            for a, p in zip(acc_refs, parts):
                a[...] += p

        @pl.when(k == nk - 1)
        def _():
            epilogue([a[...] for a in acc_refs])

    if grouped:
        live = pl.program_id(0) < used_ref[0]
        pl.when(live)(compute)

        @pl.when(jnp.logical_not(live))
        def _():
            o_ref[...] = jnp.zeros_like(o_ref)
    else:
        compute()


def _matmul(x, ws, *, out_dtype, tm, tn, tk=None, epi="none", res=None, gate=None, seq=None, batch=None,
            rows=None, tile_expert=None, tiles_used=None, name="matmul"):
    m, kdim = x.shape
    rows = m if rows is None else rows
    grouped = tile_expert is not None
    n = ws[0].shape[-1]
    tk = kdim if tk is None else tk
    nk = kdim // tk
    assert rows % tm == 0 and n % tn == 0 and kdim % tk == 0
    n_w = len(ws)

    def im(f):
        return (lambda i, j, k, te, nu: f(i, j, k, te)) if grouped else (lambda i, j, k: f(i, j, k, None))

    in_specs = [pl.BlockSpec((tm, tk), im(lambda i, j, k, te: (i, k)))]
    for _ in ws:
        if grouped:
            in_specs.append(pl.BlockSpec((1, tk, tn), im(lambda i, j, k, te: (te[i], k, j))))
        else:
            in_specs.append(pl.BlockSpec((tk, tn), im(lambda i, j, k, te: (k, j))))
    args = [x, *ws]
    vmem = 2 * (_nbytes((tm, tk), x.dtype) + n_w * _nbytes((tk, tn), ws[0].dtype) + _nbytes((tm, tn), out_dtype))
    vmem += 2 * n_w * _nbytes((tm, tn), F32)
    if epi == "gres":
        midx = _mod_index(tm, seq, batch)
        in_specs += [pl.BlockSpec((tm, tn), im(lambda i, j, k, te: (i, j))),
                     pl.BlockSpec((1, 1, tn), im(lambda i, j, k, te: (midx(i), 0, j)))]
        args += [res, gate]
        vmem += 2 * _nbytes((tm, tn), F32)
    scratch = [pltpu.VMEM((tm, tn), F32) for _ in range(n_w)] if nk > 1 else []
    vmem += len(scratch) * _nbytes((tm, tn), F32)
    grid = (rows // tm, n // tn, nk)
    out_spec = pl.BlockSpec((tm, tn), im(lambda i, j, k, te: (i, j)))
    kern = functools.partial(_mm_kernel, n_w=n_w, nk=nk, epi=epi, grouped=grouped)
    cp = _params(("parallel", "parallel", "arbitrary"), vmem)
    out_shape = jax.ShapeDtypeStruct((rows, n), out_dtype)
    if grouped:
        gs = pltpu.PrefetchScalarGridSpec(num_scalar_prefetch=2, grid=grid, in_specs=in_specs,
                                          out_specs=out_spec, scratch_shapes=scratch)
        return pl.pallas_call(kern, grid_spec=gs, out_shape=out_shape, compiler_params=cp, name=name)(
            tile_expert, tiles_used, *args)
    return pl.pallas_call(kern, grid=grid, in_specs=in_specs, out_specs=out_spec, out_shape=out_shape,
                          scratch_shapes=scratch, compiler_params=cp, name=name)(*args)


def _rope_tables(seq, grid_w, rot_dim, pad_rows):
    n_rows = seq // grid_w
    rows = jnp.repeat(jnp.arange(n_rows, dtype=F32), grid_w)
    cols = jnp.tile(jnp.arange(grid_w, dtype=F32), n_rows)
    n_freq = rot_dim // 4
    inv_freq = jnp.power(ROPE_THETA, -jnp.arange(n_freq, dtype=F32) / n_freq)
    ang = jnp.concatenate([rows[:, None] * inv_freq, cols[:, None] * inv_freq], axis=-1)
    cos, sin = jnp.cos(ang), jnp.sin(ang)
    zero = jnp.zeros_like(sin)
    reps = LANES // rot_dim
    cos_t = jnp.tile(jnp.concatenate([cos, cos], axis=-1), (1, reps))
    up_t = jnp.tile(jnp.concatenate([-sin, zero], axis=-1), (1, reps))
    down_t = jnp.tile(jnp.concatenate([zero, sin], axis=-1), (1, reps))
    ident = jnp.ones((pad_rows, LANES), F32)
    zpad = jnp.zeros((pad_rows, LANES), F32)
    return (jnp.concatenate([cos_t, ident]), jnp.concatenate([up_t, zpad]), jnp.concatenate([down_t, zpad]))


def _prep_kernel(*refs, norm, rot_dim, scale):
    refs = list(refs)
    x_ref = refs.pop(0)
    x = x_ref[...].astype(F32)
    if norm:
        x = _norm_rows(x, refs.pop(0)[...])
    if rot_dim:
        cos_ref, up_ref, down_ref = refs[0], refs[1], refs[2]
        half = rot_dim // 2
        x = (x * cos_ref[...] + pltpu.roll(x, LANES - half, 1) * up_ref[...]
             + pltpu.roll(x, half, 1) * down_ref[...])
    if scale != 1.0:
        x = x * scale
    o_ref = refs[-1]
    o_ref[...] = x.astype(o_ref.dtype)


def _prep_heads(src, col0, n_heads, *, rows, seq, lat_rows, gain=None, rope=None, rot_dim=0, scale=1.0,
                name="prep_heads"):
    tm = _pick(math.gcd(math.gcd(rows, seq), lat_rows), (256, 128, 64, 32, 16))
    in_specs = [pl.BlockSpec((tm, LANES), lambda i, h: (i, col0 + h))]
    args = [src]
    if gain is not None:
        in_specs.append(pl.BlockSpec((1, LANES), lambda i, h: (0, 0)))
        args.append(gain.reshape(1, LANES))
    if rot_dim:
        lat_tiles, seq_tiles = lat_rows // tm, seq // tm
        tab = lambda i, h: (jnp.where(i < lat_tiles, i % seq_tiles, seq_tiles), 0)
        in_specs += [pl.BlockSpec((tm, LANES), tab)] * 3
        args += list(rope)
    vmem = 2 * (_nbytes((tm, LANES), F32) * 5 + _nbytes((tm, LANES), BF16))
    return pl.pallas_call(
        functools.partial(_prep_kernel, norm=gain is not None, rot_dim=rot_dim, scale=scale),
        grid=(rows // tm, n_heads),
        in_specs=in_specs,
        out_specs=pl.BlockSpec((tm, LANES), lambda i, h: (i, h)),
        out_shape=jax.ShapeDtypeStruct((rows, n_heads * LANES), BF16),
        compiler_params=_params(("parallel", "parallel"), vmem),
        name=name,
    )(*args)


def _qk(q, k):
    return lax.dot_general(q, k, (((1,), (1,)), ((), ())), preferred_element_type=F32)


def _online_update(q, k, v, state, mask=None):
    m, l, acc = state
    s = _qk(q, k)
    if mask is not None:
        s = jnp.where(mask, s, MASK_VALUE)
    m_new = jnp.maximum(m, jnp.max(s, axis=-1, keepdims=True))
    alpha = jnp.exp(m - m_new)
    p = jnp.exp(s - m_new)
    l = alpha * l + jnp.sum(p, axis=-1, keepdims=True)
    acc = alpha * acc + jnp.dot(p.astype(v.dtype), v, preferred_element_type=F32)
    return m_new, l, acc


def _cat(refs, rows=None):
    parts = [r[...] if rows is None else r[rows, :] for r in refs]
    return parts[0] if len(parts) == 1 else jnp.concatenate(parts, axis=-1)


def _flash_kernel(*refs, mode, n_q, has_lat, tq, tqs, tkc, seq, window):
    refs = list(refs)
    q_refs = [refs.pop(0) for _ in range(n_q)]
    kc_refs = [refs.pop(0) for _ in range(n_q)]
    vc_ref = refs.pop(0)
    if has_lat:
        kl_refs = [refs.pop(0) for _ in range(n_q)]
        vl_ref = refs.pop(0)
    if mode == "window":
        sink_ref = refs.pop(0)
    if mode == "diff":
        lam_ref, subln_ref = refs.pop(0), refs.pop(0)
    o_ref = refs.pop(0)
    n_state = 2 if mode == "diff" else 1
    qi = pl.program_id(2)

    for sub in range(tq // tqs):
        rows = pl.ds(sub * tqs, tqs)
        q = _cat(q_refs, rows)
        if mode == "diff":
            lane = lax.broadcasted_iota(jnp.int32, q.shape, 1)
            qs = [jnp.where(lane < LANES // 2, q, jnp.zeros_like(q)),
                  jnp.where(lane >= LANES // 2, q, jnp.zeros_like(q))]
        else:
            qs = [q]
        if mode == "window":
            m0 = jnp.broadcast_to(sink_ref[0][:, :1], (tqs, 1))
            l0 = jnp.ones((tqs, 1), F32)
        else:
            m0 = jnp.full((tqs, 1), MASK_VALUE, F32)
            l0 = jnp.zeros((tqs, 1), F32)
        state0 = (m0, l0, jnp.zeros((tqs, LANES), F32))
        kc, vc = _cat(kc_refs), vc_ref[...]
        states = tuple(_online_update(qq, kc, vc, state0) for qq in qs)

        if has_lat:
            q_pos0 = qi * tq + sub * tqs

            def body(c, states, qs=qs, q_pos0=q_pos0):
                start = pl.multiple_of(c * tkc, tkc)
                k = _cat(kl_refs, pl.ds(start, tkc))
                v = vl_ref[pl.ds(start, tkc), :]
                mask = None
                if mode == "window":
                    qp = q_pos0 + lax.broadcasted_iota(jnp.int32, (tqs, tkc), 0)
                    kp = start + lax.broadcasted_iota(jnp.int32, (tqs, tkc), 1)
                    mask = jnp.abs(qp - kp) <= window
                return tuple(_online_update(qq, k, v, st, mask) for qq, st in zip(qs, states))

            if mode == "window":
                lo = jnp.maximum(q_pos0 - window, 0) // tkc
                hi = (jnp.minimum(q_pos0 + tqs + window, seq) + tkc - 1) // tkc
            else:
                lo, hi = 0, seq // tkc
            states = lax.fori_loop(lo, hi, body, states)

        outs = [acc / l for (_, l, acc) in states]
        if mode == "diff":
            lam_rows = lam_ref[...]
            lam = (jnp.exp(jnp.sum(lam_rows[0:1] * lam_rows[1:2], axis=-1, keepdims=True))
                   - jnp.exp(jnp.sum(lam_rows[2:3] * lam_rows[3:4], axis=-1, keepdims=True)) + D_LAMBDA_INIT)
            out = _norm_rows(outs[0] - lam * outs[1], subln_ref[...]) * (1.0 - D_LAMBDA_INIT)
        else:
            out = outs[0]
        o_ref[rows, :] = out.astype(o_ref.dtype)
    del n_state


def _flash(q_parts, k_parts, v, *, n_heads, group, batch, seq, ctx_len, lat_rows, ctx_queries=False,
           mode="softmax", sink=None, lam=None, subln=None, window=0, name="flash"):
    has_lat = not ctx_queries
    ctx_blk0 = lat_rows // ctx_len
    if ctx_queries:
        tq, nq = ctx_len, 1
        q_row = lambda b, h, i: ctx_blk0 + b
        out_rows = batch * ctx_len
        o_row = lambda b, h, i: b
    else:
        tq = _pick(seq, (512, 256, 128))
        nq = seq // tq
        q_row = lambda b, h, i: b * nq + i
        out_rows = lat_rows
        o_row = q_row
    tqs = min(tq, 256)
    tkc = _pick(seq, (512, 256, 128)) if mode != "window" else _pick(seq, (128,))
    n_q = len(q_parts)
    in_specs, args = [], []
    vmem = 0

    def add(arr, block, imap):
        nonlocal vmem
        in_specs.append(pl.BlockSpec(block, imap))
        args.append(arr)
        vmem += 2 * _nbytes(block, arr.dtype)

    for arr, c0 in q_parts:
        add(arr, (tq, LANES), lambda b, h, i, c0=c0: (q_row(b, h, i), c0 + h))
    for arr, c0, per_head in k_parts:
        add(arr, (ctx_len, LANES), lambda b, h, i, c0=c0, ph=per_head: (ctx_blk0 + b, c0 + (h // group) * ph))
    v_arr, v_c0 = v
    add(v_arr, (ctx_len, LANES), lambda b, h, i: (ctx_blk0 + b, v_c0 + h // group))
    if has_lat:
        for arr, c0, per_head in k_parts:
            add(arr, (seq, LANES), lambda b, h, i, c0=c0, ph=per_head: (b, c0 + (h // group) * ph))
        add(v_arr, (seq, LANES), lambda b, h, i: (b, v_c0 + h // group))
    if mode == "window":
        add(sink, (1, 1, LANES), lambda b, h, i: (h, 0, 0))
    if mode == "diff":
        add(lam, (4, LANES), lambda b, h, i: (0, 0))
        add(subln, (1, LANES), lambda b, h, i: (0, 0))
    vmem += 2 * _nbytes((tq, LANES), BF16) + 8 * _nbytes((tqs, max(tkc, ctx_len)), F32)
    kern = functools.partial(_flash_kernel, mode=mode, n_q=n_q, has_lat=has_lat, tq=tq, tqs=tqs, tkc=tkc,
                             seq=seq, window=window)
    return pl.pallas_call(
        kern,
        grid=(batch, n_heads, nq),
        in_specs=in_specs,
        out_specs=pl.BlockSpec((tq, LANES), lambda b, h, i: (o_row(b, h, i), h)),
        out_shape=jax.ShapeDtypeStruct((out_rows, n_heads * LANES), BF16),
        compiler_params=_params(("parallel", "parallel", "parallel"), vmem),
        name=name,
    )(*args)


def _route_kernel(lg_ref, info_ref, cnt_ref, base_ref, *, n_experts, tb):
    step = pl.program_id(0)

    @pl.when(step == 0)
    def _():
        base_ref[...] = jnp.zeros_like(base_ref)

    lane = lax.broadcasted_iota(jnp.int32, (tb, LANES), 1).astype(F32)
    logits = jnp.where(lane < n_experts, lg_ref[...], -jnp.inf)
    v1 = jnp.max(logits, axis=-1, keepdims=True)
    i1 = jnp.min(jnp.where(logits == v1, lane, float(LANES)), axis=-1, keepdims=True)
    hot1 = lane == i1
    rest = jnp.where(hot1, -jnp.inf, logits)
    v2 = jnp.max(rest, axis=-1, keepdims=True)
    i2 = jnp.min(jnp.where(rest == v2, lane, float(LANES)), axis=-1, keepdims=True)
    hot2 = lane == i2
    e = jnp.exp(v2 - v1)
    w1 = 1.0 / (1.0 + e)
    w2 = e / (1.0 + e)
    sel = jnp.where(hot1 | hot2, 1.0, 0.0)
    r = lax.broadcasted_iota(jnp.int32, (tb, tb), 0)
    c = lax.broadcasted_iota(jnp.int32, (tb, tb), 1)
    tri = jnp.where(c < r, 1.0, 0.0).astype(BF16)
    before = jnp.dot(tri, sel.astype(BF16), preferred_element_type=F32) + base_ref[...]
    rank1 = jnp.sum(jnp.where(hot1, before, 0.0), axis=-1, keepdims=True)
    rank2 = jnp.sum(jnp.where(hot2, before, 0.0), axis=-1, keepdims=True)
    base_ref[...] += jnp.sum(sel, axis=0, keepdims=True)
    cnt_ref[...] = jnp.broadcast_to(base_ref[...], cnt_ref.shape)
    info = jnp.zeros((tb, LANES), F32)
    for slot, val in enumerate((i1, i2, rank1, rank2, w1, w2)):
        info = jnp.where(lane == float(slot), val, info)
    info_ref[...] = info


def _route(logits, n_experts):
    n = logits.shape[0]
    tb = _pick(n, (256, 128, 64, 32, 16, 8))
    vmem = 4 * _nbytes((tb, LANES), F32) + 16 * _nbytes((tb, max(tb, LANES)), F32)
    return pl.pallas_call(
        functools.partial(_route_kernel, n_experts=n_experts, tb=tb),
        grid=(n // tb,),
        in_specs=[pl.BlockSpec((tb, LANES), lambda i: (i, 0))],
        out_specs=[pl.BlockSpec((tb, LANES), lambda i: (i, 0)),
                   pl.BlockSpec((8, LANES), lambda i: (0, 0))],
        out_shape=[jax.ShapeDtypeStruct((n, LANES), F32), jax.ShapeDtypeStruct((8, LANES), F32)],
        scratch_shapes=[pltpu.VMEM((1, LANES), F32)],
        compiler_params=_params(("arbitrary",), vmem),
        name="moe_route",
    )(logits)


def _row_copy(src_hbm, row, dst_vmem, slot, sem):
    return pltpu.make_async_copy(src_hbm.at[pl.ds(row, 1)], dst_vmem.at[pl.ds(slot, 1)], sem)


def _gather_kernel(src_ref, h_hbm, o_ref, buf, sem, *, tm):
    base = pl.program_id(0) * tm

    def issue(r, carry):
        _row_copy(h_hbm, src_ref[base + r], buf, r, sem).start()
        return carry

    lax.fori_loop(0, tm, issue, 0)

    def drain(r, carry):
        _row_copy(h_hbm, 0, buf, r, sem).wait()
        return carry

    lax.fori_loop(0, tm, drain, 0)
    o_ref[...] = buf[...].astype(o_ref.dtype)


def _gather_rows(src_rows, h, *, tm):
    p = src_rows.shape[0]
    d = h.shape[1]
    vmem = _nbytes((tm, d), F32) + 2 * _nbytes((tm, d), BF16)
    gs = pltpu.PrefetchScalarGridSpec(
        num_scalar_prefetch=1, grid=(p // tm,),
        in_specs=[pl.BlockSpec(memory_space=pl.ANY)],
        out_specs=pl.BlockSpec((tm, d), lambda i, src: (i, 0)),
        scratch_shapes=[pltpu.VMEM((tm, d), F32), pltpu.SemaphoreType.DMA(())])
    return pl.pallas_call(
        functools.partial(_gather_kernel, tm=tm), grid_spec=gs,
        out_shape=jax.ShapeDtypeStruct((p, d), BF16),
        compiler_params=_params(("arbitrary",), vmem),
        name="moe_gather",
    )(src_rows, h)


def _combine_kernel(d1_ref, d2_ref, y_hbm, x_ref, info_ref, gate_ref, g_ref, o_ref, buf1, buf2, sem, *, tb):
    base = pl.program_id(0) * tb

    def issue(r, carry):
        _row_copy(y_hbm, d1_ref[base + r], buf1, r, sem.at[0]).start()
        _row_copy(y_hbm, d2_ref[base + r], buf2, r, sem.at[1]).start()
        return carry

    lax.fori_loop(0, tb, issue, 0)

    def drain(r, carry):
        _row_copy(y_hbm, 0, buf1, r, sem.at[0]).wait()
        _row_copy(y_hbm, 0, buf2, r, sem.at[1]).wait()
        return carry

    lax.fori_loop(0, tb, drain, 0)
    info = info_ref[...]
    w1, w2 = info[:, 4:5], info[:, 5:6]
    x = x_ref[...] + gate_ref[0] * (w1 * buf1[...] + w2 * buf2[...])
    o_ref[...] = _norm_rows(x, g_ref[...])


def _combine(dest1, dest2, y, x, info, gate, final_norm, *, seq, batch):
    n, d = x.shape
    tb = _pick(seq, (256, 128, 64, 32, 16, 8))
    midx = _mod_index(tb, seq, batch)
    vmem = 2 * _nbytes((tb, d), F32) * 3 + 2 * _nbytes((tb, d), F32) + 2 * _nbytes((tb, LANES), F32)
    gs = pltpu.PrefetchScalarGridSpec(
        num_scalar_prefetch=2, grid=(n // tb,),
        in_specs=[pl.BlockSpec(memory_space=pl.ANY),
                  pl.BlockSpec((tb, d), lambda i, a, b: (i, 0)),
                  pl.BlockSpec((tb, LANES), lambda i, a, b: (i, 0)),
                  pl.BlockSpec((1, 1, d), lambda i, a, b: (midx(i), 0, 0)),
                  pl.BlockSpec((1, d), lambda i, a, b: (0, 0))],
        out_specs=pl.BlockSpec((tb, d), lambda i, a, b: (i, 0)),
        scratch_shapes=[pltpu.VMEM((tb, d), F32), pltpu.VMEM((tb, d), F32), pltpu.SemaphoreType.DMA((2,))])
    return pl.pallas_call(
        functools.partial(_combine_kernel, tb=tb), grid_spec=gs,
        out_shape=jax.ShapeDtypeStruct((n, d), F32),
        compiler_params=_params(("arbitrary",), vmem),
        name="moe_combine_norm",
    )(dest1, dest2, y, x, info, gate, final_norm.reshape(1, d))


def _moe(h_bf16_unused, h_f32, logits, x, gate, final_norm, w_gate, w_up, w_down, *, dims):
    n, d = h_f32.shape
    n_exp = dims.n_experts
    tm = _pick(n, (512, 256, 128, 64, 32, 16))
    p = 2 * n + n_exp * tm
    info, counts = _route(logits, n_exp)
    cnt = counts[0, :n_exp].astype(jnp.int32)
    padded = ((cnt + tm - 1) // tm) * tm
    ends = jnp.cumsum(padded)
    starts = ends - padded
    e1, e2 = info[:, 0].astype(jnp.int32), info[:, 1].astype(jnp.int32)
    dest1 = starts[e1] + info[:, 2].astype(jnp.int32)
    dest2 = starts[e2] + info[:, 3].astype(jnp.int32)
    token = jnp.arange(n, dtype=jnp.int32)
    src = jnp.zeros((p,), jnp.int32).at[dest1].set(token).at[dest2].set(token)
    tile_start = jnp.arange(p // tm, dtype=jnp.int32) * tm
    tile_expert = jnp.minimum(jnp.sum(tile_start[:, None] >= ends[None, :], axis=1), n_exp - 1).astype(jnp.int32)
    tiles_used = (ends[-1:] // tm).astype(jnp.int32)

    xs = _gather_rows(src, h_f32, tm=_pick(tm, (256, 128, 64, 32, 16)))
    f = w_gate.shape[-1]
    act = _matmul(xs, [w_gate, w_up], out_dtype=BF16, tm=tm, tn=_pick(f, (512, 256, 128)), epi="swiglu",
                  tile_expert=tile_expert, tiles_used=tiles_used, name="moe_gate_up")
    y = _matmul(act, [w_down], out_dtype=F32, tm=tm, tn=_pick(d, (1024, 512, 256, 128)),
                tile_expert=tile_expert, tiles_used=tiles_used, name="moe_down")
    return _combine(dest1, dest2, y, x, info, gate, final_norm, seq=dims.seq, batch=dims.batch)


def _pad_cols(w, n):
    return w if w.shape[-1] == n else jnp.pad(w, [(0, 0)] * (w.ndim - 1) + [(0, n - w.shape[-1])])


def _round_up(n, m):
    return (n + m - 1) // m * m


def _forward(dims, x, c, ctx, c_ctx,
             ada_w0, ada_b0, norm_mix0, norm_ffn0, w_in0, a_q_norm, a_k_norm, b_q_norm, b_kv_norm,
             w_uq, w_ukv, w_o0, ffn_w_gate, ffn_w_up, ffn_w_down,
             ada_w1, ada_b1, norm_mix1, norm_ffn1, w_in1, c_sink, d_lam_q1, d_lam_k1, d_lam_q2, d_lam_k2,
             d_subln, w_o1, router_w, moe_w_gate, moe_w_up, moe_w_down, final_norm):
    dm, bsz, seq, ctx_len = dims.d_model, dims.batch, dims.seq, dims.ctx_len
    nl, nc = bsz * seq, bsz * ctx_len
    nt = nl + nc
    ha, hka, hb = dims.a_heads, dims.a_kv_heads, dims.b_heads
    hc, hkc, hd = dims.c_heads, dims.c_kv_heads, dims.d_heads
    big = lambda n: _pick(n, (1024, 512, 256, 128, 64, 32, 16))
    tm_all, tm_lat = big(math.gcd(nt, seq)), big(math.gcd(nl, seq))

    xa = jnp.concatenate([x.reshape(nl, dm), ctx.reshape(nc, dm)], axis=0)
    cond = jnp.concatenate([c, c_ctx[None, :], jnp.zeros((8 - bsz - 1, dm), F32)], axis=0)

    def mods(ada_w, ada_b):
        m = _modulation(cond, ada_w, ada_b).reshape(8, N_MOD, 1, dm)
        return [m[:, i] for i in range(N_MOD)]

    rope128 = _rope_tables(seq, dims.grid_w, 128, 256)
    rope64 = _rope_tables(seq, dims.grid_w, 64, 256)
    prep = functools.partial(_prep_heads, seq=seq, lat_rows=nl)
    flash = functools.partial(_flash, batch=bsz, seq=seq, ctx_len=ctx_len, lat_rows=nl)

    sh1, sc1, g1, sh2, sc2, g2 = mods(ada_w0, ada_b0)
    h = _norm_mod(xa, norm_mix0, sh1, sc1, rows=nt, seq=seq, batch=bsz)
    n_a = (ha + 2 * hka) * HEAD_DIM
    n_main = n_a + dims.b_q_rank + dims.b_kv_rank
    proj = _matmul(h, [w_in0[:, :n_main].astype(BF16)], out_dtype=F32, tm=tm_all,
                   tn=_pick(n_main, (512, 256, 128)), name="in_proj0")
    w_kr = _pad_cols(w_in0[:, n_main:], LANES).astype(BF16)
    kr = _matmul(h, [w_kr], out_dtype=F32, tm=tm_all, tn=LANES, name="in_proj0_rope_key")
    aq = prep(proj, 0, ha, rows=nt, gain=a_q_norm, rope=rope128, rot_dim=128, scale=HEAD_DIM ** -0.5, name="prep_aq")
    ak = prep(proj, ha, hka, rows=nt, gain=a_k_norm, rope=rope128, rot_dim=128, name="prep_ak")
    av = prep(proj, ha + hka, hka, rows=nt, name="prep_av")
    cq = _rank_norm(proj, n_a, dims.b_q_rank, b_q_norm)
    ckv = _rank_norm(proj, n_a + dims.b_q_rank, dims.b_kv_rank, b_kv_norm)
    b_scale = (HEAD_DIM + 64) ** -0.5
    uq = w_uq.reshape(dims.b_q_rank, hb, HEAD_DIM + 64) * b_scale
    uq_nope = uq[:, :, :HEAD_DIM].reshape(dims.b_q_rank, hb * HEAD_DIM)
    uq_rope = _pad_cols(uq[:, :, HEAD_DIM:], LANES).reshape(dims.b_q_rank, hb * LANES)
    bq = _matmul(cq, [jnp.concatenate([uq_nope, uq_rope], axis=1).astype(BF16)], out_dtype=F32, tm=tm_all,
                 tn=_pick(2 * hb * LANES, (512, 256)), name="b_up_q")
    ukv = w_ukv.reshape(dims.b_kv_rank, hb, 2 * HEAD_DIM)
    ukv = jnp.concatenate([ukv[:, :, :HEAD_DIM].reshape(dims.b_kv_rank, hb * HEAD_DIM),
                           ukv[:, :, HEAD_DIM:].reshape(dims.b_kv_rank, hb * HEAD_DIM)], axis=1)
    bkv = _matmul(ckv, [ukv.astype(BF16)], out_dtype=BF16, tm=tm_all, tn=_pick(2 * hb * LANES, (512, 256)),
                  name="b_up_kv")
    bq_nope = prep(bq, 0, hb, rows=nt, name="prep_bq_nope")
    bq_rope = prep(bq, hb, hb, rows=nt, rope=rope64, rot_dim=64, name="prep_bq_rope")
    bk_rope = prep(kr, 0, 1, rows=nt, rope=rope64, rot_dim=64, name="prep_bk_rope")

    a_lat = flash([(aq, 0)], [(ak, 0, 1)], (av, 0), n_heads=ha, group=ha // hka, name="attn_a")
    a_ctx = flash([(aq, 0)], [(ak, 0, 1)], (av, 0), n_heads=ha, group=ha // hka, ctx_queries=True, name="attn_a_ctx")
    b_args = ([(bq_nope, 0), (bq_rope, 0)], [(bkv, 0, 1), (bk_rope, 0, 0)], (bkv, hb))
    b_lat = flash(*b_args, n_heads=hb, group=1, name="attn_b")
    b_ctx = flash(*b_args, n_heads=hb, group=1, ctx_queries=True, name="attn_b_ctx")
    mixed = jnp.concatenate([jnp.concatenate([a_lat, b_lat], axis=1),
                             jnp.concatenate([a_ctx, b_ctx], axis=1)], axis=0)
    xa = _matmul(mixed, [w_o0.astype(BF16)], out_dtype=F32, tm=tm_all, tn=_pick(dm, (512, 256, 128)),
                 epi="gres", res=xa, gate=g1, seq=seq, batch=bsz, name="out_proj0")

    h = _norm_mod(xa, norm_ffn0, sh2, sc2, rows=nt, seq=seq, batch=bsz)
    f_pad = _round_up(dims.ffn_dim, 1024) if dims.ffn_dim > 1024 else _round_up(dims.ffn_dim, LANES)
    tf = _pick(f_pad, (512, 256, 128))
    act = _matmul(h, [_pad_cols(ffn_w_gate, f_pad).astype(BF16), _pad_cols(ffn_w_up, f_pad).astype(BF16)],
                  out_dtype=BF16, tm=tm_all, tn=tf, epi="swiglu", name="ffn_gate_up")
    w_down = jnp.pad(ffn_w_down, ((0, f_pad - dims.ffn_dim), (0, 0))).astype(BF16)
    tk_down = f_pad // 2 if (f_pad // 2) % LANES == 0 and f_pad > 2048 else f_pad
    xa = _matmul(act, [w_down], out_dtype=F32, tm=tm_all, tn=_pick(dm, (512, 256, 128)), tk=tk_down,
                 epi="gres", res=xa, gate=g2, seq=seq, batch=bsz, name="ffn_down")

    sh1, sc1, g1, sh2, sc2, g2 = mods(ada_w1, ada_b1)
    h = _norm_mod(xa, norm_mix1, sh1, sc1, rows=nt, seq=seq, batch=bsz)
    n_in1 = w_in1.shape[1]
    proj = _matmul(h, [w_in1.astype(BF16)], out_dtype=F32, tm=tm_all, tn=_pick(n_in1, (512, 256, 128)),
                   name="in_proj1")
    col = 0
    cq_ = prep(proj, col, hc, rows=nl, rope=rope128, rot_dim=128, scale=HEAD_DIM ** -0.5, name="prep_cq")
    col += hc
    ck_ = prep(proj, col, hkc, rows=nt, rope=rope128, rot_dim=128, name="prep_ck")
    col += hkc
    cv_ = prep(proj, col, hkc, rows=nt, name="prep_cv")
    col += hkc
    dq_ = prep(proj, col, hd, rows=nl, rope=rope64, rot_dim=64, scale=64 ** -0.5, name="prep_dq")
    col += hd
    dk_ = prep(proj, col, hd, rows=nt, rope=rope64, rot_dim=64, name="prep_dk")
    col += hd
    dv_ = prep(proj, col, hd, rows=nt, name="prep_dv")
    sink = jnp.broadcast_to(c_sink.astype(F32)[:, None, None], (hc, 1, LANES))
    lam = jnp.stack([_pad_cols(v.astype(F32)[None, :], LANES)[0] for v in (d_lam_q1, d_lam_k1, d_lam_q2, d_lam_k2)])
    c_out = flash([(cq_, 0)], [(ck_, 0, 1)], (cv_, 0), n_heads=hc, group=hc // hkc, mode="window", sink=sink,
                  window=dims.window, name="attn_c")
    d_out = flash([(dq_, 0)], [(dk_, 0, 1)], (dv_, 0), n_heads=hd, group=1, mode="diff", lam=lam,
                  subln=d_subln.reshape(1, LANES), name="attn_d")
    mixed = jnp.concatenate([c_out, d_out], axis=1)
    xl = _matmul(mixed, [w_o1.astype(BF16)], out_dtype=F32, tm=tm_lat, tn=_pick(dm, (512, 256, 128)),
                 epi="gres", res=xa, gate=g1, seq=seq, batch=bsz, rows=nl, name="out_proj1")

    rw = _pad_cols(router_w, LANES)
    rw_hi = rw.astype(BF16)
    rw_lo = (rw - rw_hi.astype(F32)).astype(BF16)
    hb16, hf32, logits = _norm_mod(xl, norm_ffn1, sh2, sc2, rows=nl, seq=seq, batch=bsz, router=(rw_hi, rw_lo))
    out = _moe(hb16, hf32, logits, xl, g2, final_norm, moe_w_gate.astype(BF16), moe_w_up.astype(BF16),
               moe_w_down.astype(BF16), dims=dims)
    return out.reshape(bsz, seq, dm)


def _rank_norm_kernel(x_ref, g_ref, o_ref):
    o_ref[...] = _norm_rows(x_ref[...], g_ref[...]).astype(o_ref.dtype)


def _rank_norm(src, col0, width, gain):
    rows = src.shape[0]
    tm = _pick(rows, (256, 128, 64, 32, 16))
    assert col0 % width == 0
    vmem = 2 * (_nbytes((tm, width), F32) + _nbytes((tm, width), BF16))
    return pl.pallas_call(
        _rank_norm_kernel,
        grid=(rows // tm,),
        in_specs=[pl.BlockSpec((tm, width), lambda i: (i, col0 // width)),
                  pl.BlockSpec((1, width), lambda i: (0, 0))],
        out_specs=pl.BlockSpec((tm, width), lambda i: (i, 0)),
        out_shape=jax.ShapeDtypeStruct((rows, width), BF16),
        compiler_params=_params(("parallel",), vmem),
        name="rank_norm",
    )(src, gain.reshape(1, width))


_DIMS = Dims(d_model=4096, batch=4, seq=4096, ctx_len=256, grid_w=64, a_heads=16, a_kv_heads=4, b_heads=16,
             b_q_rank=1536, b_kv_rank=512, c_heads=16, c_kv_heads=4, window=128, d_heads=16, ffn_dim=11008,
             n_experts=8, expert_dim=3584)


def kernel(x, c, ctx, c_ctx, ada_w0, ada_b0, norm_mix0, norm_ffn0, w_in0, a_q_norm, a_k_norm, b_q_norm, b_kv_norm, w_uq, w_ukv, w_o0, ffn_w_gate, ffn_w_up, ffn_w_down, ada_w1, ada_b1, norm_mix1, norm_ffn1, w_in1, c_sink, d_lam_q1, d_lam_k1, d_lam_q2, d_lam_k2, d_subln, w_o1, router_w, moe_w_gate, moe_w_up, moe_w_down, final_norm):
    return _forward(_DIMS, x, c, ctx, c_ctx, ada_w0, ada_b0, norm_mix0, norm_ffn0, w_in0, a_q_norm, a_k_norm,
                    b_q_norm, b_kv_norm, w_uq, w_ukv, w_o0, ffn_w_gate, ffn_w_up, ffn_w_down, ada_w1, ada_b1,
                    norm_mix1, norm_ffn1, w_in1, c_sink, d_lam_q1, d_lam_k1, d_lam_q2, d_lam_k2, d_subln, w_o1,
                    router_w, moe_w_gate, moe_w_up, moe_w_down, final_norm)
```

```python
import functools
import math
from typing import NamedTuple

import jax
import jax.numpy as jnp
from jax import lax
from jax.experimental import pallas as pl
from jax.experimental.pallas import tpu as pltpu

F32 = jnp.float32
BF16 = jnp.bfloat16

LANES = 128
HEAD_DIM = 128
ROPE_THETA = 10000.0
NORM_EPS = 1e-6
MASK_VALUE = -1e30
LOG2E = math.log2(math.e)
N_MOD = 6
D_LAYER_INDEX = 1
D_LAMBDA_INIT = 0.8 - 0.6 * math.exp(-0.3 * D_LAYER_INDEX)
VMEM_HEADROOM = 6 << 20


class Dims(NamedTuple):
    d_model: int
    batch: int
    seq: int
    ctx_len: int
    grid_w: int
    a_heads: int
    a_kv_heads: int
    b_heads: int
    b_q_rank: int
    b_kv_rank: int
    c_heads: int
    c_kv_heads: int
    window: int
    d_heads: int
    ffn_dim: int
    n_experts: int
    expert_dim: int


def _pick(n, prefs):
    for p in prefs:
        if n % p == 0:
            return p
    raise ValueError(f"no tile in {prefs} divides {n}")


def _params(sem, vmem_bytes):
    return pltpu.CompilerParams(dimension_semantics=sem, vmem_limit_bytes=int(vmem_bytes) + VMEM_HEADROOM)


def _nbytes(shape, dtype):
    return math.prod(shape) * jnp.dtype(dtype).itemsize


def _mod_kernel(c_ref, w_ref, b_ref, o_ref):
    c = c_ref[...]
    s = (c * jax.nn.sigmoid(c)).astype(BF16)
    o_ref[...] = jnp.dot(s, w_ref[...].astype(BF16), preferred_element_type=F32) + b_ref[...]


def _modulation(cond, ada_w, ada_b):
    rows, d = cond.shape
    n = ada_w.shape[1]
    tn = _pick(n, (512, 256, 128))
    vmem = 2 * (_nbytes((d, tn), F32) + _nbytes((rows, tn), F32) * 2) + _nbytes((rows, d), F32) * 2
    return pl.pallas_call(
        _mod_kernel,
        grid=(n // tn,),
        in_specs=[pl.BlockSpec((rows, d), lambda j: (0, 0)),
                  pl.BlockSpec((d, tn), lambda j: (0, j)),
                  pl.BlockSpec((1, tn), lambda j: (0, j))],
        out_specs=pl.BlockSpec((rows, tn), lambda j: (0, j)),
        out_shape=jax.ShapeDtypeStruct((rows, n), F32),
        compiler_params=_params(("parallel",), vmem),
        name="modulation",
    )(cond, ada_w, ada_b.reshape(1, n))


def _norm_rows(x, g):
    var = jnp.mean(x * x, axis=-1, keepdims=True)
    return x * lax.rsqrt(var + NORM_EPS) * g


def _norm_mod_kernel(x_ref, g_ref, sh_ref, sc_ref, o_ref):
    y = _norm_rows(x_ref[...], g_ref[...])
    o_ref[...] = (y * (1.0 + sc_ref[0]) + sh_ref[0]).astype(o_ref.dtype)


def _norm_mod_router_kernel(x_ref, g_ref, sh_ref, sc_ref, rhi_ref, rlo_ref, o_ref, of_ref, lg_ref):
    y = _norm_rows(x_ref[...], g_ref[...])
    h = y * (1.0 + sc_ref[0]) + sh_ref[0]
    of_ref[...] = h
    hi = h.astype(BF16)
    o_ref[...] = hi
    lo = (h - hi.astype(F32)).astype(BF16)
    lg_ref[...] = (jnp.dot(hi, rhi_ref[...], preferred_element_type=F32)
                   + jnp.dot(hi, rlo_ref[...], preferred_element_type=F32)
                   + jnp.dot(lo, rhi_ref[...], preferred_element_type=F32))


def _mod_index(tm, seq, batch):
    return lambda i: jnp.minimum((i * tm) // seq, batch)


def _norm_mod(x, g, shift, scale, *, rows, seq, batch, router=None):
    m, d = x.shape
    tm = _pick(math.gcd(rows, seq), (256, 128, 64, 32, 16))
    midx = _mod_index(tm, seq, batch)
    in_specs = [pl.BlockSpec((tm, d), lambda i: (i, 0)),
                pl.BlockSpec((1, d), lambda i: (0, 0)),
                pl.BlockSpec((1, 1, d), lambda i: (midx(i), 0, 0)),
                pl.BlockSpec((1, 1, d), lambda i: (midx(i), 0, 0))]
    vmem = 2 * (_nbytes((tm, d), F32) + _nbytes((tm, d), BF16)) + 6 * _nbytes((1, d), F32)
    if router is None:
        return pl.pallas_call(
            _norm_mod_kernel,
            grid=(rows // tm,),
            in_specs=in_specs,
            out_specs=pl.BlockSpec((tm, d), lambda i: (i, 0)),
            out_shape=jax.ShapeDtypeStruct((rows, d), BF16),
            compiler_params=_params(("parallel",), vmem),
            name="norm_modulate",
        )(x, g.reshape(1, d), shift, scale)
    rhi, rlo = router
    vmem += 2 * (_nbytes((tm, d), F32) + _nbytes((tm, LANES), F32)) + 4 * _nbytes((d, LANES), BF16)
    return pl.pallas_call(
        _norm_mod_router_kernel,
        grid=(rows // tm,),
        in_specs=in_specs + [pl.BlockSpec((d, LANES), lambda i: (0, 0)),
                             pl.BlockSpec((d, LANES), lambda i: (0, 0))],
        out_specs=[pl.BlockSpec((tm, d), lambda i: (i, 0)),
                   pl.BlockSpec((tm, d), lambda i: (i, 0)),
                   pl.BlockSpec((tm, LANES), lambda i: (i, 0))],
        out_shape=[jax.ShapeDtypeStruct((rows, d), BF16),
                   jax.ShapeDtypeStruct((rows, d), F32),
                   jax.ShapeDtypeStruct((rows, LANES), F32)],
        compiler_params=_params(("parallel",), vmem),
        name="norm_modulate_router",
    )(x, g.reshape(1, d), shift, scale, rhi, rlo)


def _mm_kernel(*refs, n_w, nk, epi, grouped):
    refs = list(refs)
    if grouped:
        used_ref = refs[1]
        refs = refs[2:]
    x_ref, w_refs = refs[0], refs[1:1 + n_w]
    pos = 1 + n_w
    if epi == "gres":
        res_ref, gate_ref = refs[pos], refs[pos + 1]
        pos += 2
    o_ref = refs[pos]
    acc_refs = refs[pos + 1:]
    k = pl.program_id(2)

    def epilogue(parts):
        if epi == "swiglu":
            g, u = parts
            out = g * jax.nn.sigmoid(g) * u
        elif epi == "gres":
            out = res_ref[...] + gate_ref[0] * parts[0]
        else:
            out = parts[0]
        o_ref[...] = out.astype(o_ref.dtype)

    def compute():
        x = x_ref[...]
        parts = [jnp.dot(x, (w[0] if grouped else w[...]), preferred_element_type=F32) for w in w_refs]
        if nk == 1:
            epilogue(parts)
            return

        @pl.when(k == 0)
        def _():
            for a, p in zip(acc_refs, parts):
                a[...] = p

        @pl.when(k > 0)
        def _():
            for a, p in zip(acc_refs, parts):
                a[...] += p

        @pl.when(k == nk - 1)
        def _():
            epilogue([a[...] for a in acc_refs])

    if grouped:
        live = pl.program_id(0) < used_ref[0]
        pl.when(live)(compute)

        @pl.when(jnp.logical_not(live))
        def _():
            o_ref[...] = jnp.zeros_like(o_ref)
    else:
        compute()


def _matmul(x, ws, *, out_dtype, tm, tn, tk=None, epi="none", res=None, gate=None, seq=None, batch=None,
            rows=None, tile_expert=None, tiles_used=None, name="matmul"):
    m, kdim = x.shape
    rows = m if rows is None else rows
    grouped = tile_expert is not None
    n = ws[0].shape[-1]
    tk = kdim if tk is None else tk
    nk = kdim // tk
    assert rows % tm == 0 and n % tn == 0 and kdim % tk == 0
    n_w = len(ws)

    def im(f):
        return (lambda i, j, k, te, nu: f(i, j, k, te)) if grouped else (lambda i, j, k: f(i, j, k, None))

    in_specs = [pl.BlockSpec((tm, tk), im(lambda i, j, k, te: (i, k)))]
    for _ in ws:
        if grouped:
            in_specs.append(pl.BlockSpec((1, tk, tn), im(lambda i, j, k, te: (te[i], k, j))))
        else:
            in_specs.append(pl.BlockSpec((tk, tn), im(lambda i, j, k, te: (k, j))))
    args = [x, *ws]
    vmem = 2 * (_nbytes((tm, tk), x.dtype) + n_w * _nbytes((tk, tn), ws[0].dtype) + _nbytes((tm, tn), out_dtype))
    vmem += 2 * n_w * _nbytes((tm, tn), F32)
    if epi == "gres":
        midx = _mod_index(tm, seq, batch)
        in_specs += [pl.BlockSpec((tm, tn), im(lambda i, j, k, te: (i, j))),
                     pl.BlockSpec((1, 1, tn), im(lambda i, j, k, te: (midx(i), 0, j)))]
        args += [res, gate]
        vmem += 2 * _nbytes((tm, tn), F32)
    scratch = [pltpu.VMEM((tm, tn), F32) for _ in range(n_w)] if nk > 1 else []
    vmem += len(scratch) * _nbytes((tm, tn), F32)
    grid = (rows // tm, n // tn, nk)
    out_spec = pl.BlockSpec((tm, tn), im(lambda i, j, k, te: (i, j)))
    kern = functools.partial(_mm_kernel, n_w=n_w, nk=nk, epi=epi, grouped=grouped)
    cp = _params(("parallel", "parallel", "arbitrary"), vmem)
    out_shape = jax.ShapeDtypeStruct((rows, n), out_dtype)
    if grouped:
        gs = pltpu.PrefetchScalarGridSpec(num_scalar_prefetch=2, grid=grid, in_specs=in_specs,
                                          out_specs=out_spec, scratch_shapes=scratch)
        return pl.pallas_call(kern, grid_spec=gs, out_shape=out_shape, compiler_params=cp, name=name)(
            tile_expert, tiles_used, *args)
    return pl.pallas_call(kern, grid=grid, in_specs=in_specs, out_specs=out_spec, out_shape=out_shape,
                          scratch_shapes=scratch, compiler_params=cp, name=name)(*args)


def _rope_tables(seq, grid_w, rot_dim, pad_rows):
    n_rows = seq // grid_w
    rows = jnp.repeat(jnp.arange(n_rows, dtype=F32), grid_w)
    cols = jnp.tile(jnp.arange(grid_w, dtype=F32), n_rows)
    n_freq = rot_dim // 4
    inv_freq = jnp.power(ROPE_THETA, -jnp.arange(n_freq, dtype=F32) / n_freq)
    ang = jnp.concatenate([rows[:, None] * inv_freq, cols[:, None] * inv_freq], axis=-1)
    cos, sin = jnp.cos(ang), jnp.sin(ang)
    zero = jnp.zeros_like(sin)
    reps = LANES // rot_dim
    cos_t = jnp.tile(jnp.concatenate([cos, cos], axis=-1), (1, reps))
    up_t = jnp.tile(jnp.concatenate([-sin, zero], axis=-1), (1, reps))
    down_t = jnp.tile(jnp.concatenate([zero, sin], axis=-1), (1, reps))
    ident = jnp.ones((pad_rows, LANES), F32)
    zpad = jnp.zeros((pad_rows, LANES), F32)
    return (jnp.concatenate([cos_t, ident]), jnp.concatenate([up_t, zpad]), jnp.concatenate([down_t, zpad]))


def _prep_kernel(*refs, n_heads, norm, rot_dim, scale):
    refs = list(refs)
    x_ref = refs.pop(0)
    gain = refs.pop(0)[...] if norm else None
    if rot_dim:
        cos, up, down = refs[0][...], refs[1][...], refs[2][...]
        half = rot_dim // 2
    o_ref = refs[-1]
    for h in range(n_heads):
        cols = slice(h * LANES, (h + 1) * LANES)
        x = x_ref[:, cols].astype(F32)
        if norm:
            x = _norm_rows(x, gain)
        if rot_dim:
            x = x * cos + pltpu.roll(x, LANES - half, 1) * up + pltpu.roll(x, half, 1) * down
        if scale != 1.0:
            x = x * scale
        o_ref[:, cols] = x.astype(o_ref.dtype)


def _prep_heads(src, col0, n_heads, *, rows, seq, lat_rows, gain=None, rope=None, rot_dim=0, scale=1.0,
                name="prep_heads"):
    tm = _pick(math.gcd(math.gcd(rows, seq), lat_rows), (256, 128, 64, 32, 16))
    hb = math.gcd(col0, n_heads)
    width = hb * LANES
    in_specs = [pl.BlockSpec((tm, width), lambda i, j: (i, col0 // hb + j))]
    args = [src]
    if gain is not None:
        in_specs.append(pl.BlockSpec((1, LANES), lambda i, j: (0, 0)))
        args.append(gain.reshape(1, LANES))
    if rot_dim:
        lat_tiles, seq_tiles = lat_rows // tm, seq // tm
        tab = lambda i, j: (jnp.where(i < lat_tiles, i % seq_tiles, seq_tiles), 0)
        in_specs += [pl.BlockSpec((tm, LANES), tab)] * 3
        args += list(rope)
    vmem = 2 * (_nbytes((tm, width), src.dtype) + _nbytes((tm, width), BF16) + 3 * _nbytes((tm, LANES), F32))
    vmem += 8 * _nbytes((tm, LANES), F32)
    return pl.pallas_call(
        functools.partial(_prep_kernel, n_heads=hb, norm=gain is not None, rot_dim=rot_dim, scale=scale),
        grid=(rows // tm, n_heads // hb),
        in_specs=in_specs,
        out_specs=pl.BlockSpec((tm, width), lambda i, j: (i, j)),
        out_shape=jax.ShapeDtypeStruct((rows, n_heads * LANES), BF16),
        compiler_params=_params(("parallel", "parallel"), vmem),
        name=name,
    )(*args)


def _values_t_kernel(x_ref, o_ref, *, n_heads):
    for h in range(n_heads):
        cols = slice(h * LANES, (h + 1) * LANES)
        o_ref[0, cols, :] = x_ref[:, cols].astype(F32).T.astype(o_ref.dtype)


def _values_t(src, col0, n_heads, *, chunk, row0, rows, name):
    hb = math.gcd(col0, n_heads)
    width = hb * LANES
    assert row0 % chunk == 0 and rows % chunk == 0
    vmem = 2 * (_nbytes((chunk, width), src.dtype) + _nbytes((chunk, width), BF16)) + 4 * _nbytes((chunk, LANES), F32)
    return pl.pallas_call(
        functools.partial(_values_t_kernel, n_heads=hb),
        grid=(rows // chunk, n_heads // hb),
        in_specs=[pl.BlockSpec((chunk, width), lambda i, j: (row0 // chunk + i, col0 // hb + j))],
        out_specs=pl.BlockSpec((1, width, chunk), lambda i, j: (i, j, 0)),
        out_shape=jax.ShapeDtypeStruct((rows // chunk, n_heads * LANES, chunk), BF16),
        compiler_params=_params(("parallel", "parallel"), vmem),
        name=name,
    )(src)


def _kv_chunk(seq):
    return _pick(seq // 2, (512, 256, 128))


def _qk(q, k):
    return lax.dot_general(q, k, (((1,), (1,)), ((), ())), preferred_element_type=F32)


def _cat(refs, rows=None):
    parts = [r[...] if rows is None else r[rows, :] for r in refs]
    return parts[0] if len(parts) == 1 else jnp.concatenate(parts, axis=-1)


def _flash_kernel(*refs, mode, n_q, has_lat, tq, tqs, tkc, seq, window):
    refs = list(refs)
    q_refs = [refs.pop(0) for _ in range(n_q)]
    kc_refs = [refs.pop(0) for _ in range(n_q)]
    vc_ref = refs.pop(0)
    if has_lat:
        kl_refs = [refs.pop(0) for _ in range(n_q)]
        vl_ref = refs.pop(0)
    if mode == "window":
        sink_ref = refs.pop(0)
    if mode == "diff":
        lam_ref, subln_ref = refs.pop(0), refs.pop(0)
    o_ref = refs.pop(0)
    n_state = 2 if mode == "diff" else 1
    n_sub = tq // tqs
    chain_ids = [(sub, st) for sub in range(n_sub) for st in range(n_state)]
    state_refs = {cid: tuple(refs[5 * i:5 * i + 3]) for i, cid in enumerate(chain_ids)}
    score_refs = {cid: tuple(refs[5 * i + 3:5 * i + 5]) for i, cid in enumerate(chain_ids)}
    qi = pl.program_id(2)

    def scores(k, only_sub=None):
        out = {}
        for sub, st in chain_ids:
            if only_sub is not None and sub != only_sub:
                continue
            q = _cat(q_refs, pl.ds(sub * tqs, tqs))
            if mode == "diff":
                lane = lax.broadcasted_iota(jnp.int32, q.shape, 1)
                keep = (lane < LANES // 2) if st == 0 else (lane >= LANES // 2)
                q = jnp.where(keep, q, jnp.zeros_like(q))
            out[sub, st] = _qk(k, q)
        return out

    def absorb(s_of, vt, *, first=False, mask=None):
        for cid, s in s_of.items():
            m_ref, l_ref, acc_ref = state_refs[cid]
            if mask is not None:
                s = jnp.where(mask, s, MASK_VALUE)
            s_max = jnp.max(s, axis=0, keepdims=True)
            if first and mode == "window":
                m_old = jnp.broadcast_to(sink_ref[0][:, :1], (1, tqs))
                m_new = jnp.maximum(m_old, s_max)
                l_old = jnp.exp2(m_old - m_new)
            elif first:
                m_new, l_old = s_max, None
            else:
                m_old = m_ref[...]
                m_new = jnp.maximum(m_old, s_max)
                alpha = jnp.exp2(m_old - m_new)
                l_old = alpha * l_ref[...]
            p = jnp.exp2(s - m_new)
            l_new = jnp.sum(p, axis=0, keepdims=True)
            pv = jnp.dot(vt, p.astype(vt.dtype), preferred_element_type=F32)
            m_ref[...] = m_new
            l_ref[...] = l_new if l_old is None else l_old + l_new
            acc_ref[...] = pv if first else alpha * acc_ref[...] + pv

    def lat_keys(c):
        return _cat(kl_refs, pl.ds(pl.multiple_of(c * tkc, tkc), tkc))

    def stash(slot, s_of):
        for cid, s in s_of.items():
            score_refs[cid][slot][...] = s

    def fetch(slot):
        return {cid: score_refs[cid][slot][...] for cid in chain_ids}

    s_ctx = scores(_cat(kc_refs))
    if has_lat and mode == "window":
        tkw = tqs + 2 * window
        blocks = []
        for sub in range(n_sub):
            q_pos0 = qi * tq + sub * tqs
            start = pl.multiple_of(jnp.clip(q_pos0 - window, 0, seq - tkw), LANES)
            kp = start + lax.broadcasted_iota(jnp.int32, (tkw, tqs), 0)
            qp = q_pos0 + lax.broadcasted_iota(jnp.int32, (tkw, tqs), 1)
            blocks.append((scores(_cat(kl_refs, pl.ds(start, tkw)), only_sub=sub), start // LANES,
                           jnp.abs(qp - kp) <= window))
        absorb(s_ctx, vc_ref[0], first=True)
        for s_of, first_chunk, mask in blocks:
            vt = jnp.concatenate([vl_ref[first_chunk + j] for j in range(tkw // LANES)], axis=-1)
            absorb(s_of, vt, mask=mask)
    elif has_lat:
        n_chunks = seq // tkc
        stash(0, scores(lat_keys(0)))
        absorb(s_ctx, vc_ref[0], first=True)

        def body(i, carry):
            c = 2 * i
            stash(1, scores(lat_keys(c + 1)))
            absorb(fetch(0), vl_ref[c])
            stash(0, scores(lat_keys(c + 2)))
            absorb(fetch(1), vl_ref[c + 1])
            return carry

        lax.fori_loop(0, n_chunks // 2 - 1, body, 0)
        stash(1, scores(lat_keys(n_chunks - 1)))
        absorb(fetch(0), vl_ref[n_chunks - 2])
        absorb(fetch(1), vl_ref[n_chunks - 1])
    else:
        absorb(s_ctx, vc_ref[0], first=True)

    for sub in range(n_sub):
        rows = pl.ds(sub * tqs, tqs)
        outs = [(state_refs[sub, st][2][...] / state_refs[sub, st][1][...]).T for st in range(n_state)]
        if mode == "diff":
            lam_rows = lam_ref[...]
            lam = (jnp.exp(jnp.sum(lam_rows[0:1] * lam_rows[1:2], axis=-1, keepdims=True))
                   - jnp.exp(jnp.sum(lam_rows[2:3] * lam_rows[3:4], axis=-1, keepdims=True)) + D_LAMBDA_INIT)
            out = _norm_rows(outs[0] - lam * outs[1], subln_ref[...]) * (1.0 - D_LAMBDA_INIT)
        else:
            out = outs[0]
        o_ref[rows, :] = out.astype(o_ref.dtype)


def _flash(q_parts, k_parts, v, *, n_heads, group, batch, seq, ctx_len, lat_rows, ctx_queries=False,
           mode="softmax", sink=None, lam=None, subln=None, window=0, name="flash"):
    has_lat = not ctx_queries
    ctx_blk0 = lat_rows // ctx_len
    if ctx_queries:
        tq, nq = ctx_len, 1
        q_row = lambda b, h, i: ctx_blk0 + b
        out_rows = batch * ctx_len
        o_row = lambda b, h, i: b
    else:
        tq = _pick(seq, (512, 256, 128))
        nq = seq // tq
        q_row = lambda b, h, i: b * nq + i
        out_rows = lat_rows
        o_row = q_row
    tqs = min(tq, 256)
    tkc = _kv_chunk(seq)
    n_state = 2 if mode == "diff" else 1
    if mode == "window":
        assert seq >= tqs + 2 * window and window % 16 == 0
    n_q = len(q_parts)
    in_specs, args = [], []
    vmem = 0

    def add(arr, block, imap):
        nonlocal vmem
        in_specs.append(pl.BlockSpec(block, imap))
        args.append(arr)
        vmem += 2 * _nbytes(block, arr.dtype)

    for arr, c0 in q_parts:
        add(arr, (tq, LANES), lambda b, h, i, c0=c0: (q_row(b, h, i), c0 + h))
    for arr, c0, per_head in k_parts:
        add(arr, (ctx_len, LANES), lambda b, h, i, c0=c0, ph=per_head: (ctx_blk0 + b, c0 + (h // group) * ph))
    vt_lat, vt_ctx = v
    add(vt_ctx, (1, LANES, ctx_len), lambda b, h, i: (b, h // group, 0))
    if has_lat:
        for arr, c0, per_head in k_parts:
            add(arr, (seq, LANES), lambda b, h, i, c0=c0, ph=per_head: (b, c0 + (h // group) * ph))
        tkv = vt_lat.shape[-1]
        assert tkv == (LANES if mode == "window" else tkc)
        add(vt_lat, (seq // tkv, LANES, tkv), lambda b, h, i: (b, h // group, 0))
    if mode == "window":
        add(sink, (1, 1, LANES), lambda b, h, i: (h, 0, 0))
    if mode == "diff":
        add(lam, (4, LANES), lambda b, h, i: (0, 0))
        add(subln, (1, LANES), lambda b, h, i: (0, 0))
    n_sub = tq // tqs
    scratch = [pltpu.VMEM((1, tqs), F32), pltpu.VMEM((1, tqs), F32), pltpu.VMEM((LANES, tqs), F32),
               pltpu.VMEM((tkc, tqs), F32), pltpu.VMEM((tkc, tqs), F32)] * (n_state * n_sub)
    vmem += 2 * n_state * n_sub * _nbytes((tkc, tqs), F32)
    vmem += 2 * _nbytes((tq, LANES), BF16) + 3 * _nbytes((n_state, tq, LANES), F32)
    vmem += 4 * n_state * (tq // tqs) * _nbytes((tqs, max(tkc, ctx_len, tqs + 2 * window)), F32)
    kern = functools.partial(_flash_kernel, mode=mode, n_q=n_q, has_lat=has_lat, tq=tq, tqs=tqs, tkc=tkc,
                             seq=seq, window=window)
    return pl.pallas_call(
        kern,
        grid=(batch, n_heads, nq),
        in_specs=in_specs,
        out_specs=pl.BlockSpec((tq, LANES), lambda b, h, i: (o_row(b, h, i), h)),
        out_shape=jax.ShapeDtypeStruct((out_rows, n_heads * LANES), BF16),
        scratch_shapes=scratch,
        compiler_params=_params(("parallel", "parallel", "parallel"), vmem),
        name=name,
    )(*args)


def _route_kernel(lg_ref, info_ref, cnt_ref, base_ref, *, n_experts, tb):
    step = pl.program_id(0)

    @pl.when(step == 0)
    def _():
        base_ref[...] = jnp.zeros_like(base_ref)

    lane = lax.broadcasted_iota(jnp.int32, (tb, LANES), 1).astype(F32)
    logits = jnp.where(lane < n_experts, lg_ref[...], -jnp.inf)
    v1 = jnp.max(logits, axis=-1, keepdims=True)
    i1 = jnp.min(jnp.where(logits == v1, lane, float(LANES)), axis=-1, keepdims=True)
    hot1 = lane == i1
    rest = jnp.where(hot1, -jnp.inf, logits)
    v2 = jnp.max(rest, axis=-1, keepdims=True)
    i2 = jnp.min(jnp.where(rest == v2, lane, float(LANES)), axis=-1, keepdims=True)
    hot2 = lane == i2
    e = jnp.exp(v2 - v1)
    w1 = 1.0 / (1.0 + e)
    w2 = e / (1.0 + e)
    sel = jnp.where(hot1 | hot2, 1.0, 0.0)
    r = lax.broadcasted_iota(jnp.int32, (tb, tb), 0)
    c = lax.broadcasted_iota(jnp.int32, (tb, tb), 1)
    tri = jnp.where(c < r, 1.0, 0.0).astype(BF16)
    before = jnp.dot(tri, sel.astype(BF16), preferred_element_type=F32) + base_ref[...]
    rank1 = jnp.sum(jnp.where(hot1, before, 0.0), axis=-1, keepdims=True)
    rank2 = jnp.sum(jnp.where(hot2, before, 0.0), axis=-1, keepdims=True)
    base_ref[...] += jnp.sum(sel, axis=0, keepdims=True)
    cnt_ref[...] = jnp.broadcast_to(base_ref[...], cnt_ref.shape)
    info = jnp.zeros((tb, LANES), F32)
    for slot, val in enumerate((i1, i2, rank1, rank2, w1, w2)):
        info = jnp.where(lane == float(slot), val, info)
    info_ref[...] = info


def _route(logits, n_experts):
    n = logits.shape[0]
    tb = _pick(n, (256, 128, 64, 32, 16, 8))
    vmem = 4 * _nbytes((tb, LANES), F32) + 16 * _nbytes((tb, max(tb, LANES)), F32)
    return pl.pallas_call(
        functools.partial(_route_kernel, n_experts=n_experts, tb=tb),
        grid=(n // tb,),
        in_specs=[pl.BlockSpec((tb, LANES), lambda i: (i, 0))],
        out_specs=[pl.BlockSpec((tb, LANES), lambda i: (i, 0)),
                   pl.BlockSpec((8, LANES), lambda i: (0, 0))],
        out_shape=[jax.ShapeDtypeStruct((n, LANES), F32), jax.ShapeDtypeStruct((8, LANES), F32)],
        scratch_shapes=[pltpu.VMEM((1, LANES), F32)],
        compiler_params=_params(("arbitrary",), vmem),
        name="moe_route",
    )(logits)


def _row_copy(src_hbm, row, dst_vmem, slot, sem):
    return pltpu.make_async_copy(src_hbm.at[pl.ds(row, 1)], dst_vmem.at[pl.ds(slot, 1)], sem)


def _gather_kernel(src_ref, h_hbm, o_ref, buf, sem, *, tm):
    base = pl.program_id(0) * tm

    def issue(r, carry):
        _row_copy(h_hbm, src_ref[base + r], buf, r, sem).start()
        return carry

    lax.fori_loop(0, tm, issue, 0)

    def drain(r, carry):
        _row_copy(h_hbm, 0, buf, r, sem).wait()
        return carry

    lax.fori_loop(0, tm, drain, 0)
    o_ref[...] = buf[...].astype(o_ref.dtype)


def _gather_rows(src_rows, h, *, tm):
    p = src_rows.shape[0]
    d = h.shape[1]
    vmem = _nbytes((tm, d), F32) + 2 * _nbytes((tm, d), BF16)
    gs = pltpu.PrefetchScalarGridSpec(
        num_scalar_prefetch=1, grid=(p // tm,),
        in_specs=[pl.BlockSpec(memory_space=pl.ANY)],
        out_specs=pl.BlockSpec((tm, d), lambda i, src: (i, 0)),
        scratch_shapes=[pltpu.VMEM((tm, d), F32), pltpu.SemaphoreType.DMA(())])
    return pl.pallas_call(
        functools.partial(_gather_kernel, tm=tm), grid_spec=gs,
        out_shape=jax.ShapeDtypeStruct((p, d), BF16),
        compiler_params=_params(("arbitrary",), vmem),
        name="moe_gather",
    )(src_rows, h)


def _combine_kernel(d1_ref, d2_ref, y_hbm, x_ref, info_ref, gate_ref, g_ref, o_ref, buf1, buf2, sem, *, tb):
    base = pl.program_id(0) * tb

    def issue(r, carry):
        _row_copy(y_hbm, d1_ref[base + r], buf1, r, sem.at[0]).start()
        _row_copy(y_hbm, d2_ref[base + r], buf2, r, sem.at[1]).start()
        return carry

    lax.fori_loop(0, tb, issue, 0)

    def drain(r, carry):
        _row_copy(y_hbm, 0, buf1, r, sem.at[0]).wait()
        _row_copy(y_hbm, 0, buf2, r, sem.at[1]).wait()
        return carry

    lax.fori_loop(0, tb, drain, 0)
    info = info_ref[...]
    w1, w2 = info[:, 4:5], info[:, 5:6]
    x = x_ref[...] + gate_ref[0] * (w1 * buf1[...] + w2 * buf2[...])
    o_ref[...] = _norm_rows(x, g_ref[...])


def _combine(dest1, dest2, y, x, info, gate, final_norm, *, seq, batch):
    n, d = x.shape
    tb = _pick(seq, (256, 128, 64, 32, 16, 8))
    midx = _mod_index(tb, seq, batch)
    vmem = 2 * _nbytes((tb, d), F32) * 3 + 2 * _nbytes((tb, d), F32) + 2 * _nbytes((tb, LANES), F32)
    gs = pltpu.PrefetchScalarGridSpec(
        num_scalar_prefetch=2, grid=(n // tb,),
        in_specs=[pl.BlockSpec(memory_space=pl.ANY),
                  pl.BlockSpec((tb, d), lambda i, a, b: (i, 0)),
                  pl.BlockSpec((tb, LANES), lambda i, a, b: (i, 0)),
                  pl.BlockSpec((1, 1, d), lambda i, a, b: (midx(i), 0, 0)),
                  pl.BlockSpec((1, d), lambda i, a, b: (0, 0))],
        out_specs=pl.BlockSpec((tb, d), lambda i, a, b: (i, 0)),
        scratch_shapes=[pltpu.VMEM((tb, d), F32), pltpu.VMEM((tb, d), F32), pltpu.SemaphoreType.DMA((2,))])
    return pl.pallas_call(
        functools.partial(_combine_kernel, tb=tb), grid_spec=gs,
        out_shape=jax.ShapeDtypeStruct((n, d), F32),
        compiler_params=_params(("arbitrary",), vmem),
        name="moe_combine_norm",
    )(dest1, dest2, y, x, info, gate, final_norm.reshape(1, d))


def _moe(h_bf16_unused, h_f32, logits, x, gate, final_norm, w_gate, w_up, w_down, *, dims):
    n, d = h_f32.shape
    n_exp = dims.n_experts
    tm = _pick(n, (512, 256, 128, 64, 32, 16))
    p = 2 * n + n_exp * tm
    info, counts = _route(logits, n_exp)
    cnt = counts[0, :n_exp].astype(jnp.int32)
    padded = ((cnt + tm - 1) // tm) * tm
    ends = jnp.cumsum(padded)
    starts = ends - padded
    e1, e2 = info[:, 0].astype(jnp.int32), info[:, 1].astype(jnp.int32)
    dest1 = starts[e1] + info[:, 2].astype(jnp.int32)
    dest2 = starts[e2] + info[:, 3].astype(jnp.int32)
    token = jnp.arange(n, dtype=jnp.int32)
    src = jnp.zeros((p,), jnp.int32).at[dest1].set(token).at[dest2].set(token)
    tile_start = jnp.arange(p // tm, dtype=jnp.int32) * tm
    tile_expert = jnp.minimum(jnp.sum(tile_start[:, None] >= ends[None, :], axis=1), n_exp - 1).astype(jnp.int32)
    tiles_used = (ends[-1:] // tm).astype(jnp.int32)

    xs = _gather_rows(src, h_f32, tm=_pick(tm, (256, 128, 64, 32, 16)))
    f = w_gate.shape[-1]
    act = _matmul(xs, [w_gate, w_up], out_dtype=BF16, tm=tm, tn=_pick(f, (512, 256, 128)), epi="swiglu",
                  tile_expert=tile_expert, tiles_used=tiles_used, name="moe_gate_up")
    y = _matmul(act, [w_down], out_dtype=F32, tm=tm, tn=_pick(d, (1024, 512, 256, 128)),
                tile_expert=tile_expert, tiles_used=tiles_used, name="moe_down")
    return _combine(dest1, dest2, y, x, info, gate, final_norm, seq=dims.seq, batch=dims.batch)


def _pad_cols(w, n):
    return w if w.shape[-1] == n else jnp.pad(w, [(0, 0)] * (w.ndim - 1) + [(0, n - w.shape[-1])])


def _round_up(n, m):
    return (n + m - 1) // m * m


def _forward(dims, x, c, ctx, c_ctx,
             ada_w0, ada_b0, norm_mix0, norm_ffn0, w_in0, a_q_norm, a_k_norm, b_q_norm, b_kv_norm,
             w_uq, w_ukv, w_o0, ffn_w_gate, ffn_w_up, ffn_w_down,
             ada_w1, ada_b1, norm_mix1, norm_ffn1, w_in1, c_sink, d_lam_q1, d_lam_k1, d_lam_q2, d_lam_k2,
             d_subln, w_o1, router_w, moe_w_gate, moe_w_up, moe_w_down, final_norm):
    dm, bsz, seq, ctx_len = dims.d_model, dims.batch, dims.seq, dims.ctx_len
    nl, nc = bsz * seq, bsz * ctx_len
    nt = nl + nc
    ha, hka, hb = dims.a_heads, dims.a_kv_heads, dims.b_heads
    hc, hkc, hd = dims.c_heads, dims.c_kv_heads, dims.d_heads
    big = lambda n: _pick(n, (1024, 512, 256, 128, 64, 32, 16))
    tm_all, tm_lat = big(math.gcd(nt, seq)), big(math.gcd(nl, seq))

    xa = jnp.concatenate([x.reshape(nl, dm), ctx.reshape(nc, dm)], axis=0)
    cond = jnp.concatenate([c, c_ctx[None, :], jnp.zeros((8 - bsz - 1, dm), F32)], axis=0)

    def mods(ada_w, ada_b):
        m = _modulation(cond, ada_w, ada_b).reshape(8, N_MOD, 1, dm)
        return [m[:, i] for i in range(N_MOD)]

    rope128 = _rope_tables(seq, dims.grid_w, 128, 256)
    rope64 = _rope_tables(seq, dims.grid_w, 64, 256)
    prep = functools.partial(_prep_heads, seq=seq, lat_rows=nl)
    flash = functools.partial(_flash, batch=bsz, seq=seq, ctx_len=ctx_len, lat_rows=nl)

    def values_t(src, col0, n_heads, name, chunk=_kv_chunk(seq)):
        return (_values_t(src, col0, n_heads, chunk=chunk, row0=0, rows=nl, name=name),
                _values_t(src, col0, n_heads, chunk=ctx_len, row0=nl, rows=nc, name=name + "_ctx"))

    sh1, sc1, g1, sh2, sc2, g2 = mods(ada_w0, ada_b0)
    h = _norm_mod(xa, norm_mix0, sh1, sc1, rows=nt, seq=seq, batch=bsz)
    n_a = (ha + 2 * hka) * HEAD_DIM
    n_main = n_a + dims.b_q_rank + dims.b_kv_rank
    proj = _matmul(h, [w_in0[:, :n_main].astype(BF16)], out_dtype=F32, tm=tm_all,
                   tn=_pick(n_main, (512, 256, 128)), name="in_proj0")
    w_kr = _pad_cols(w_in0[:, n_main:], LANES).astype(BF16)
    kr = _matmul(h, [w_kr], out_dtype=F32, tm=tm_all, tn=LANES, name="in_proj0_rope_key")
    aq = prep(proj, 0, ha, rows=nt, gain=a_q_norm, rope=rope128, rot_dim=128, scale=LOG2E * HEAD_DIM ** -0.5, name="prep_aq")
    ak = prep(proj, ha, hka, rows=nt, gain=a_k_norm, rope=rope128, rot_dim=128, name="prep_ak")
    av = values_t(proj, ha + hka, hka, "vt_a")
    cq = _rank_norm(proj, n_a, dims.b_q_rank, b_q_norm)
    ckv = _rank_norm(proj, n_a + dims.b_q_rank, dims.b_kv_rank, b_kv_norm)
    b_scale = LOG2E * (HEAD_DIM + 64) ** -0.5
    uq = w_uq.reshape(dims.b_q_rank, hb, HEAD_DIM + 64) * b_scale
    uq_nope = uq[:, :, :HEAD_DIM].reshape(dims.b_q_rank, hb * HEAD_DIM)
    uq_rope = _pad_cols(uq[:, :, HEAD_DIM:], LANES).reshape(dims.b_q_rank, hb * LANES)
    bq = _matmul(cq, [jnp.concatenate([uq_nope, uq_rope], axis=1).astype(BF16)], out_dtype=F32, tm=tm_all,
                 tn=_pick(2 * hb * LANES, (512, 256)), name="b_up_q")
    ukv = w_ukv.reshape(dims.b_kv_rank, hb, 2 * HEAD_DIM)
    ukv = jnp.concatenate([ukv[:, :, :HEAD_DIM].reshape(dims.b_kv_rank, hb * HEAD_DIM),
                           ukv[:, :, HEAD_DIM:].reshape(dims.b_kv_rank, hb * HEAD_DIM)], axis=1)
    bkv = _matmul(ckv, [ukv.astype(BF16)], out_dtype=BF16, tm=tm_all, tn=_pick(2 * hb * LANES, (512, 256)),
                  name="b_up_kv")
    bq_nope = prep(bq, 0, hb, rows=nt, name="prep_bq_nope")
    bq_rope = prep(bq, hb, hb, rows=nt, rope=rope64, rot_dim=64, name="prep_bq_rope")
    bk_rope = prep(kr, 0, 1, rows=nt, rope=rope64, rot_dim=64, name="prep_bk_rope")

    a_lat = flash([(aq, 0)], [(ak, 0, 1)], av, n_heads=ha, group=ha // hka, name="attn_a")
    a_ctx = flash([(aq, 0)], [(ak, 0, 1)], av, n_heads=ha, group=ha // hka, ctx_queries=True, name="attn_a_ctx")
    b_args = ([(bq_nope, 0), (bq_rope, 0)], [(bkv, 0, 1), (bk_rope, 0, 0)], values_t(bkv, hb, hb, "vt_b"))
    b_lat = flash(*b_args, n_heads=hb, group=1, name="attn_b")
    b_ctx = flash(*b_args, n_heads=hb, group=1, ctx_queries=True, name="attn_b_ctx")
    mixed = jnp.concatenate([jnp.concatenate([a_lat, b_lat], axis=1),
                             jnp.concatenate([a_ctx, b_ctx], axis=1)], axis=0)
    xa = _matmul(mixed, [w_o0.astype(BF16)], out_dtype=F32, tm=tm_all, tn=_pick(dm, (512, 256, 128)),
                 epi="gres", res=xa, gate=g1, seq=seq, batch=bsz, name="out_proj0")

    h = _norm_mod(xa, norm_ffn0, sh2, sc2, rows=nt, seq=seq, batch=bsz)
    f_pad = _round_up(dims.ffn_dim, 1024) if dims.ffn_dim > 1024 else _round_up(dims.ffn_dim, LANES)
    tf = _pick(f_pad, (512, 256, 128))
    act = _matmul(h, [_pad_cols(ffn_w_gate, f_pad).astype(BF16), _pad_cols(ffn_w_up, f_pad).astype(BF16)],
                  out_dtype=BF16, tm=tm_all, tn=tf, epi="swiglu", name="ffn_gate_up")
    w_down = jnp.pad(ffn_w_down, ((0, f_pad - dims.ffn_dim), (0, 0))).astype(BF16)
    tk_down = f_pad // 2 if (f_pad // 2) % LANES == 0 and f_pad > 2048 else f_pad
    xa = _matmul(act, [w_down], out_dtype=F32, tm=tm_all, tn=_pick(dm, (512, 256, 128)), tk=tk_down,
                 epi="gres", res=xa, gate=g2, seq=seq, batch=bsz, name="ffn_down")

    sh1, sc1, g1, sh2, sc2, g2 = mods(ada_w1, ada_b1)
    h = _norm_mod(xa, norm_mix1, sh1, sc1, rows=nt, seq=seq, batch=bsz)
    n_in1 = w_in1.shape[1]
    proj = _matmul(h, [w_in1.astype(BF16)], out_dtype=F32, tm=tm_all, tn=_pick(n_in1, (512, 256, 128)),
                   name="in_proj1")
    col = 0
    cq_ = prep(proj, col, hc, rows=nl, rope=rope128, rot_dim=128, scale=LOG2E * HEAD_DIM ** -0.5, name="prep_cq")
    col += hc
    ck_ = prep(proj, col, hkc, rows=nt, rope=rope128, rot_dim=128, name="prep_ck")
    col += hkc
    cv_ = values_t(proj, col, hkc, "vt_c", chunk=LANES)
    col += hkc
    dq_ = prep(proj, col, hd, rows=nl, rope=rope64, rot_dim=64, scale=LOG2E * 64 ** -0.5, name="prep_dq")
    col += hd
    dk_ = prep(proj, col, hd, rows=nt, rope=rope64, rot_dim=64, name="prep_dk")
    col += hd
    dv_ = values_t(proj, col, hd, "vt_d")
    sink = jnp.broadcast_to(LOG2E * c_sink.astype(F32)[:, None, None], (hc, 1, LANES))
    lam = jnp.stack([_pad_cols(v.astype(F32)[None, :], LANES)[0] for v in (d_lam_q1, d_lam_k1, d_lam_q2, d_lam_k2)])
    c_out = flash([(cq_, 0)], [(ck_, 0, 1)], cv_, n_heads=hc, group=hc // hkc, mode="window", sink=sink,
                  window=dims.window, name="attn_c")
    d_out = flash([(dq_, 0)], [(dk_, 0, 1)], dv_, n_heads=hd, group=1, mode="diff", lam=lam,
                  subln=d_subln.reshape(1, LANES), name="attn_d")
    mixed = jnp.concatenate([c_out, d_out], axis=1)
    xl = _matmul(mixed, [w_o1.astype(BF16)], out_dtype=F32, tm=tm_lat, tn=_pick(dm, (512, 256, 128)),
                 epi="gres", res=xa, gate=g1, seq=seq, batch=bsz, rows=nl, name="out_proj1")

    rw = _pad_cols(router_w, LANES)
    rw_hi = rw.astype(BF16)
    rw_lo = (rw - rw_hi.astype(F32)).astype(BF16)
    hb16, hf32, logits = _norm_mod(xl, norm_ffn1, sh2, sc2, rows=nl, seq=seq, batch=bsz, router=(rw_hi, rw_lo))
    out = _moe(hb16, hf32, logits, xl, g2, final_norm, moe_w_gate.astype(BF16), moe_w_up.astype(BF16),
               moe_w_down.astype(BF16), dims=dims)
    return out.reshape(bsz, seq, dm)


def _rank_norm_kernel(x_ref, g_ref, o_ref):
    o_ref[...] = _norm_rows(x_ref[...], g_ref[...]).astype(o_ref.dtype)


def _rank_norm(src, col0, width, gain):
    rows = src.shape[0]
    tm = _pick(rows, (256, 128, 64, 32, 16))
    assert col0 % width == 0
    vmem = 2 * (_nbytes((tm, width), F32) + _nbytes((tm, width), BF16))
    return pl.pallas_call(
        _rank_norm_kernel,
        grid=(rows // tm,),
        in_specs=[pl.BlockSpec((tm, width), lambda i: (i, col0 // width)),
                  pl.BlockSpec((1, width), lambda i: (0, 0))],
        out_specs=pl.BlockSpec((tm, width), lambda i: (i, 0)),
        out_shape=jax.ShapeDtypeStruct((rows, width), BF16),
        compiler_params=_params(("parallel",), vmem),
        name="rank_norm",
    )(src, gain.reshape(1, width))


_DIMS = Dims(d_model=4096, batch=4, seq=4096, ctx_len=256, grid_w=64, a_heads=16, a_kv_heads=4, b_heads=16,
             b_q_rank=1536, b_kv_rank=512, c_heads=16, c_kv_heads=4, window=128, d_heads=16, ffn_dim=11008,
             n_experts=8, expert_dim=3584)


def kernel(x, c, ctx, c_ctx, ada_w0, ada_b0, norm_mix0, norm_ffn0, w_in0, a_q_norm, a_k_norm, b_q_norm, b_kv_norm, w_uq, w_ukv, w_o0, ffn_w_gate, ffn_w_up, ffn_w_down, ada_w1, ada_b1, norm_mix1, norm_ffn1, w_in1, c_sink, d_lam_q1, d_lam_k1, d_lam_q2, d_lam_k2, d_subln, w_o1, router_w, moe_w_gate, moe_w_up, moe_w_down, final_norm):
    return _forward(_DIMS, x, c, ctx, c_ctx, ada_w0, ada_b0, norm_mix0, norm_ffn0, w_in0, a_q_norm, a_k_norm,
                    b_q_norm, b_kv_norm, w_uq, w_ukv, w_o0, ffn_w_gate, ffn_w_up, ffn_w_down, ada_w1, ada_b1,
                    norm_mix1, norm_ffn1, w_in1, c_sink, d_lam_q1, d_lam_k1, d_lam_q2, d_lam_k2, d_subln, w_o1,
                    router_w, moe_w_gate, moe_w_up, moe_w_down, final_norm)
```

```python
import functools
import math
from typing import NamedTuple

import jax
import jax.numpy as jnp
from jax import lax
from jax.experimental import pallas as pl
from jax.experimental.pallas import tpu as pltpu

F32 = jnp.float32
BF16 = jnp.bfloat16

LANES = 128
HEAD_DIM = 128
ROPE_THETA = 10000.0
NORM_EPS = 1e-6
MASK_VALUE = -1e30
LOG2E = math.log2(math.e)
N_MOD = 6
D_LAYER_INDEX = 1
D_LAMBDA_INIT = 0.8 - 0.6 * math.exp(-0.3 * D_LAYER_INDEX)
VMEM_HEADROOM = 6 << 20
DMA_LOOP_UNROLL = 8
HEAD_SUBBLOCK = 256


class Dims(NamedTuple):
    d_model: int
    batch: int
    seq: int
    ctx_len: int
    grid_w: int
    a_heads: int
    a_kv_heads: int
    b_heads: int
    b_q_rank: int
    b_kv_rank: int
    c_heads: int
    c_kv_heads: int
    window: int
    d_heads: int
    ffn_dim: int
    n_experts: int
    expert_dim: int


def _pick(n, prefs):
    for p in prefs:
        if n % p == 0:
            return p
    raise ValueError(f"no tile in {prefs} divides {n}")


def _params(sem, vmem_bytes):
    return pltpu.CompilerParams(dimension_semantics=sem, vmem_limit_bytes=int(vmem_bytes) + VMEM_HEADROOM)


def _nbytes(shape, dtype):
    return math.prod(shape) * jnp.dtype(dtype).itemsize


def _mod_kernel(c_ref, w_ref, b_ref, o_ref):
    c = c_ref[...]
    s = (c * jax.nn.sigmoid(c)).astype(BF16)
    o_ref[...] = jnp.dot(s, w_ref[...].astype(BF16), preferred_element_type=F32) + b_ref[...]


def _modulation(cond, ada_w, ada_b):
    rows, d = cond.shape
    n = ada_w.shape[1]
    tn = _pick(n, (512, 256, 128))
    vmem = 2 * (_nbytes((d, tn), F32) + _nbytes((rows, tn), F32) * 2) + _nbytes((rows, d), F32) * 2
    return pl.pallas_call(
        _mod_kernel,
        grid=(n // tn,),
        in_specs=[pl.BlockSpec((rows, d), lambda j: (0, 0)),
                  pl.BlockSpec((d, tn), lambda j: (0, j)),
                  pl.BlockSpec((1, tn), lambda j: (0, j))],
        out_specs=pl.BlockSpec((rows, tn), lambda j: (0, j)),
        out_shape=jax.ShapeDtypeStruct((rows, n), F32),
        compiler_params=_params(("parallel",), vmem),
        name="modulation",
    )(cond, ada_w, ada_b.reshape(1, n))


def _norm_rows(x, g):
    var = jnp.mean(x * x, axis=-1, keepdims=True)
    return x * lax.rsqrt(var + NORM_EPS) * g


def _norm_mod_kernel(x_ref, g_ref, sh_ref, sc_ref, o_ref):
    y = _norm_rows(x_ref[...], g_ref[...])
    o_ref[...] = (y * (1.0 + sc_ref[0]) + sh_ref[0]).astype(o_ref.dtype)


def _norm_mod_router_kernel(x_ref, g_ref, sh_ref, sc_ref, rhi_ref, rlo_ref, of_ref, lg_ref):
    y = _norm_rows(x_ref[...], g_ref[...])
    h = y * (1.0 + sc_ref[0]) + sh_ref[0]
    of_ref[...] = h
    hi = h.astype(BF16)
    lo = (h - hi.astype(F32)).astype(BF16)
    lg_ref[...] = (jnp.dot(hi, rhi_ref[...], preferred_element_type=F32)
                   + jnp.dot(hi, rlo_ref[...], preferred_element_type=F32)
                   + jnp.dot(lo, rhi_ref[...], preferred_element_type=F32))


def _mod_index(tm, seq, batch):
    return lambda i: jnp.minimum((i * tm) // seq, batch)


def _norm_mod(x, g, shift, scale, *, rows, seq, batch, router=None):
    m, d = x.shape
    tm = _pick(math.gcd(rows, seq), (256, 128, 64, 32, 16))
    midx = _mod_index(tm, seq, batch)
    in_specs = [pl.BlockSpec((tm, d), lambda i: (i, 0)),
                pl.BlockSpec((1, d), lambda i: (0, 0)),
                pl.BlockSpec((1, 1, d), lambda i: (midx(i), 0, 0)),
                pl.BlockSpec((1, 1, d), lambda i: (midx(i), 0, 0))]
    vmem = 2 * (_nbytes((tm, d), F32) + _nbytes((tm, d), BF16)) + 6 * _nbytes((1, d), F32)
    if router is None:
        return pl.pallas_call(
            _norm_mod_kernel,
            grid=(rows // tm,),
            in_specs=in_specs,
            out_specs=pl.BlockSpec((tm, d), lambda i: (i, 0)),
            out_shape=jax.ShapeDtypeStruct((rows, d), BF16),
            compiler_params=_params(("parallel",), vmem),
            name="norm_modulate",
        )(x, g.reshape(1, d), shift, scale)
    rhi, rlo = router
    vmem += 2 * (_nbytes((tm, d), F32) + _nbytes((tm, LANES), F32)) + 4 * _nbytes((d, LANES), BF16)
    return pl.pallas_call(
        _norm_mod_router_kernel,
        grid=(rows // tm,),
        in_specs=in_specs + [pl.BlockSpec((d, LANES), lambda i: (0, 0)),
                             pl.BlockSpec((d, LANES), lambda i: (0, 0))],
        out_specs=[pl.BlockSpec((tm, d), lambda i: (i, 0)),
                   pl.BlockSpec((tm, LANES), lambda i: (i, 0))],
        out_shape=[jax.ShapeDtypeStruct((rows, d), F32),
                   jax.ShapeDtypeStruct((rows, LANES), F32)],
        compiler_params=_params(("parallel",), vmem),
        name="norm_modulate_router",
    )(x, g.reshape(1, d), shift, scale, rhi, rlo)


def _head_op(x, gain, rope, rot_dim, scale):
    if gain is not None:
        x = _norm_rows(x, gain)
    if rot_dim:
        cos, up, down = rope
        half = rot_dim // 2
        x = x * cos + pltpu.roll(x, LANES - half, 1) * up + pltpu.roll(x, half, 1) * down
    return x if scale == 1.0 else x * scale


def _mm_kernel(*refs, n_w, nk, epi, grouped, head_norm=False, rot_dim=0, scale=1.0):
    refs = list(refs)
    if grouped:
        used_ref = refs[1]
        refs = refs[2:]
    x_ref, w_refs = refs[0], refs[1:1 + n_w]
    pos = 1 + n_w
    if epi == "gres":
        res_ref, gate_ref = refs[pos], refs[pos + 1]
        pos += 2
    if epi == "heads":
        gain_ref = refs[pos] if head_norm else None
        pos += int(head_norm)
        rope_refs = refs[pos:pos + 3] if rot_dim else None
        pos += 3 if rot_dim else 0
    o_ref = refs[pos]
    acc_refs = refs[pos + 1:]
    k = pl.program_id(2)

    def epilogue(parts):
        if epi == "swiglu":
            g, u = parts
            out = g * jax.nn.sigmoid(g) * u
        elif epi == "gres":
            out = res_ref[...] + gate_ref[0] * parts[0]
        else:
            out = parts[0]
        o_ref[...] = out.astype(o_ref.dtype)

    def compute_heads():
        x = x_ref[...]
        gain = gain_ref[...] if head_norm else None
        rope = tuple(r[...] for r in rope_refs) if rot_dim else None
        tn = o_ref.shape[1]
        sub = HEAD_SUBBLOCK if tn % HEAD_SUBBLOCK == 0 else tn
        starts = list(range(0, tn, sub))
        dot = lambda c: jnp.dot(x, w_refs[0][:, c:c + sub], preferred_element_type=F32)
        part = dot(starts[0])
        for idx, c in enumerate(starts):
            nxt = dot(starts[idx + 1]) if idx + 1 < len(starts) else None
            for hh in range(sub // LANES):
                cols = slice(hh * LANES, (hh + 1) * LANES)
                o_ref[:, c + hh * LANES:c + (hh + 1) * LANES] = _head_op(
                    part[:, cols], gain, rope, rot_dim, scale).astype(o_ref.dtype)
            part = nxt

    def compute():
        if epi == "heads":
            compute_heads()
            return
        x = x_ref[...]
        parts = [jnp.dot(x, (w[0] if grouped else w[...]), preferred_element_type=F32) for w in w_refs]
        if nk == 1:
            epilogue(parts)
            return

        @pl.when(k == 0)
        def _():
            for a, p in zip(acc_refs, parts):
                a[...] = p

        @pl.when(k > 0)
        def _():
            for a, p in zip(acc_refs, parts):
                a[...] += p

        @pl.when(k == nk - 1)
        def _():
            epilogue([a[...] for a in acc_refs])

    if grouped:
        live = pl.program_id(0) < used_ref[0]
        pl.when(live)(compute)

        @pl.when(jnp.logical_not(live))
        def _():
            o_ref[...] = jnp.zeros_like(o_ref)
    else:
        compute()


def _matmul(x, ws, *, out_dtype, tm, tn, tk=None, epi="none", res=None, gate=None, seq=None, batch=None,
            rows=None, tile_expert=None, tiles_used=None, head=None, name="matmul"):
    m, kdim = x.shape
    rows = m if rows is None else rows
    grouped = tile_expert is not None
    n = ws[0].shape[-1]
    tk = kdim if tk is None else tk
    nk = kdim // tk
    assert rows % tm == 0 and n % tn == 0 and kdim % tk == 0
    n_w = len(ws)

    def im(f):
        return (lambda i, j, k, te, nu: f(i, j, k, te)) if grouped else (lambda i, j, k: f(i, j, k, None))

    in_specs = [pl.BlockSpec((tm, tk), im(lambda i, j, k, te: (i, k)))]
    for _ in ws:
        if grouped:
            in_specs.append(pl.BlockSpec((1, tk, tn), im(lambda i, j, k, te: (te[i], k, j))))
        else:
            in_specs.append(pl.BlockSpec((tk, tn), im(lambda i, j, k, te: (k, j))))
    args = [x, *ws]
    vmem = 2 * (_nbytes((tm, tk), x.dtype) + n_w * _nbytes((tk, tn), ws[0].dtype) + _nbytes((tm, tn), out_dtype))
    vmem += 2 * n_w * _nbytes((tm, tn), F32)
    if epi == "gres":
        midx = _mod_index(tm, seq, batch)
        in_specs += [pl.BlockSpec((tm, tn), im(lambda i, j, k, te: (i, j))),
                     pl.BlockSpec((1, 1, tn), im(lambda i, j, k, te: (midx(i), 0, j)))]
        args += [res, gate]
        vmem += 2 * _nbytes((tm, tn), F32)
    head_kw = {}
    if epi == "heads":
        gain, rope, rot_dim = head.get("gain"), head.get("rope"), head.get("rot_dim", 0)
        head_kw = dict(head_norm=gain is not None, rot_dim=rot_dim, scale=head.get("scale", 1.0))
        if gain is not None:
            in_specs.append(pl.BlockSpec((1, LANES), im(lambda i, j, k, te: (0, 0))))
            args.append(gain.reshape(1, LANES))
        if rot_dim:
            lat_tiles, seq_tiles = head["lat_rows"] // tm, seq // tm
            assert rope[0].shape[0] >= seq + tm
            tab = im(lambda i, j, k, te: (jnp.where(i < lat_tiles, i % seq_tiles, seq_tiles), 0))
            in_specs += [pl.BlockSpec((tm, LANES), tab)] * 3
            args += list(rope)
            vmem += 2 * 3 * _nbytes((tm, LANES), F32)
        vmem += 6 * _nbytes((tm, LANES), F32)
    scratch = [pltpu.VMEM((tm, tn), F32) for _ in range(n_w)] if nk > 1 else []
    vmem += len(scratch) * _nbytes((tm, tn), F32)
    grid = (rows // tm, n // tn, nk)
    out_spec = pl.BlockSpec((tm, tn), im(lambda i, j, k, te: (i, j)))
    out_shape = jax.ShapeDtypeStruct((rows, n), out_dtype)
    kern = functools.partial(_mm_kernel, n_w=n_w, nk=nk, epi=epi, grouped=grouped, **head_kw)
    cp = _params(("parallel", "parallel", "arbitrary"), vmem)
    if grouped:
        gs = pltpu.PrefetchScalarGridSpec(num_scalar_prefetch=2, grid=grid, in_specs=in_specs,
                                          out_specs=out_spec, scratch_shapes=scratch)
        return pl.pallas_call(kern, grid_spec=gs, out_shape=out_shape, compiler_params=cp, name=name)(
            tile_expert, tiles_used, *args)
    return pl.pallas_call(kern, grid=grid, in_specs=in_specs, out_specs=out_spec, out_shape=out_shape,
                          scratch_shapes=scratch, compiler_params=cp, name=name)(*args)


def _rope_tables(seq, grid_w, rot_dim, pad_rows):
    n_rows = seq // grid_w
    rows = jnp.repeat(jnp.arange(n_rows, dtype=F32), grid_w)
    cols = jnp.tile(jnp.arange(grid_w, dtype=F32), n_rows)
    n_freq = rot_dim // 4
    inv_freq = jnp.power(ROPE_THETA, -jnp.arange(n_freq, dtype=F32) / n_freq)
    ang = jnp.concatenate([rows[:, None] * inv_freq, cols[:, None] * inv_freq], axis=-1)
    cos, sin = jnp.cos(ang), jnp.sin(ang)
    zero = jnp.zeros_like(sin)
    reps = LANES // rot_dim
    cos_t = jnp.tile(jnp.concatenate([cos, cos], axis=-1), (1, reps))
    up_t = jnp.tile(jnp.concatenate([-sin, zero], axis=-1), (1, reps))
    down_t = jnp.tile(jnp.concatenate([zero, sin], axis=-1), (1, reps))
    ident = jnp.ones((pad_rows, LANES), F32)
    zpad = jnp.zeros((pad_rows, LANES), F32)
    return (jnp.concatenate([cos_t, ident]), jnp.concatenate([up_t, zpad]), jnp.concatenate([down_t, zpad]))


def _values_t_kernel(x_ref, o_ref, *, n_heads):
    for h in range(n_heads):
        cols = slice(h * LANES, (h + 1) * LANES)
        o_ref[0, cols, :] = x_ref[:, cols].astype(F32).T.astype(o_ref.dtype)


def _values_t(src, col0, n_heads, *, chunk, row0, rows, name):
    hb = math.gcd(col0, n_heads)
    width = hb * LANES
    assert row0 % chunk == 0 and rows % chunk == 0
    vmem = 2 * (_nbytes((chunk, width), src.dtype) + _nbytes((chunk, width), BF16)) + 4 * _nbytes((chunk, LANES), F32)
    return pl.pallas_call(
        functools.partial(_values_t_kernel, n_heads=hb),
        grid=(rows // chunk, n_heads // hb),
        in_specs=[pl.BlockSpec((chunk, width), lambda i, j: (row0 // chunk + i, col0 // hb + j))],
        out_specs=pl.BlockSpec((1, width, chunk), lambda i, j: (i, j, 0)),
        out_shape=jax.ShapeDtypeStruct((rows // chunk, n_heads * LANES, chunk), BF16),
        compiler_params=_params(("parallel", "parallel"), vmem),
        name=name,
    )(src)


def _kv_chunk(seq):
    return _pick(seq // 2, (512, 256, 128))


def _qk(q, k):
    return lax.dot_general(q, k, (((1,), (1,)), ((), ())), preferred_element_type=F32)


def _cat(refs, rows=None):
    parts = [r[...] if rows is None else r[rows, :] for r in refs]
    return parts[0] if len(parts) == 1 else jnp.concatenate(parts, axis=-1)


def _flash_kernel(*refs, mode, n_q, has_lat, tq, tqs, tkc, seq, window):
    refs = list(refs)
    q_refs = [refs.pop(0) for _ in range(n_q)]
    kc_refs = [refs.pop(0) for _ in range(n_q)]
    vc_ref = refs.pop(0)
    if has_lat:
        kl_refs = [refs.pop(0) for _ in range(n_q)]
        vl_ref = refs.pop(0)
    if mode == "window":
        sink_ref = refs.pop(0)
    if mode == "diff":
        lam_ref, subln_ref = refs.pop(0), refs.pop(0)
    o_ref = refs.pop(0)
    n_state = 2 if mode == "diff" else 1
    n_sub = tq // tqs
    chain_ids = [(sub, st) for sub in range(n_sub) for st in range(n_state)]
    per_chain = 7
    state_refs = {cid: tuple(refs[per_chain * i:][:3]) for i, cid in enumerate(chain_ids)}
    score_refs = {cid: tuple(refs[per_chain * i + 3:][:2]) for i, cid in enumerate(chain_ids)}
    smax_refs = {cid: tuple(refs[per_chain * i + 5:][:2]) for i, cid in enumerate(chain_ids)}
    qi = pl.program_id(2)

    def scores(k, only_sub=None):
        out = {}
        for sub, st in chain_ids:
            if only_sub is not None and sub != only_sub:
                continue
            q = _cat(q_refs, pl.ds(sub * tqs, tqs))
            if mode == "diff":
                lane = lax.broadcasted_iota(jnp.int32, q.shape, 1)
                keep = (lane < LANES // 2) if st == 0 else (lane >= LANES // 2)
                q = jnp.where(keep, q, jnp.zeros_like(q))
            out[sub, st] = _qk(k, q)
        return out

    def absorb(s_of, vt, *, first=False, mask=None):
        for cid, s in s_of.items():
            m_ref, l_ref, acc_ref = state_refs[cid]
            if isinstance(s, tuple):
                s, s_max = s
            else:
                if mask is not None:
                    s = jnp.where(mask, s, MASK_VALUE)
                s_max = jnp.max(s, axis=0, keepdims=True)
            if first and mode == "window":
                m_old = jnp.broadcast_to(sink_ref[0][:, :1], (1, tqs))
                m_new = jnp.maximum(m_old, s_max)
                l_old = jnp.exp2(m_old - m_new)
            elif first:
                m_new, l_old = s_max, None
            else:
                m_old = m_ref[...]
                m_new = jnp.maximum(m_old, s_max)
                alpha = jnp.exp2(m_old - m_new)
                l_old = alpha * l_ref[...]
            p = jnp.exp2(s - m_new)
            l_new = jnp.sum(p, axis=0, keepdims=True)
            pv = jnp.dot(vt, p.astype(vt.dtype), preferred_element_type=F32)
            m_ref[...] = m_new
            l_ref[...] = l_new if l_old is None else l_old + l_new
            acc_ref[...] = pv if first else alpha * acc_ref[...] + pv

    def lat_keys(c):
        return _cat(kl_refs, pl.ds(pl.multiple_of(c * tkc, tkc), tkc))

    def stash(slot, s_of):
        for cid, s in s_of.items():
            score_refs[cid][slot][...] = s
            smax_refs[cid][slot][...] = jnp.max(s, axis=0, keepdims=True)

    def fetch(slot):
        return {cid: (score_refs[cid][slot][...], smax_refs[cid][slot][...]) for cid in chain_ids}

    s_ctx = scores(_cat(kc_refs))
    if has_lat and mode == "window":
        tkw = tqs + 2 * window
        blocks = []
        for sub in range(n_sub):
            q_pos0 = qi * tq + sub * tqs
            start = pl.multiple_of(jnp.clip(q_pos0 - window, 0, seq - tkw), LANES)
            kp = start + lax.broadcasted_iota(jnp.int32, (tkw, tqs), 0)
            qp = q_pos0 + lax.broadcasted_iota(jnp.int32, (tkw, tqs), 1)
            blocks.append((scores(_cat(kl_refs, pl.ds(start, tkw)), only_sub=sub), start // LANES,
                           jnp.abs(qp - kp) <= window))
        absorb(s_ctx, vc_ref[0], first=True)
        for s_of, first_chunk, mask in blocks:
            vt = jnp.concatenate([vl_ref[first_chunk + j] for j in range(tkw // LANES)], axis=-1)
            absorb(s_of, vt, mask=mask)
    elif has_lat:
        n_chunks = seq // tkc
        stash(0, scores(lat_keys(0)))
        absorb(s_ctx, vc_ref[0], first=True)

        def body(i, carry):
            c = 2 * i
            stash(1, scores(lat_keys(c + 1)))
            absorb(fetch(0), vl_ref[c])
            stash(0, scores(lat_keys(c + 2)))
            absorb(fetch(1), vl_ref[c + 1])
            return carry

        lax.fori_loop(0, n_chunks // 2 - 1, body, 0)
        stash(1, scores(lat_keys(n_chunks - 1)))
        absorb(fetch(0), vl_ref[n_chunks - 2])
        absorb(fetch(1), vl_ref[n_chunks - 1])
    else:
        absorb(s_ctx, vc_ref[0], first=True)

    for sub in range(n_sub):
        rows = pl.ds(sub * tqs, tqs)
        outs = [(state_refs[sub, st][2][...] / state_refs[sub, st][1][...]).T for st in range(n_state)]
        if mode == "diff":
            lam_rows = lam_ref[...]
            lam = (jnp.exp(jnp.sum(lam_rows[0:1] * lam_rows[1:2], axis=-1, keepdims=True))
                   - jnp.exp(jnp.sum(lam_rows[2:3] * lam_rows[3:4], axis=-1, keepdims=True)) + D_LAMBDA_INIT)
            out = _norm_rows(outs[0] - lam * outs[1], subln_ref[...]) * (1.0 - D_LAMBDA_INIT)
        else:
            out = outs[0]
        o_ref[rows, :] = out.astype(o_ref.dtype)


def _flash(q_parts, k_parts, v, *, n_heads, group, batch, seq, ctx_len, lat_rows, ctx_queries=False,
           mode="softmax", sink=None, lam=None, subln=None, window=0, name="flash"):
    has_lat = not ctx_queries
    ctx_blk0 = lat_rows // ctx_len
    if ctx_queries:
        tq, nq = ctx_len, 1
        q_row = lambda b, h, i: ctx_blk0 + b
        out_rows = batch * ctx_len
        o_row = lambda b, h, i: b
    else:
        tq = _pick(seq, (512, 256, 128) if mode == "diff" else (1024, 512, 256, 128))
        nq = seq // tq
        q_row = lambda b, h, i: b * nq + i
        out_rows = lat_rows
        o_row = q_row
    tqs = min(tq, 256)
    tkc = _kv_chunk(seq)
    n_state = 2 if mode == "diff" else 1
    if mode == "window":
        assert seq >= tqs + 2 * window and window % 16 == 0
    n_q = len(q_parts)
    in_specs, args = [], []
    vmem = 0

    def add(arr, block, imap):
        nonlocal vmem
        in_specs.append(pl.BlockSpec(block, imap))
        args.append(arr)
        vmem += 2 * _nbytes(block, arr.dtype)

    for arr, c0 in q_parts:
        add(arr, (tq, LANES), lambda b, h, i, c0=c0: (q_row(b, h, i), c0 + h))
    for arr, c0, per_head in k_parts:
        add(arr, (ctx_len, LANES), lambda b, h, i, c0=c0, ph=per_head: (ctx_blk0 + b, c0 + (h // group) * ph))
    vt_lat, vt_ctx = v
    add(vt_ctx, (1, LANES, ctx_len), lambda b, h, i: (b, h // group, 0))
    if has_lat:
        for arr, c0, per_head in k_parts:
            add(arr, (seq, LANES), lambda b, h, i, c0=c0, ph=per_head: (b, c0 + (h // group) * ph))
        tkv = vt_lat.shape[-1]
        assert tkv == (LANES if mode == "window" else tkc)
        add(vt_lat, (seq // tkv, LANES, tkv), lambda b, h, i: (b, h // group, 0))
    if mode == "window":
        add(sink, (1, 1, LANES), lambda b, h, i: (h, 0, 0))
    if mode == "diff":
        add(lam, (4, LANES), lambda b, h, i: (0, 0))
        add(subln, (1, LANES), lambda b, h, i: (0, 0))
    n_sub = tq // tqs
    scratch = [pltpu.VMEM((1, tqs), F32), pltpu.VMEM((1, tqs), F32), pltpu.VMEM((LANES, tqs), F32),
               pltpu.VMEM((tkc, tqs), F32), pltpu.VMEM((tkc, tqs), F32),
               pltpu.VMEM((1, tqs), F32), pltpu.VMEM((1, tqs), F32)] * (n_state * n_sub)
    vmem += 2 * n_state * n_sub * _nbytes((tkc, tqs), F32)
    vmem += 2 * _nbytes((tq, LANES), BF16) + 3 * _nbytes((n_state, tq, LANES), F32)
    vmem += 4 * n_state * (tq // tqs) * _nbytes((tqs, max(tkc, ctx_len, tqs + 2 * window)), F32)
    kern = functools.partial(_flash_kernel, mode=mode, n_q=n_q, has_lat=has_lat, tq=tq, tqs=tqs, tkc=tkc,
                             seq=seq, window=window)
    return pl.pallas_call(
        kern,
        grid=(batch, n_heads, nq),
        in_specs=in_specs,
        out_specs=pl.BlockSpec((tq, LANES), lambda b, h, i: (o_row(b, h, i), h)),
        out_shape=jax.ShapeDtypeStruct((out_rows, n_heads * LANES), BF16),
        scratch_shapes=scratch,
        compiler_params=_params(("parallel", "parallel", "parallel"), vmem),
        name=name,
    )(*args)


def _route_kernel(lg_ref, info_ref, cnt_ref, base_ref, *, n_experts, tb):
    step = pl.program_id(0)

    @pl.when(step == 0)
    def _():
        base_ref[...] = jnp.zeros_like(base_ref)

    lane = lax.broadcasted_iota(jnp.int32, (tb, LANES), 1).astype(F32)
    logits = jnp.where(lane < n_experts, lg_ref[...], -jnp.inf)
    v1 = jnp.max(logits, axis=-1, keepdims=True)
    i1 = jnp.min(jnp.where(logits == v1, lane, float(LANES)), axis=-1, keepdims=True)
    hot1 = lane == i1
    rest = jnp.where(hot1, -jnp.inf, logits)
    v2 = jnp.max(rest, axis=-1, keepdims=True)
    i2 = jnp.min(jnp.where(rest == v2, lane, float(LANES)), axis=-1, keepdims=True)
    hot2 = lane == i2
    e = jnp.exp(v2 - v1)
    w1 = 1.0 / (1.0 + e)
    w2 = e / (1.0 + e)
    sel = jnp.where(hot1 | hot2, 1.0, 0.0)
    r = lax.broadcasted_iota(jnp.int32, (tb, tb), 0)
    c = lax.broadcasted_iota(jnp.int32, (tb, tb), 1)
    tri = jnp.where(c < r, 1.0, 0.0).astype(BF16)
    before = jnp.dot(tri, sel.astype(BF16), preferred_element_type=F32) + base_ref[...]
    rank1 = jnp.sum(jnp.where(hot1, before, 0.0), axis=-1, keepdims=True)
    rank2 = jnp.sum(jnp.where(hot2, before, 0.0), axis=-1, keepdims=True)
    base_ref[...] += jnp.sum(sel, axis=0, keepdims=True)
    cnt_ref[...] = jnp.broadcast_to(base_ref[...], cnt_ref.shape)
    info = jnp.zeros((tb, LANES), F32)
    for slot, val in enumerate((i1, i2, rank1, rank2, w1, w2)):
        info = jnp.where(lane == float(slot), val, info)
    info_ref[...] = info


def _route(logits, n_experts):
    n = logits.shape[0]
    tb = _pick(n, (256, 128, 64, 32, 16, 8))
    vmem = 4 * _nbytes((tb, LANES), F32) + 16 * _nbytes((tb, max(tb, LANES)), F32)
    return pl.pallas_call(
        functools.partial(_route_kernel, n_experts=n_experts, tb=tb),
        grid=(n // tb,),
        in_specs=[pl.BlockSpec((tb, LANES), lambda i: (i, 0))],
        out_specs=[pl.BlockSpec((tb, LANES), lambda i: (i, 0)),
                   pl.BlockSpec((8, LANES), lambda i: (0, 0))],
        out_shape=[jax.ShapeDtypeStruct((n, LANES), F32), jax.ShapeDtypeStruct((8, LANES), F32)],
        scratch_shapes=[pltpu.VMEM((1, LANES), F32)],
        compiler_params=_params(("arbitrary",), vmem),
        name="moe_route",
    )(logits)


def _row_copy(src_hbm, row, dst_vmem, slot, sem):
    return pltpu.make_async_copy(src_hbm.at[pl.ds(row, 1)], dst_vmem.at[pl.ds(slot, 1)], sem)


def _gather_kernel(src_ref, h_hbm, o_ref, buf, sem, *, tm):
    base = pl.program_id(0) * tm

    def issue(r, carry):
        _row_copy(h_hbm, src_ref[base + r], buf, r, sem).start()
        return carry

    lax.fori_loop(0, tm, issue, 0, unroll=DMA_LOOP_UNROLL)

    def drain(r, carry):
        _row_copy(h_hbm, 0, buf, r, sem).wait()
        return carry

    lax.fori_loop(0, tm, drain, 0, unroll=DMA_LOOP_UNROLL)
    o_ref[...] = buf[...].astype(o_ref.dtype)


def _gather_rows(src_rows, h, *, tm):
    p = src_rows.shape[0]
    d = h.shape[1]
    vmem = _nbytes((tm, d), F32) + 2 * _nbytes((tm, d), BF16)
    gs = pltpu.PrefetchScalarGridSpec(
        num_scalar_prefetch=1, grid=(p // tm,),
        in_specs=[pl.BlockSpec(memory_space=pl.ANY)],
        out_specs=pl.BlockSpec((tm, d), lambda i, src: (i, 0)),
        scratch_shapes=[pltpu.VMEM((tm, d), F32), pltpu.SemaphoreType.DMA(())])
    return pl.pallas_call(
        functools.partial(_gather_kernel, tm=tm), grid_spec=gs,
        out_shape=jax.ShapeDtypeStruct((p, d), BF16),
        compiler_params=_params(("arbitrary",), vmem),
        name="moe_gather",
    )(src_rows, h)


def _combine_kernel(d1_ref, d2_ref, y_hbm, x_ref, info_ref, gate_ref, g_ref, o_ref, buf1, buf2, sem, *, tb):
    base = pl.program_id(0) * tb

    def issue(r, carry):
        _row_copy(y_hbm, d1_ref[base + r], buf1, r, sem.at[0]).start()
        _row_copy(y_hbm, d2_ref[base + r], buf2, r, sem.at[1]).start()
        return carry

    lax.fori_loop(0, tb, issue, 0, unroll=DMA_LOOP_UNROLL)

    def drain(r, carry):
        _row_copy(y_hbm, 0, buf1, r, sem.at[0]).wait()
        _row_copy(y_hbm, 0, buf2, r, sem.at[1]).wait()
        return carry

    lax.fori_loop(0, tb, drain, 0, unroll=DMA_LOOP_UNROLL)
    info = info_ref[...]
    w1, w2 = info[:, 4:5], info[:, 5:6]
    x = x_ref[...] + gate_ref[0] * (w1 * buf1[...] + w2 * buf2[...])
    o_ref[...] = _norm_rows(x, g_ref[...])


def _combine(dest1, dest2, y, x, info, gate, final_norm, *, seq, batch):
    n, d = x.shape
    tb = _pick(seq, (256, 128, 64, 32, 16, 8))
    midx = _mod_index(tb, seq, batch)
    vmem = 2 * _nbytes((tb, d), F32) * 3 + 2 * _nbytes((tb, d), F32) + 2 * _nbytes((tb, LANES), F32)
    gs = pltpu.PrefetchScalarGridSpec(
        num_scalar_prefetch=2, grid=(n // tb,),
        in_specs=[pl.BlockSpec(memory_space=pl.ANY),
                  pl.BlockSpec((tb, d), lambda i, a, b: (i, 0)),
                  pl.BlockSpec((tb, LANES), lambda i, a, b: (i, 0)),
                  pl.BlockSpec((1, 1, d), lambda i, a, b: (midx(i), 0, 0)),
                  pl.BlockSpec((1, d), lambda i, a, b: (0, 0))],
        out_specs=pl.BlockSpec((tb, d), lambda i, a, b: (i, 0)),
        scratch_shapes=[pltpu.VMEM((tb, d), F32), pltpu.VMEM((tb, d), F32), pltpu.SemaphoreType.DMA((2,))])
    return pl.pallas_call(
        functools.partial(_combine_kernel, tb=tb), grid_spec=gs,
        out_shape=jax.ShapeDtypeStruct((n, d), F32),
        compiler_params=_params(("arbitrary",), vmem),
        name="moe_combine_norm",
    )(dest1, dest2, y, x, info, gate, final_norm.reshape(1, d))


def _moe(h_f32, logits, x, gate, final_norm, w_gate, w_up, w_down, *, dims):
    n, d = x.shape
    n_exp = dims.n_experts
    tm = _pick(n, (512, 256, 128, 64, 32, 16))
    p = 2 * n + n_exp * tm
    info, counts = _route(logits, n_exp)
    cnt = counts[0, :n_exp].astype(jnp.int32)
    padded = ((cnt + tm - 1) // tm) * tm
    ends = jnp.cumsum(padded)
    starts = ends - padded
    e1, e2 = info[:, 0].astype(jnp.int32), info[:, 1].astype(jnp.int32)
    dest1 = starts[e1] + info[:, 2].astype(jnp.int32)
    dest2 = starts[e2] + info[:, 3].astype(jnp.int32)
    token = jnp.arange(n, dtype=jnp.int32)
    src = jnp.zeros((p,), jnp.int32).at[dest1].set(token).at[dest2].set(token)
    tile_start = jnp.arange(p // tm, dtype=jnp.int32) * tm
    tile_expert = jnp.minimum(jnp.sum(tile_start[:, None] >= ends[None, :], axis=1), n_exp - 1).astype(jnp.int32)
    tiles_used = (ends[-1:] // tm).astype(jnp.int32)

    xs = _gather_rows(src, h_f32, tm=_pick(tm, (256, 128, 64, 32, 16)))
    f = w_gate.shape[-1]
    act = _matmul(xs, [w_gate, w_up], out_dtype=BF16, tm=tm, tn=_pick(f, (512, 256, 128)), epi="swiglu",
                  tile_expert=tile_expert, tiles_used=tiles_used, name="moe_gate_up")
    y = _matmul(act, [w_down], out_dtype=F32, tm=tm, tn=_pick(d, (1024, 512, 256, 128)),
                tile_expert=tile_expert, tiles_used=tiles_used, name="moe_down")
    return _combine(dest1, dest2, y, x, info, gate, final_norm, seq=dims.seq, batch=dims.batch)


def _pad_cols(w, n):
    return w if w.shape[-1] == n else jnp.pad(w, [(0, 0)] * (w.ndim - 1) + [(0, n - w.shape[-1])])


def _round_up(n, m):
    return (n + m - 1) // m * m


def _forward(dims, x, c, ctx, c_ctx,
             ada_w0, ada_b0, norm_mix0, norm_ffn0, w_in0, a_q_norm, a_k_norm, b_q_norm, b_kv_norm,
             w_uq, w_ukv, w_o0, ffn_w_gate, ffn_w_up, ffn_w_down,
             ada_w1, ada_b1, norm_mix1, norm_ffn1, w_in1, c_sink, d_lam_q1, d_lam_k1, d_lam_q2, d_lam_k2,
             d_subln, w_o1, router_w, moe_w_gate, moe_w_up, moe_w_down, final_norm):
    dm, bsz, seq, ctx_len = dims.d_model, dims.batch, dims.seq, dims.ctx_len
    nl, nc = bsz * seq, bsz * ctx_len
    nt = nl + nc
    ha, hka, hb = dims.a_heads, dims.a_kv_heads, dims.b_heads
    hc, hkc, hd = dims.c_heads, dims.c_kv_heads, dims.d_heads
    big = lambda n: _pick(n, (1024, 512, 256, 128, 64, 32, 16))
    tm_all, tm_lat = big(math.gcd(nt, seq)), big(math.gcd(nl, seq))

    xa = jnp.concatenate([x.reshape(nl, dm), ctx.reshape(nc, dm)], axis=0)
    cond = jnp.concatenate([c, c_ctx[None, :], jnp.zeros((8 - bsz - 1, dm), F32)], axis=0)

    def mods(ada_w, ada_b):
        m = _modulation(cond, ada_w, ada_b).reshape(8, N_MOD, 1, dm)
        return [m[:, i] for i in range(N_MOD)]

    rope128 = _rope_tables(seq, dims.grid_w, 128, tm_all)
    rope64 = _rope_tables(seq, dims.grid_w, 64, tm_all)
    flash = functools.partial(_flash, batch=bsz, seq=seq, ctx_len=ctx_len, lat_rows=nl)

    def project(src, w, *, rows=nt, out_dtype=BF16, name, **head):
        n = w.shape[1]
        tm = tm_all if rows == nt else tm_lat
        if head:
            return _matmul(src, [w.astype(BF16)], out_dtype=out_dtype, tm=tm, tn=_pick(n, (1024, 512, 256, 128)),
                           rows=rows, epi="heads", head=dict(head, lat_rows=nl), seq=seq, name=name)
        return _matmul(src, [w.astype(BF16)], out_dtype=out_dtype, tm=tm, tn=_pick(n, (512, 256, 128)),
                       rows=rows, name=name)

    def values_t(src, col0, n_heads, name, chunk=_kv_chunk(seq)):
        return (_values_t(src, col0, n_heads, chunk=chunk, row0=0, rows=nl, name=name),
                _values_t(src, col0, n_heads, chunk=ctx_len, row0=nl, rows=nc, name=name + "_ctx"))

    sh1, sc1, g1, sh2, sc2, g2 = mods(ada_w0, ada_b0)
    h = _norm_mod(xa, norm_mix0, sh1, sc1, rows=nt, seq=seq, batch=bsz)
    na_q, na_kv, rq, rkv = ha * HEAD_DIM, hka * HEAD_DIM, dims.b_q_rank, dims.b_kv_rank
    c_ak, c_av, c_cq, c_ckv, c_kr = na_q, na_q + na_kv, na_q + 2 * na_kv, na_q + 2 * na_kv + rq, na_q + 2 * na_kv + rq + rkv
    aq = project(h, w_in0[:, :c_ak], gain=a_q_norm, rope=rope128, rot_dim=128, scale=LOG2E * HEAD_DIM ** -0.5,
                 name="proj_aq")
    ak = project(h, w_in0[:, c_ak:c_av], gain=a_k_norm, rope=rope128, rot_dim=128, name="proj_ak")
    bk_rope = project(h, _pad_cols(w_in0[:, c_kr:], LANES), rope=rope64, rot_dim=64, name="proj_bk_rope")
    rest = project(h, jnp.concatenate([w_in0[:, c_cq:c_kr], w_in0[:, c_av:c_cq]], axis=1), out_dtype=F32,
                   name="proj_ranks_av")
    av = values_t(rest, (rq + rkv) // HEAD_DIM, hka, "vt_a")
    cq = _rank_norm(rest, 0, rq, b_q_norm)
    ckv = _rank_norm(rest, rq, rkv, b_kv_norm)
    b_scale = LOG2E * (HEAD_DIM + 64) ** -0.5
    uq = w_uq.reshape(rq, hb, HEAD_DIM + 64) * b_scale
    bq_nope = project(cq, uq[:, :, :HEAD_DIM].reshape(rq, hb * HEAD_DIM), name="b_up_q_nope")
    bq_rope = project(cq, _pad_cols(uq[:, :, HEAD_DIM:], LANES).reshape(rq, hb * LANES), rope=rope64, rot_dim=64,
                      name="b_up_q_rope")
    ukv = w_ukv.reshape(rkv, hb, 2 * HEAD_DIM)
    ukv = jnp.concatenate([ukv[:, :, :HEAD_DIM].reshape(rkv, hb * HEAD_DIM),
                           ukv[:, :, HEAD_DIM:].reshape(rkv, hb * HEAD_DIM)], axis=1)
    bkv = project(ckv, ukv, name="b_up_kv")

    a_lat = flash([(aq, 0)], [(ak, 0, 1)], av, n_heads=ha, group=ha // hka, name="attn_a")
    a_ctx = flash([(aq, 0)], [(ak, 0, 1)], av, n_heads=ha, group=ha // hka, ctx_queries=True, name="attn_a_ctx")
    b_args = ([(bq_nope, 0), (bq_rope, 0)], [(bkv, 0, 1), (bk_rope, 0, 0)], values_t(bkv, hb, hb, "vt_b"))
    b_lat = flash(*b_args, n_heads=hb, group=1, name="attn_b")
    b_ctx = flash(*b_args, n_heads=hb, group=1, ctx_queries=True, name="attn_b_ctx")
    mixed = jnp.concatenate([jnp.concatenate([a_lat, b_lat], axis=1),
                             jnp.concatenate([a_ctx, b_ctx], axis=1)], axis=0)
    xa = _matmul(mixed, [w_o0.astype(BF16)], out_dtype=F32, tm=tm_all, tn=_pick(dm, (512, 256, 128)),
                 epi="gres", res=xa, gate=g1, seq=seq, batch=bsz, name="out_proj0")

    h = _norm_mod(xa, norm_ffn0, sh2, sc2, rows=nt, seq=seq, batch=bsz)
    f_pad = _round_up(dims.ffn_dim, 1024) if dims.ffn_dim > 1024 else _round_up(dims.ffn_dim, LANES)
    tf = _pick(f_pad, (512, 256, 128))
    act = _matmul(h, [_pad_cols(ffn_w_gate, f_pad).astype(BF16), _pad_cols(ffn_w_up, f_pad).astype(BF16)],
                  out_dtype=BF16, tm=tm_all, tn=tf, epi="swiglu", name="ffn_gate_up")
    w_down = jnp.pad(ffn_w_down, ((0, f_pad - dims.ffn_dim), (0, 0))).astype(BF16)
    tk_down = f_pad // 2 if (f_pad // 2) % LANES == 0 and f_pad > 2048 else f_pad
    xa = _matmul(act, [w_down], out_dtype=F32, tm=tm_all, tn=_pick(dm, (512, 256, 128)), tk=tk_down,
                 epi="gres", res=xa, gate=g2, seq=seq, batch=bsz, name="ffn_down")

    sh1, sc1, g1, sh2, sc2, g2 = mods(ada_w1, ada_b1)
    h = _norm_mod(xa, norm_mix1, sh1, sc1, rows=nt, seq=seq, batch=bsz)
    bounds = [0]
    for n_heads in (hc, hkc, hkc, hd, hd, hd):
        bounds.append(bounds[-1] + n_heads * HEAD_DIM)
    w_cq, w_ck, w_cv, w_dq, w_dk, w_dv = (w_in1[:, lo:hi] for lo, hi in zip(bounds[:-1], bounds[1:]))
    cq_ = project(h, w_cq, rows=nl, rope=rope128, rot_dim=128, scale=LOG2E * HEAD_DIM ** -0.5, name="proj_cq")
    ck_ = project(h, w_ck, rope=rope128, rot_dim=128, name="proj_ck")
    dq_ = project(h, w_dq, rows=nl, rope=rope64, rot_dim=64, scale=LOG2E * 64 ** -0.5, name="proj_dq")
    dk_ = project(h, w_dk, rope=rope64, rot_dim=64, name="proj_dk")
    values = project(h, jnp.concatenate([w_cv, w_dv], axis=1), name="proj_cv_dv")
    cv_ = values_t(values, 0, hkc, "vt_c", chunk=LANES)
    dv_ = values_t(values, hkc, hd, "vt_d")
    sink = jnp.broadcast_to(LOG2E * c_sink.astype(F32)[:, None, None], (hc, 1, LANES))
    lam = jnp.stack([_pad_cols(v.astype(F32)[None, :], LANES)[0] for v in (d_lam_q1, d_lam_k1, d_lam_q2, d_lam_k2)])
    c_out = flash([(cq_, 0)], [(ck_, 0, 1)], cv_, n_heads=hc, group=hc // hkc, mode="window", sink=sink,
                  window=dims.window, name="attn_c")
    d_out = flash([(dq_, 0)], [(dk_, 0, 1)], dv_, n_heads=hd, group=1, mode="diff", lam=lam,
                  subln=d_subln.reshape(1, LANES), name="attn_d")
    mixed = jnp.concatenate([c_out, d_out], axis=1)
    xl = _matmul(mixed, [w_o1.astype(BF16)], out_dtype=F32, tm=tm_lat, tn=_pick(dm, (512, 256, 128)),
                 epi="gres", res=xa, gate=g1, seq=seq, batch=bsz, rows=nl, name="out_proj1")

    rw = _pad_cols(router_w, LANES)
    rw_hi = rw.astype(BF16)
    rw_lo = (rw - rw_hi.astype(F32)).astype(BF16)
    hf32, logits = _norm_mod(xl, norm_ffn1, sh2, sc2, rows=nl, seq=seq, batch=bsz, router=(rw_hi, rw_lo))
    out = _moe(hf32, logits, xl, g2, final_norm, moe_w_gate.astype(BF16), moe_w_up.astype(BF16),
               moe_w_down.astype(BF16), dims=dims)
    return out.reshape(bsz, seq, dm)


def _rank_norm_kernel(x_ref, g_ref, o_ref):
    o_ref[...] = _norm_rows(x_ref[...], g_ref[...]).astype(o_ref.dtype)


def _rank_norm(src, col0, width, gain):
    rows = src.shape[0]
    tm = _pick(rows, (256, 128, 64, 32, 16))
    assert col0 % width == 0
    vmem = 2 * (_nbytes((tm, width), F32) + _nbytes((tm, width), BF16))
    return pl.pallas_call(
        _rank_norm_kernel,
        grid=(rows // tm,),
        in_specs=[pl.BlockSpec((tm, width), lambda i: (i, col0 // width)),
                  pl.BlockSpec((1, width), lambda i: (0, 0))],
        out_specs=pl.BlockSpec((tm, width), lambda i: (i, 0)),
        out_shape=jax.ShapeDtypeStruct((rows, width), BF16),
        compiler_params=_params(("parallel",), vmem),
        name="rank_norm",
    )(src, gain.reshape(1, width))


_DIMS = Dims(d_model=4096, batch=4, seq=4096, ctx_len=256, grid_w=64, a_heads=16, a_kv_heads=4, b_heads=16,
             b_q_rank=1536, b_kv_rank=512, c_heads=16, c_kv_heads=4, window=128, d_heads=16, ffn_dim=11008,
             n_experts=8, expert_dim=3584)


def kernel(x, c, ctx, c_ctx, ada_w0, ada_b0, norm_mix0, norm_ffn0, w_in0, a_q_norm, a_k_norm, b_q_norm, b_kv_norm, w_uq, w_ukv, w_o0, ffn_w_gate, ffn_w_up, ffn_w_down, ada_w1, ada_b1, norm_mix1, norm_ffn1, w_in1, c_sink, d_lam_q1, d_lam_k1, d_lam_q2, d_lam_k2, d_subln, w_o1, router_w, moe_w_gate, moe_w_up, moe_w_down, final_norm):
    return _forward(_DIMS, x, c, ctx, c_ctx, ada_w0, ada_b0, norm_mix0, norm_ffn0, w_in0, a_q_norm, a_k_norm,
                    b_q_norm, b_kv_norm, w_uq, w_ukv, w_o0, ffn_w_gate, ffn_w_up, ffn_w_down, ada_w1, ada_b1,
                    norm_mix1, norm_ffn1, w_in1, c_sink, d_lam_q1, d_lam_k1, d_lam_q2, d_lam_k2, d_subln, w_o1,
                    router_w, moe_w_gate, moe_w_up, moe_w_down, final_norm)
```

```python
import functools
import math
from typing import NamedTuple

import jax
import jax.numpy as jnp
from jax import lax
from jax.experimental import pallas as pl
from jax.experimental.pallas import tpu as pltpu

F32 = jnp.float32
BF16 = jnp.bfloat16

LANES = 128
HEAD_DIM = 128
ROPE_THETA = 10000.0
NORM_EPS = 1e-6
MASK_VALUE = -1e30
LOG2E = math.log2(math.e)
N_MOD = 6
D_LAYER_INDEX = 1
D_LAMBDA_INIT = 0.8 - 0.6 * math.exp(-0.3 * D_LAYER_INDEX)
VMEM_HEADROOM = 6 << 20
DMA_LOOP_UNROLL = 8
HEAD_SUBBLOCK = 256


class Dims(NamedTuple):
    d_model: int
    batch: int
    seq: int
    ctx_len: int
    grid_w: int
    a_heads: int
    a_kv_heads: int
    b_heads: int
    b_q_rank: int
    b_kv_rank: int
    c_heads: int
    c_kv_heads: int
    window: int
    d_heads: int
    ffn_dim: int
    n_experts: int
    expert_dim: int


def _pick(n, prefs):
    for p in prefs:
        if n % p == 0:
            return p
    raise ValueError(f"no tile in {prefs} divides {n}")


def _params(sem, vmem_bytes):
    return pltpu.CompilerParams(dimension_semantics=sem, vmem_limit_bytes=int(vmem_bytes) + VMEM_HEADROOM)


def _nbytes(shape, dtype):
    return math.prod(shape) * jnp.dtype(dtype).itemsize


def _mod_kernel(c_ref, w_ref, b_ref, o_ref):
    c = c_ref[...]
    s = (c * jax.nn.sigmoid(c)).astype(BF16)
    o_ref[...] = jnp.dot(s, w_ref[...].astype(BF16), preferred_element_type=F32) + b_ref[...]


def _modulation(cond, ada_w, ada_b):
    rows, d = cond.shape
    n = ada_w.shape[1]
    tn = _pick(n, (512, 256, 128))
    vmem = 2 * (_nbytes((d, tn), F32) + _nbytes((rows, tn), F32) * 2) + _nbytes((rows, d), F32) * 2
    return pl.pallas_call(
        _mod_kernel,
        grid=(n // tn,),
        in_specs=[pl.BlockSpec((rows, d), lambda j: (0, 0)),
                  pl.BlockSpec((d, tn), lambda j: (0, j)),
                  pl.BlockSpec((1, tn), lambda j: (0, j))],
        out_specs=pl.BlockSpec((rows, tn), lambda j: (0, j)),
        out_shape=jax.ShapeDtypeStruct((rows, n), F32),
        compiler_params=_params(("parallel",), vmem),
        name="modulation",
    )(cond, ada_w, ada_b.reshape(1, n))


def _norm_rows(x, g):
    var = jnp.mean(x * x, axis=-1, keepdims=True)
    return x * lax.rsqrt(var + NORM_EPS) * g


def _norm_mod_kernel(x_ref, g_ref, sh_ref, sc_ref, o_ref):
    y = _norm_rows(x_ref[...], g_ref[...])
    o_ref[...] = (y * (1.0 + sc_ref[0]) + sh_ref[0]).astype(o_ref.dtype)


def _norm_mod_router_kernel(x_ref, g_ref, sh_ref, sc_ref, rhi_ref, rlo_ref, of_ref, lg_ref):
    y = _norm_rows(x_ref[...], g_ref[...])
    h = y * (1.0 + sc_ref[0]) + sh_ref[0]
    of_ref[...] = h
    hi = h.astype(BF16)
    lo = (h - hi.astype(F32)).astype(BF16)
    lg_ref[...] = (jnp.dot(hi, rhi_ref[...], preferred_element_type=F32)
                   + jnp.dot(hi, rlo_ref[...], preferred_element_type=F32)
                   + jnp.dot(lo, rhi_ref[...], preferred_element_type=F32))


def _mod_index(tm, seq, batch):
    return lambda i: jnp.minimum((i * tm) // seq, batch)


def _norm_mod(x, g, shift, scale, *, rows, seq, batch, router=None):
    m, d = x.shape
    tm = _pick(math.gcd(rows, seq), (256, 128, 64, 32, 16))
    midx = _mod_index(tm, seq, batch)
    in_specs = [pl.BlockSpec((tm, d), lambda i: (i, 0)),
                pl.BlockSpec((1, d), lambda i: (0, 0)),
                pl.BlockSpec((1, 1, d), lambda i: (midx(i), 0, 0)),
                pl.BlockSpec((1, 1, d), lambda i: (midx(i), 0, 0))]
    vmem = 2 * (_nbytes((tm, d), F32) + _nbytes((tm, d), BF16)) + 6 * _nbytes((1, d), F32)
    if router is None:
        return pl.pallas_call(
            _norm_mod_kernel,
            grid=(rows // tm,),
            in_specs=in_specs,
            out_specs=pl.BlockSpec((tm, d), lambda i: (i, 0)),
            out_shape=jax.ShapeDtypeStruct((rows, d), BF16),
            compiler_params=_params(("parallel",), vmem),
            name="norm_modulate",
        )(x, g.reshape(1, d), shift, scale)
    rhi, rlo = router
    vmem += 2 * (_nbytes((tm, d), F32) + _nbytes((tm, LANES), F32)) + 4 * _nbytes((d, LANES), BF16)
    return pl.pallas_call(
        _norm_mod_router_kernel,
        grid=(rows // tm,),
        in_specs=in_specs + [pl.BlockSpec((d, LANES), lambda i: (0, 0)),
                             pl.BlockSpec((d, LANES), lambda i: (0, 0))],
        out_specs=[pl.BlockSpec((tm, d), lambda i: (i, 0)),
                   pl.BlockSpec((tm, LANES), lambda i: (i, 0))],
        out_shape=[jax.ShapeDtypeStruct((rows, d), F32),
                   jax.ShapeDtypeStruct((rows, LANES), F32)],
        compiler_params=_params(("parallel",), vmem),
        name="norm_modulate_router",
    )(x, g.reshape(1, d), shift, scale, rhi, rlo)


def _head_op(x, gain, rope, rot_dim, scale):
    if gain is not None:
        x = _norm_rows(x, gain)
    if rot_dim:
        cos, up, down = rope
        half = rot_dim // 2
        x = x * cos + pltpu.roll(x, LANES - half, 1) * up + pltpu.roll(x, half, 1) * down
    return x if scale == 1.0 else x * scale


def _mm_kernel(*refs, n_w, nk, epi, head_norm=False, rot_dim=0, scale=1.0):
    refs = list(refs)
    x_ref, w_refs = refs[0], refs[1:1 + n_w]

    def weight(w_ref, cols=slice(None)):
        w = w_ref[:, cols]
        return w if w.dtype == x_ref.dtype else w.astype(x_ref.dtype)

    pos = 1 + n_w
    if epi == "gres":
        res_ref, gate_ref = refs[pos], refs[pos + 1]
        pos += 2
    if epi == "heads":
        gain_ref = refs[pos] if head_norm else None
        pos += int(head_norm)
        rope_refs = refs[pos:pos + 3] if rot_dim else None
        pos += 3 if rot_dim else 0
    o_ref = refs[pos]
    acc_refs = refs[pos + 1:]
    k = pl.program_id(2)

    def epilogue(parts):
        if epi == "swiglu":
            g, u = parts
            out = g * jax.nn.sigmoid(g) * u
        elif epi == "gres":
            out = res_ref[...] + gate_ref[0] * parts[0]
        else:
            out = parts[0]
        o_ref[...] = out.astype(o_ref.dtype)

    def compute_heads():
        x = x_ref[...]
        gain = gain_ref[...] if head_norm else None
        rope = tuple(r[...] for r in rope_refs) if rot_dim else None
        tn = o_ref.shape[1]
        sub = HEAD_SUBBLOCK if tn % HEAD_SUBBLOCK == 0 else tn
        starts = list(range(0, tn, sub))
        dot = lambda c: jnp.dot(x, weight(w_refs[0], slice(c, c + sub)), preferred_element_type=F32)
        part = dot(starts[0])
        for idx, c in enumerate(starts):
            nxt = dot(starts[idx + 1]) if idx + 1 < len(starts) else None
            for hh in range(sub // LANES):
                cols = slice(hh * LANES, (hh + 1) * LANES)
                o_ref[:, c + hh * LANES:c + (hh + 1) * LANES] = _head_op(
                    part[:, cols], gain, rope, rot_dim, scale).astype(o_ref.dtype)
            part = nxt

    def compute():
        if epi == "heads":
            compute_heads()
            return
        x = x_ref[...]
        parts = [jnp.dot(x, weight(w), preferred_element_type=F32) for w in w_refs]
        if nk == 1:
            epilogue(parts)
            return

        @pl.when(k == 0)
        def _():
            for a, p in zip(acc_refs, parts):
                a[...] = p

        @pl.when(k > 0)
        def _():
            for a, p in zip(acc_refs, parts):
                a[...] += p

        @pl.when(k == nk - 1)
        def _():
            epilogue([a[...] for a in acc_refs])

    compute()


def _matmul(x, ws, *, out_dtype, tm, tn, tk=None, epi="none", res=None, gate=None, seq=None, batch=None,
            rows=None, head=None, name="matmul"):
    m, kdim = x.shape
    rows = m if rows is None else rows
    n = ws[0].shape[-1]
    tk = kdim if tk is None else tk
    nk = kdim // tk
    assert rows % tm == 0 and n % tn == 0 and kdim % tk == 0
    n_w = len(ws)

    in_specs = [pl.BlockSpec((tm, tk), lambda i, j, k: (i, k))] + [pl.BlockSpec((tk, tn), lambda i, j, k: (k, j))] * n_w
    args = [x, *ws]
    vmem = 2 * (_nbytes((tm, tk), x.dtype) + n_w * _nbytes((tk, tn), ws[0].dtype) + _nbytes((tm, tn), out_dtype))
    vmem += 2 * n_w * _nbytes((tm, tn), F32)
    if ws[0].dtype != x.dtype:
        vmem += n_w * _nbytes((tk, tn), x.dtype)
    if epi == "gres":
        midx = _mod_index(tm, seq, batch)
        in_specs += [pl.BlockSpec((tm, tn), lambda i, j, k: (i, j)),
                     pl.BlockSpec((1, 1, tn), lambda i, j, k: (midx(i), 0, j))]
        args += [res, gate]
        vmem += 2 * _nbytes((tm, tn), F32)
    head_kw = {}
    if epi == "heads":
        gain, rope, rot_dim = head.get("gain"), head.get("rope"), head.get("rot_dim", 0)
        head_kw = dict(head_norm=gain is not None, rot_dim=rot_dim, scale=head.get("scale", 1.0))
        if gain is not None:
            in_specs.append(pl.BlockSpec((1, LANES), lambda i, j, k: (0, 0)))
            args.append(gain.reshape(1, LANES))
        if rot_dim:
            lat_tiles, seq_tiles = head["lat_rows"] // tm, seq // tm
            assert rope[0].shape[0] >= seq + tm
            tab = lambda i, j, k: (jnp.where(i < lat_tiles, i % seq_tiles, seq_tiles), 0)
            in_specs += [pl.BlockSpec((tm, LANES), tab)] * 3
            args += list(rope)
            vmem += 2 * 3 * _nbytes((tm, LANES), F32)
        vmem += 6 * _nbytes((tm, LANES), F32)
    scratch = [pltpu.VMEM((tm, tn), F32) for _ in range(n_w)] if nk > 1 else []
    vmem += len(scratch) * _nbytes((tm, tn), F32)
    return pl.pallas_call(
        functools.partial(_mm_kernel, n_w=n_w, nk=nk, epi=epi, **head_kw),
        grid=(rows // tm, n // tn, nk),
        in_specs=in_specs,
        out_specs=pl.BlockSpec((tm, tn), lambda i, j, k: (i, j)),
        out_shape=jax.ShapeDtypeStruct((rows, n), out_dtype),
        scratch_shapes=scratch,
        compiler_params=_params(("parallel", "parallel", "arbitrary"), vmem),
        name=name,
    )(*args)


def _rope_tables(seq, grid_w, rot_dim, pad_rows):
    n_rows = seq // grid_w
    rows = jnp.repeat(jnp.arange(n_rows, dtype=F32), grid_w)
    cols = jnp.tile(jnp.arange(grid_w, dtype=F32), n_rows)
    n_freq = rot_dim // 4
    inv_freq = jnp.power(ROPE_THETA, -jnp.arange(n_freq, dtype=F32) / n_freq)
    ang = jnp.concatenate([rows[:, None] * inv_freq, cols[:, None] * inv_freq], axis=-1)
    cos, sin = jnp.cos(ang), jnp.sin(ang)
    zero = jnp.zeros_like(sin)
    reps = LANES // rot_dim
    cos_t = jnp.tile(jnp.concatenate([cos, cos], axis=-1), (1, reps))
    up_t = jnp.tile(jnp.concatenate([-sin, zero], axis=-1), (1, reps))
    down_t = jnp.tile(jnp.concatenate([zero, sin], axis=-1), (1, reps))
    ident = jnp.ones((pad_rows, LANES), F32)
    zpad = jnp.zeros((pad_rows, LANES), F32)
    return (jnp.concatenate([cos_t, ident]), jnp.concatenate([up_t, zpad]), jnp.concatenate([down_t, zpad]))


def _values_t_kernel(x_ref, o_ref, *, n_heads):
    for h in range(n_heads):
        cols = slice(h * LANES, (h + 1) * LANES)
        o_ref[0, cols, :] = x_ref[:, cols].astype(F32).T.astype(o_ref.dtype)


def _values_t(src, col0, n_heads, *, chunk, row0, rows, name):
    hb = math.gcd(col0, n_heads)
    width = hb * LANES
    assert row0 % chunk == 0 and rows % chunk == 0
    vmem = 2 * (_nbytes((chunk, width), src.dtype) + _nbytes((chunk, width), BF16)) + 4 * _nbytes((chunk, LANES), F32)
    return pl.pallas_call(
        functools.partial(_values_t_kernel, n_heads=hb),
        grid=(rows // chunk, n_heads // hb),
        in_specs=[pl.BlockSpec((chunk, width), lambda i, j: (row0 // chunk + i, col0 // hb + j))],
        out_specs=pl.BlockSpec((1, width, chunk), lambda i, j: (i, j, 0)),
        out_shape=jax.ShapeDtypeStruct((rows // chunk, n_heads * LANES, chunk), BF16),
        compiler_params=_params(("parallel", "parallel"), vmem),
        name=name,
    )(src)


def _kv_chunk(seq):
    return _pick(seq // 2, (512, 256, 128))


def _qk(q, k):
    return lax.dot_general(q, k, (((1,), (1,)), ((), ())), preferred_element_type=F32)


def _cat(refs, rows=None):
    parts = [r[...] if rows is None else r[rows, :] for r in refs]
    return parts[0] if len(parts) == 1 else jnp.concatenate(parts, axis=-1)


def _flash_kernel(*refs, mode, n_q, has_lat, tq, tqs, tkc, seq, window):
    refs = list(refs)
    q_refs = [refs.pop(0) for _ in range(n_q)]
    kc_refs = [refs.pop(0) for _ in range(n_q)]
    vc_ref = refs.pop(0)
    if has_lat:
        kl_refs = [refs.pop(0) for _ in range(n_q)]
        vl_ref = refs.pop(0)
    if mode == "window":
        sink_ref = refs.pop(0)
    if mode == "diff":
        lam_ref, subln_ref = refs.pop(0), refs.pop(0)
    o_ref = refs.pop(0)
    n_state = 2 if mode == "diff" else 1
    n_sub = tq // tqs
    chain_ids = [(sub, st) for sub in range(n_sub) for st in range(n_state)]
    per_chain = 7
    state_refs = {cid: tuple(refs[per_chain * i:][:3]) for i, cid in enumerate(chain_ids)}
    score_refs = {cid: tuple(refs[per_chain * i + 3:][:2]) for i, cid in enumerate(chain_ids)}
    smax_refs = {cid: tuple(refs[per_chain * i + 5:][:2]) for i, cid in enumerate(chain_ids)}
    qi = pl.program_id(2)

    def scores(k, only_sub=None):
        out = {}
        for sub, st in chain_ids:
            if only_sub is not None and sub != only_sub:
                continue
            q = _cat(q_refs, pl.ds(sub * tqs, tqs))
            if mode == "diff":
                lane = lax.broadcasted_iota(jnp.int32, q.shape, 1)
                keep = (lane < LANES // 2) if st == 0 else (lane >= LANES // 2)
                q = jnp.where(keep, q, jnp.zeros_like(q))
            out[sub, st] = _qk(k, q)
        return out

    def absorb(s_of, vt, *, first=False, mask=None):
        for cid, s in s_of.items():
            m_ref, l_ref, acc_ref = state_refs[cid]
            if isinstance(s, tuple):
                s, s_max = s
            else:
                if mask is not None:
                    s = jnp.where(mask, s, MASK_VALUE)
                s_max = jnp.max(s, axis=0, keepdims=True)
            if first and mode == "window":
                m_old = jnp.broadcast_to(sink_ref[0][:, :1], (1, tqs))
                m_new = jnp.maximum(m_old, s_max)
                l_old = jnp.exp2(m_old - m_new)
            elif first:
                m_new, l_old = s_max, None
            else:
                m_old = m_ref[...]
                m_new = jnp.maximum(m_old, s_max)
                alpha = jnp.exp2(m_old - m_new)
                l_old = alpha * l_ref[...]
            p = jnp.exp2(s - m_new)
            l_new = jnp.sum(p, axis=0, keepdims=True)
            pv = jnp.dot(vt, p.astype(vt.dtype), preferred_element_type=F32)
            m_ref[...] = m_new
            l_ref[...] = l_new if l_old is None else l_old + l_new
            acc_ref[...] = pv if first else alpha * acc_ref[...] + pv

    def lat_keys(c):
        return _cat(kl_refs, pl.ds(pl.multiple_of(c * tkc, tkc), tkc))

    def stash(slot, s_of):
        for cid, s in s_of.items():
            score_refs[cid][slot][...] = s
            smax_refs[cid][slot][...] = jnp.max(s, axis=0, keepdims=True)

    def fetch(slot):
        return {cid: (score_refs[cid][slot][...], smax_refs[cid][slot][...]) for cid in chain_ids}

    s_ctx = scores(_cat(kc_refs))
    if has_lat and mode == "window":
        tkw = tqs + 2 * window
        blocks = []
        for sub in range(n_sub):
            q_pos0 = qi * tq + sub * tqs
            start = pl.multiple_of(jnp.clip(q_pos0 - window, 0, seq - tkw), LANES)
            kp = start + lax.broadcasted_iota(jnp.int32, (tkw, tqs), 0)
            qp = q_pos0 + lax.broadcasted_iota(jnp.int32, (tkw, tqs), 1)
            blocks.append((scores(_cat(kl_refs, pl.ds(start, tkw)), only_sub=sub), start // LANES,
                           jnp.abs(qp - kp) <= window))
        absorb(s_ctx, vc_ref[0], first=True)
        for s_of, first_chunk, mask in blocks:
            vt = jnp.concatenate([vl_ref[first_chunk + j] for j in range(tkw // LANES)], axis=-1)
            absorb(s_of, vt, mask=mask)
    elif has_lat:
        n_chunks = seq // tkc
        stash(0, scores(lat_keys(0)))
        absorb(s_ctx, vc_ref[0], first=True)

        def body(i, carry):
            c = 2 * i
            stash(1, scores(lat_keys(c + 1)))
            absorb(fetch(0), vl_ref[c])
            stash(0, scores(lat_keys(c + 2)))
            absorb(fetch(1), vl_ref[c + 1])
            return carry

        lax.fori_loop(0, n_chunks // 2 - 1, body, 0)
        stash(1, scores(lat_keys(n_chunks - 1)))
        absorb(fetch(0), vl_ref[n_chunks - 2])
        absorb(fetch(1), vl_ref[n_chunks - 1])
    else:
        absorb(s_ctx, vc_ref[0], first=True)

    for sub in range(n_sub):
        rows = pl.ds(sub * tqs, tqs)
        outs = [(state_refs[sub, st][2][...] / state_refs[sub, st][1][...]).T for st in range(n_state)]
        if mode == "diff":
            lam_rows = lam_ref[...]
            lam = (jnp.exp(jnp.sum(lam_rows[0:1] * lam_rows[1:2], axis=-1, keepdims=True))
                   - jnp.exp(jnp.sum(lam_rows[2:3] * lam_rows[3:4], axis=-1, keepdims=True)) + D_LAMBDA_INIT)
            out = _norm_rows(outs[0] - lam * outs[1], subln_ref[...]) * (1.0 - D_LAMBDA_INIT)
        else:
            out = outs[0]
        o_ref[rows, :] = out.astype(o_ref.dtype)


def _flash(q_parts, k_parts, v, *, n_heads, group, batch, seq, ctx_len, lat_rows, ctx_queries=False,
           mode="softmax", sink=None, lam=None, subln=None, window=0, name="flash"):
    has_lat = not ctx_queries
    ctx_blk0 = lat_rows // ctx_len
    if ctx_queries:
        tq, nq = ctx_len, 1
        q_row = lambda b, h, i: ctx_blk0 + b
        out_rows = batch * ctx_len
        o_row = lambda b, h, i: b
    else:
        tq = _pick(seq, (1024, 512, 256, 128))
        nq = seq // tq
        q_row = lambda b, h, i: b * nq + i
        out_rows = lat_rows
        o_row = q_row
    tqs = min(tq, 256)
    tkc = _kv_chunk(seq)
    n_state = 2 if mode == "diff" else 1
    if mode == "window":
        assert seq >= tqs + 2 * window and window % 16 == 0
    n_q = len(q_parts)
    in_specs, args = [], []
    vmem = 0

    def add(arr, block, imap):
        nonlocal vmem
        in_specs.append(pl.BlockSpec(block, imap))
        args.append(arr)
        vmem += 2 * _nbytes(block, arr.dtype)

    for arr, c0 in q_parts:
        add(arr, (tq, LANES), lambda b, h, i, c0=c0: (q_row(b, h, i), c0 + h))
    for arr, c0, per_head in k_parts:
        add(arr, (ctx_len, LANES), lambda b, h, i, c0=c0, ph=per_head: (ctx_blk0 + b, c0 + (h // group) * ph))
    vt_lat, vt_ctx = v
    add(vt_ctx, (1, LANES, ctx_len), lambda b, h, i: (b, h // group, 0))
    if has_lat:
        for arr, c0, per_head in k_parts:
            add(arr, (seq, LANES), lambda b, h, i, c0=c0, ph=per_head: (b, c0 + (h // group) * ph))
        tkv = vt_lat.shape[-1]
        assert tkv == (LANES if mode == "window" else tkc)
        add(vt_lat, (seq // tkv, LANES, tkv), lambda b, h, i: (b, h // group, 0))
    if mode == "window":
        add(sink, (1, 1, LANES), lambda b, h, i: (h, 0, 0))
    if mode == "diff":
        add(lam, (4, LANES), lambda b, h, i: (0, 0))
        add(subln, (1, LANES), lambda b, h, i: (0, 0))
    n_sub = tq // tqs
    scratch = [pltpu.VMEM((1, tqs), F32), pltpu.VMEM((1, tqs), F32), pltpu.VMEM((LANES, tqs), F32),
               pltpu.VMEM((tkc, tqs), F32), pltpu.VMEM((tkc, tqs), F32),
               pltpu.VMEM((1, tqs), F32), pltpu.VMEM((1, tqs), F32)] * (n_state * n_sub)
    vmem += 2 * n_state * n_sub * _nbytes((tkc, tqs), F32)
    vmem += 2 * _nbytes((tq, LANES), BF16) + 3 * _nbytes((n_state, tq, LANES), F32)
    vmem += 4 * n_state * (tq // tqs) * _nbytes((tqs, max(tkc, ctx_len, tqs + 2 * window)), F32)
    kern = functools.partial(_flash_kernel, mode=mode, n_q=n_q, has_lat=has_lat, tq=tq, tqs=tqs, tkc=tkc,
                             seq=seq, window=window)
    return pl.pallas_call(
        kern,
        grid=(batch, n_heads, nq),
        in_specs=in_specs,
        out_specs=pl.BlockSpec((tq, LANES), lambda b, h, i: (o_row(b, h, i), h)),
        out_shape=jax.ShapeDtypeStruct((out_rows, n_heads * LANES), BF16),
        scratch_shapes=scratch,
        compiler_params=_params(("parallel", "parallel", "parallel"), vmem),
        name=name,
    )(*args)


def _route_kernel(lg_ref, info_ref, cnt_ref, base_ref, *, n_experts, tb):
    step = pl.program_id(0)

    @pl.when(step == 0)
    def _():
        base_ref[...] = jnp.zeros_like(base_ref)

    lane = lax.broadcasted_iota(jnp.int32, (tb, LANES), 1).astype(F32)
    logits = jnp.where(lane < n_experts, lg_ref[...], -jnp.inf)
    v1 = jnp.max(logits, axis=-1, keepdims=True)
    i1 = jnp.min(jnp.where(logits == v1, lane, float(LANES)), axis=-1, keepdims=True)
    hot1 = lane == i1
    rest = jnp.where(hot1, -jnp.inf, logits)
    v2 = jnp.max(rest, axis=-1, keepdims=True)
    i2 = jnp.min(jnp.where(rest == v2, lane, float(LANES)), axis=-1, keepdims=True)
    hot2 = lane == i2
    e = jnp.exp(v2 - v1)
    w1 = 1.0 / (1.0 + e)
    w2 = e / (1.0 + e)
    sel = jnp.where(hot1 | hot2, 1.0, 0.0)
    r = lax.broadcasted_iota(jnp.int32, (tb, tb), 0)
    c = lax.broadcasted_iota(jnp.int32, (tb, tb), 1)
    tri = jnp.where(c < r, 1.0, 0.0).astype(BF16)
    before = jnp.dot(tri, sel.astype(BF16), preferred_element_type=F32) + base_ref[...]
    rank1 = jnp.sum(jnp.where(hot1, before, 0.0), axis=-1, keepdims=True)
    rank2 = jnp.sum(jnp.where(hot2, before, 0.0), axis=-1, keepdims=True)
    base_ref[...] += jnp.sum(sel, axis=0, keepdims=True)
    cnt_ref[...] = jnp.broadcast_to(base_ref[...], cnt_ref.shape)
    info = jnp.zeros((tb, LANES), F32)
    for slot, val in enumerate((i1, i2, rank1, rank2, w1, w2)):
        info = jnp.where(lane == float(slot), val, info)
    info_ref[...] = info


def _route(logits, n_experts):
    n = logits.shape[0]
    tb = _pick(n, (256, 128, 64, 32, 16, 8))
    vmem = 4 * _nbytes((tb, LANES), F32) + 16 * _nbytes((tb, max(tb, LANES)), F32)
    return pl.pallas_call(
        functools.partial(_route_kernel, n_experts=n_experts, tb=tb),
        grid=(n // tb,),
        in_specs=[pl.BlockSpec((tb, LANES), lambda i: (i, 0))],
        out_specs=[pl.BlockSpec((tb, LANES), lambda i: (i, 0)),
                   pl.BlockSpec((8, LANES), lambda i: (0, 0))],
        out_shape=[jax.ShapeDtypeStruct((n, LANES), F32), jax.ShapeDtypeStruct((8, LANES), F32)],
        scratch_shapes=[pltpu.VMEM((1, LANES), F32)],
        compiler_params=_params(("arbitrary",), vmem),
        name="moe_route",
    )(logits)


def _row_copy(src_hbm, row, dst_vmem, slot, sem):
    return pltpu.make_async_copy(src_hbm.at[pl.ds(row, 1)], dst_vmem.at[pl.ds(slot, 1)], sem)


def _gather_kernel(src_ref, h_hbm, o_ref, buf, sem, *, tm):
    base = pl.program_id(0) * tm

    def issue(r, carry):
        _row_copy(h_hbm, src_ref[base + r], buf, r, sem).start()
        return carry

    lax.fori_loop(0, tm, issue, 0, unroll=DMA_LOOP_UNROLL)

    def drain(r, carry):
        _row_copy(h_hbm, 0, buf, r, sem).wait()
        return carry

    lax.fori_loop(0, tm, drain, 0, unroll=DMA_LOOP_UNROLL)
    o_ref[...] = buf[...].astype(o_ref.dtype)


def _gather_rows(src_rows, h, *, tm):
    p = src_rows.shape[0]
    d = h.shape[1]
    vmem = _nbytes((tm, d), F32) + 2 * _nbytes((tm, d), BF16)
    gs = pltpu.PrefetchScalarGridSpec(
        num_scalar_prefetch=1, grid=(p // tm,),
        in_specs=[pl.BlockSpec(memory_space=pl.ANY)],
        out_specs=pl.BlockSpec((tm, d), lambda i, src: (i, 0)),
        scratch_shapes=[pltpu.VMEM((tm, d), F32), pltpu.SemaphoreType.DMA(())])
    return pl.pallas_call(
        functools.partial(_gather_kernel, tm=tm), grid_spec=gs,
        out_shape=jax.ShapeDtypeStruct((p, d), BF16),
        compiler_params=_params(("arbitrary",), vmem),
        name="moe_gather",
    )(src_rows, h)


def _combine_kernel(d1_ref, d2_ref, y_hbm, x_ref, info_ref, gate_ref, g_ref, o_ref, buf1, buf2, sem, *, tb):
    base = pl.program_id(0) * tb

    def issue(r, carry):
        _row_copy(y_hbm, d1_ref[base + r], buf1, r, sem.at[0]).start()
        _row_copy(y_hbm, d2_ref[base + r], buf2, r, sem.at[1]).start()
        return carry

    lax.fori_loop(0, tb, issue, 0, unroll=DMA_LOOP_UNROLL)

    def drain(r, carry):
        _row_copy(y_hbm, 0, buf1, r, sem.at[0]).wait()
        _row_copy(y_hbm, 0, buf2, r, sem.at[1]).wait()
        return carry

    lax.fori_loop(0, tb, drain, 0, unroll=DMA_LOOP_UNROLL)
    info = info_ref[...]
    w1, w2 = info[:, 4:5], info[:, 5:6]
    x = x_ref[...] + gate_ref[0] * (w1 * buf1[...] + w2 * buf2[...])
    o_ref[...] = _norm_rows(x, g_ref[...])


def _combine(dest1, dest2, y, x, info, gate, final_norm, *, seq, batch):
    n, d = x.shape
    tb = _pick(seq, (256, 128, 64, 32, 16, 8))
    midx = _mod_index(tb, seq, batch)
    vmem = 2 * _nbytes((tb, d), F32) * 3 + 2 * _nbytes((tb, d), F32) + 2 * _nbytes((tb, LANES), F32)
    gs = pltpu.PrefetchScalarGridSpec(
        num_scalar_prefetch=2, grid=(n // tb,),
        in_specs=[pl.BlockSpec(memory_space=pl.ANY),
                  pl.BlockSpec((tb, d), lambda i, a, b: (i, 0)),
                  pl.BlockSpec((tb, LANES), lambda i, a, b: (i, 0)),
                  pl.BlockSpec((1, 1, d), lambda i, a, b: (midx(i), 0, 0)),
                  pl.BlockSpec((1, d), lambda i, a, b: (0, 0))],
        out_specs=pl.BlockSpec((tb, d), lambda i, a, b: (i, 0)),
        scratch_shapes=[pltpu.VMEM((tb, d), F32), pltpu.VMEM((tb, d), F32), pltpu.SemaphoreType.DMA((2,))])
    return pl.pallas_call(
        functools.partial(_combine_kernel, tb=tb), grid_spec=gs,
        out_shape=jax.ShapeDtypeStruct((n, d), F32),
        compiler_params=_params(("arbitrary",), vmem),
        name="moe_combine_norm",
    )(dest1, dest2, y, x, info, gate, final_norm.reshape(1, d))


def _expert_mm_kernel(te_ref, used_ref, x_ref, *refs, n_w, swiglu):
    w_refs, o_ref, wbf_refs = refs[:n_w], refs[n_w], refs[n_w + 1:]
    i = pl.program_id(1)
    live = i < used_ref[0]
    fresh = jnp.logical_or(i == 0, te_ref[i] != te_ref[jnp.maximum(i - 1, 0)])

    @pl.when(jnp.logical_and(live, fresh))
    def _():
        for w, wb in zip(w_refs, wbf_refs):
            wb[...] = w[0].astype(wb.dtype)

    @pl.when(live)
    def _():
        x = x_ref[...]
        parts = [jnp.dot(x, wb[...], preferred_element_type=F32) for wb in wbf_refs]
        out = parts[0] * jax.nn.sigmoid(parts[0]) * parts[1] if swiglu else parts[0]
        o_ref[...] = out.astype(o_ref.dtype)

    @pl.when(jnp.logical_not(live))
    def _():
        o_ref[...] = jnp.zeros_like(o_ref)


def _expert_matmul(x, ws, tile_expert, tiles_used, *, out_dtype, tm, tn, swiglu=False, name):
    rows, kdim = x.shape
    n = ws[0].shape[-1]
    n_w = len(ws)
    assert rows % tm == 0 and n % tn == 0
    vmem = 2 * (_nbytes((tm, kdim), x.dtype) + n_w * _nbytes((kdim, tn), ws[0].dtype) + _nbytes((tm, tn), out_dtype))
    vmem += n_w * (_nbytes((kdim, tn), BF16) + 2 * _nbytes((tm, tn), F32))
    gs = pltpu.PrefetchScalarGridSpec(
        num_scalar_prefetch=2, grid=(n // tn, rows // tm),
        in_specs=[pl.BlockSpec((tm, kdim), lambda j, i, te, nu: (i, 0))]
        + [pl.BlockSpec((1, kdim, tn), lambda j, i, te, nu: (te[i], 0, j))] * n_w,
        out_specs=pl.BlockSpec((tm, tn), lambda j, i, te, nu: (i, j)),
        scratch_shapes=[pltpu.VMEM((kdim, tn), BF16)] * n_w)
    return pl.pallas_call(
        functools.partial(_expert_mm_kernel, n_w=n_w, swiglu=swiglu), grid_spec=gs,
        out_shape=jax.ShapeDtypeStruct((rows, n), out_dtype),
        compiler_params=_params(("arbitrary", "arbitrary"), vmem),
        name=name,
    )(tile_expert, tiles_used, x, *ws)


def _moe(h_f32, logits, x, gate, final_norm, w_gate, w_up, w_down, *, dims):
    n, d = x.shape
    n_exp = dims.n_experts
    tm = _pick(n, (512, 256, 128, 64, 32, 16))
    p = 2 * n + n_exp * tm
    info, counts = _route(logits, n_exp)
    cnt = counts[0, :n_exp].astype(jnp.int32)
    padded = ((cnt + tm - 1) // tm) * tm
    ends = jnp.cumsum(padded)
    starts = ends - padded
    e1, e2 = info[:, 0].astype(jnp.int32), info[:, 1].astype(jnp.int32)
    dest1 = starts[e1] + info[:, 2].astype(jnp.int32)
    dest2 = starts[e2] + info[:, 3].astype(jnp.int32)
    token = jnp.arange(n, dtype=jnp.int32)
    src = jnp.zeros((p,), jnp.int32).at[dest1].set(token).at[dest2].set(token)
    tile_start = jnp.arange(p // tm, dtype=jnp.int32) * tm
    tile_expert = jnp.minimum(jnp.sum(tile_start[:, None] >= ends[None, :], axis=1), n_exp - 1).astype(jnp.int32)
    tiles_used = (ends[-1:] // tm).astype(jnp.int32)

    xs = _gather_rows(src, h_f32, tm=_pick(tm, (256, 128, 64, 32, 16)))
    f = w_gate.shape[-1]
    act = _expert_matmul(xs, [w_gate, w_up], tile_expert, tiles_used, out_dtype=BF16, tm=tm,
                         tn=_pick(f, (512, 256, 128)), swiglu=True, name="moe_gate_up")
    y = _expert_matmul(act, [w_down], tile_expert, tiles_used, out_dtype=F32, tm=tm,
                       tn=_pick(d, (512, 256, 128)), name="moe_down")
    return _combine(dest1, dest2, y, x, info, gate, final_norm, seq=dims.seq, batch=dims.batch)


def _pad_cols(w, n):
    return w if w.shape[-1] == n else jnp.pad(w, [(0, 0)] * (w.ndim - 1) + [(0, n - w.shape[-1])])


def _forward(dims, x, c, ctx, c_ctx,
             ada_w0, ada_b0, norm_mix0, norm_ffn0, w_in0, a_q_norm, a_k_norm, b_q_norm, b_kv_norm,
             w_uq, w_ukv, w_o0, ffn_w_gate, ffn_w_up, ffn_w_down,
             ada_w1, ada_b1, norm_mix1, norm_ffn1, w_in1, c_sink, d_lam_q1, d_lam_k1, d_lam_q2, d_lam_k2,
             d_subln, w_o1, router_w, moe_w_gate, moe_w_up, moe_w_down, final_norm):
    dm, bsz, seq, ctx_len = dims.d_model, dims.batch, dims.seq, dims.ctx_len
    nl, nc = bsz * seq, bsz * ctx_len
    nt = nl + nc
    ha, hka, hb = dims.a_heads, dims.a_kv_heads, dims.b_heads
    hc, hkc, hd = dims.c_heads, dims.c_kv_heads, dims.d_heads
    big = lambda n: _pick(n, (1024, 512, 256, 128, 64, 32, 16))
    tm_all, tm_lat = big(math.gcd(nt, seq)), big(math.gcd(nl, seq))

    xa = jnp.concatenate([x.reshape(nl, dm), ctx.reshape(nc, dm)], axis=0)
    cond = jnp.concatenate([c, c_ctx[None, :], jnp.zeros((8 - bsz - 1, dm), F32)], axis=0)

    def mods(ada_w, ada_b):
        m = _modulation(cond, ada_w, ada_b).reshape(8, N_MOD, 1, dm)
        return [m[:, i] for i in range(N_MOD)]

    rope128 = _rope_tables(seq, dims.grid_w, 128, tm_all)
    rope64 = _rope_tables(seq, dims.grid_w, 64, tm_all)
    flash = functools.partial(_flash, batch=bsz, seq=seq, ctx_len=ctx_len, lat_rows=nl)

    def project(src, w, *, rows=nt, out_dtype=BF16, name, **head):
        n = w.shape[1]
        tm = tm_all if rows == nt else tm_lat
        if head:
            return _matmul(src, [w.astype(BF16)], out_dtype=out_dtype, tm=tm, tn=_pick(n, (1024, 512, 256, 128)),
                           rows=rows, epi="heads", head=dict(head, lat_rows=nl), seq=seq, name=name)
        return _matmul(src, [w.astype(BF16)], out_dtype=out_dtype, tm=tm, tn=_pick(n, (512, 256, 128)),
                       rows=rows, name=name)

    def values_t(src, col0, n_heads, name, chunk=_kv_chunk(seq)):
        return (_values_t(src, col0, n_heads, chunk=chunk, row0=0, rows=nl, name=name),
                _values_t(src, col0, n_heads, chunk=ctx_len, row0=nl, rows=nc, name=name + "_ctx"))

    sh1, sc1, g1, sh2, sc2, g2 = mods(ada_w0, ada_b0)
    h = _norm_mod(xa, norm_mix0, sh1, sc1, rows=nt, seq=seq, batch=bsz)
    na_q, na_kv, rq, rkv = ha * HEAD_DIM, hka * HEAD_DIM, dims.b_q_rank, dims.b_kv_rank
    c_ak, c_av, c_cq, c_ckv, c_kr = na_q, na_q + na_kv, na_q + 2 * na_kv, na_q + 2 * na_kv + rq, na_q + 2 * na_kv + rq + rkv
    aq = project(h, w_in0[:, :c_ak], gain=a_q_norm, rope=rope128, rot_dim=128, scale=LOG2E * HEAD_DIM ** -0.5,
                 name="proj_aq")
    ak = project(h, w_in0[:, c_ak:c_av], gain=a_k_norm, rope=rope128, rot_dim=128, name="proj_ak")
    bk_rope = project(h, _pad_cols(w_in0[:, c_kr:], LANES), rope=rope64, rot_dim=64, name="proj_bk_rope")
    rest = project(h, jnp.concatenate([w_in0[:, c_cq:c_kr], w_in0[:, c_av:c_cq]], axis=1), out_dtype=F32,
                   name="proj_ranks_av")
    av = values_t(rest, (rq + rkv) // HEAD_DIM, hka, "vt_a")
    cq = _rank_norm(rest, 0, rq, b_q_norm)
    ckv = _rank_norm(rest, rq, rkv, b_kv_norm)
    b_scale = LOG2E * (HEAD_DIM + 64) ** -0.5
    uq = w_uq.reshape(rq, hb, HEAD_DIM + 64) * b_scale
    bq_nope = project(cq, uq[:, :, :HEAD_DIM].reshape(rq, hb * HEAD_DIM), name="b_up_q_nope")
    bq_rope = project(cq, _pad_cols(uq[:, :, HEAD_DIM:], LANES).reshape(rq, hb * LANES), rope=rope64, rot_dim=64,
                      name="b_up_q_rope")
    ukv = w_ukv.reshape(rkv, hb, 2 * HEAD_DIM)
    ukv = jnp.concatenate([ukv[:, :, :HEAD_DIM].reshape(rkv, hb * HEAD_DIM),
                           ukv[:, :, HEAD_DIM:].reshape(rkv, hb * HEAD_DIM)], axis=1)
    bkv = project(ckv, ukv, name="b_up_kv")

    a_lat = flash([(aq, 0)], [(ak, 0, 1)], av, n_heads=ha, group=ha // hka, name="attn_a")
    a_ctx = flash([(aq, 0)], [(ak, 0, 1)], av, n_heads=ha, group=ha // hka, ctx_queries=True, name="attn_a_ctx")
    b_args = ([(bq_nope, 0), (bq_rope, 0)], [(bkv, 0, 1), (bk_rope, 0, 0)], values_t(bkv, hb, hb, "vt_b"))
    b_lat = flash(*b_args, n_heads=hb, group=1, name="attn_b")
    b_ctx = flash(*b_args, n_heads=hb, group=1, ctx_queries=True, name="attn_b_ctx")
    mixed = jnp.concatenate([jnp.concatenate([a_lat, b_lat], axis=1),
                             jnp.concatenate([a_ctx, b_ctx], axis=1)], axis=0)
    xa = _matmul(mixed, [w_o0], out_dtype=F32, tm=tm_all, tn=_pick(dm, (512, 256, 128)),
                 epi="gres", res=xa, gate=g1, seq=seq, batch=bsz, name="out_proj0")

    h = _norm_mod(xa, norm_ffn0, sh2, sc2, rows=nt, seq=seq, batch=bsz)
    ffn = dims.ffn_dim
    act = _matmul(h, [ffn_w_gate, ffn_w_up], out_dtype=BF16, tm=tm_all, tn=_pick(ffn, (256, 128)), epi="swiglu",
                  name="ffn_gate_up")
    tk_down = ffn // 2 if (ffn // 2) % LANES == 0 and ffn > 2048 else ffn
    xa = _matmul(act, [ffn_w_down.astype(BF16)], out_dtype=F32, tm=tm_all, tn=_pick(dm, (512, 256, 128)), tk=tk_down,
                 epi="gres", res=xa, gate=g2, seq=seq, batch=bsz, name="ffn_down")

    sh1, sc1, g1, sh2, sc2, g2 = mods(ada_w1, ada_b1)
    h = _norm_mod(xa, norm_mix1, sh1, sc1, rows=nt, seq=seq, batch=bsz)
    bounds = [0]
    for n_heads in (hc, hkc, hkc, hd, hd, hd):
        bounds.append(bounds[-1] + n_heads * HEAD_DIM)
    w_cq, w_ck, w_cv, w_dq, w_dk, w_dv = (w_in1[:, lo:hi] for lo, hi in zip(bounds[:-1], bounds[1:]))
    cq_ = project(h, w_cq, rows=nl, rope=rope128, rot_dim=128, scale=LOG2E * HEAD_DIM ** -0.5, name="proj_cq")
    ck_ = project(h, w_ck, rope=rope128, rot_dim=128, name="proj_ck")
    dq_ = project(h, w_dq, rows=nl, rope=rope64, rot_dim=64, scale=LOG2E * 64 ** -0.5, name="proj_dq")
    dk_ = project(h, w_dk, rope=rope64, rot_dim=64, name="proj_dk")
    values = project(h, jnp.concatenate([w_cv, w_dv], axis=1), name="proj_cv_dv")
    cv_ = values_t(values, 0, hkc, "vt_c", chunk=LANES)
    dv_ = values_t(values, hkc, hd, "vt_d")
    sink = jnp.broadcast_to(LOG2E * c_sink.astype(F32)[:, None, None], (hc, 1, LANES))
    lam = jnp.stack([_pad_cols(v.astype(F32)[None, :], LANES)[0] for v in (d_lam_q1, d_lam_k1, d_lam_q2, d_lam_k2)])
    c_out = flash([(cq_, 0)], [(ck_, 0, 1)], cv_, n_heads=hc, group=hc // hkc, mode="window", sink=sink,
                  window=dims.window, name="attn_c")
    d_out = flash([(dq_, 0)], [(dk_, 0, 1)], dv_, n_heads=hd, group=1, mode="diff", lam=lam,
                  subln=d_subln.reshape(1, LANES), name="attn_d")
    mixed = jnp.concatenate([c_out, d_out], axis=1)
    xl = _matmul(mixed, [w_o1], out_dtype=F32, tm=tm_lat, tn=_pick(dm, (512, 256, 128)),
                 epi="gres", res=xa, gate=g1, seq=seq, batch=bsz, rows=nl, name="out_proj1")

    rw = _pad_cols(router_w, LANES)
    rw_hi = rw.astype(BF16)
    rw_lo = (rw - rw_hi.astype(F32)).astype(BF16)
    hf32, logits = _norm_mod(xl, norm_ffn1, sh2, sc2, rows=nl, seq=seq, batch=bsz, router=(rw_hi, rw_lo))
    out = _moe(hf32, logits, xl, g2, final_norm, moe_w_gate, moe_w_up, moe_w_down, dims=dims)
    return out.reshape(bsz, seq, dm)


def _rank_norm_kernel(x_ref, g_ref, o_ref):
    o_ref[...] = _norm_rows(x_ref[...], g_ref[...]).astype(o_ref.dtype)


def _rank_norm(src, col0, width, gain):
    rows = src.shape[0]
    tm = _pick(rows, (256, 128, 64, 32, 16))
    assert col0 % width == 0
    vmem = 2 * (_nbytes((tm, width), F32) + _nbytes((tm, width), BF16))
    return pl.pallas_call(
        _rank_norm_kernel,
        grid=(rows // tm,),
        in_specs=[pl.BlockSpec((tm, width), lambda i: (i, col0 // width)),
                  pl.BlockSpec((1, width), lambda i: (0, 0))],
        out_specs=pl.BlockSpec((tm, width), lambda i: (i, 0)),
        out_shape=jax.ShapeDtypeStruct((rows, width), BF16),
        compiler_params=_params(("parallel",), vmem),
        name="rank_norm",
    )(src, gain.reshape(1, width))


_DIMS = Dims(d_model=4096, batch=4, seq=4096, ctx_len=256, grid_w=64, a_heads=16, a_kv_heads=4, b_heads=16,
             b_q_rank=1536, b_kv_rank=512, c_heads=16, c_kv_heads=4, window=128, d_heads=16, ffn_dim=11008,
             n_experts=8, expert_dim=3584)


def kernel(x, c, ctx, c_ctx, ada_w0, ada_b0, norm_mix0, norm_ffn0, w_in0, a_q_norm, a_k_norm, b_q_norm, b_kv_norm, w_uq, w_ukv, w_o0, ffn_w_gate, ffn_w_up, ffn_w_down, ada_w1, ada_b1, norm_mix1, norm_ffn1, w_in1, c_sink, d_lam_q1, d_lam_k1, d_lam_q2, d_lam_k2, d_subln, w_o1, router_w, moe_w_gate, moe_w_up, moe_w_down, final_norm):
    return _forward(_DIMS, x, c, ctx, c_ctx, ada_w0, ada_b0, norm_mix0, norm_ffn0, w_in0, a_q_norm, a_k_norm,
                    b_q_norm, b_kv_norm, w_uq, w_ukv, w_o0, ffn_w_gate, ffn_w_up, ffn_w_down, ada_w1, ada_b1,
                    norm_mix1, norm_ffn1, w_in1, c_sink, d_lam_q1, d_lam_k1, d_lam_q2, d_lam_k2, d_subln, w_o1,
                    router_w, moe_w_gate, moe_w_up, moe_w_down, final_norm)
```

```python
import functools
import math
from typing import NamedTuple

import jax
import jax.numpy as jnp
from jax import lax
from jax.experimental import pallas as pl
from jax.experimental.pallas import tpu as pltpu

F32 = jnp.float32
BF16 = jnp.bfloat16

LANES = 128
HEAD_DIM = 128
ROPE_THETA = 10000.0
NORM_EPS = 1e-6
MASK_VALUE = -1e30
LOG2E = math.log2(math.e)
N_MOD = 6
D_LAYER_INDEX = 1
D_LAMBDA_INIT = 0.8 - 0.6 * math.exp(-0.3 * D_LAYER_INDEX)
VMEM_HEADROOM = 6 << 20
DMA_LOOP_UNROLL = 8
HEAD_SUBBLOCK = 256


class Dims(NamedTuple):
    d_model: int
    batch: int
    seq: int
    ctx_len: int
    grid_w: int
    a_heads: int
    a_kv_heads: int
    b_heads: int
    b_q_rank: int
    b_kv_rank: int
    c_heads: int
    c_kv_heads: int
    window: int
    d_heads: int
    ffn_dim: int
    n_experts: int
    expert_dim: int


def _pick(n, prefs):
    for p in prefs:
        if n % p == 0:
            return p
    raise ValueError(f"no tile in {prefs} divides {n}")


def _params(sem, vmem_bytes):
    return pltpu.CompilerParams(dimension_semantics=sem, vmem_limit_bytes=int(vmem_bytes) + VMEM_HEADROOM)


def _nbytes(shape, dtype):
    return math.prod(shape) * jnp.dtype(dtype).itemsize


def _mod_kernel(c_ref, w_ref, b_ref, o_ref):
    c = c_ref[...]
    s = (c * jax.nn.sigmoid(c)).astype(BF16)
    o_ref[...] = jnp.dot(s, w_ref[...].astype(BF16), preferred_element_type=F32) + b_ref[...]


def _modulation(cond, ada_w, ada_b):
    rows, d = cond.shape
    n = ada_w.shape[1]
    tn = _pick(n, (512, 256, 128))
    vmem = 2 * (_nbytes((d, tn), F32) + _nbytes((rows, tn), F32) * 2) + _nbytes((rows, d), F32) * 2
    return pl.pallas_call(
        _mod_kernel,
        grid=(n // tn,),
        in_specs=[pl.BlockSpec((rows, d), lambda j: (0, 0)),
                  pl.BlockSpec((d, tn), lambda j: (0, j)),
                  pl.BlockSpec((1, tn), lambda j: (0, j))],
        out_specs=pl.BlockSpec((rows, tn), lambda j: (0, j)),
        out_shape=jax.ShapeDtypeStruct((rows, n), F32),
        compiler_params=_params(("parallel",), vmem),
        name="modulation",
    )(cond, ada_w, ada_b.reshape(1, n))


def _norm_rows(x, g):
    var = jnp.mean(x * x, axis=-1, keepdims=True)
    return x * lax.rsqrt(var + NORM_EPS) * g


def _norm_mod_kernel(x_ref, g_ref, sh_ref, sc_ref, o_ref):
    y = _norm_rows(x_ref[...], g_ref[...])
    o_ref[...] = (y * (1.0 + sc_ref[0]) + sh_ref[0]).astype(o_ref.dtype)


def _norm_mod_router_kernel(x_ref, g_ref, sh_ref, sc_ref, rhi_ref, rlo_ref, of_ref, lg_ref):
    y = _norm_rows(x_ref[...], g_ref[...])
    h = y * (1.0 + sc_ref[0]) + sh_ref[0]
    of_ref[...] = h
    hi = h.astype(BF16)
    lo = (h - hi.astype(F32)).astype(BF16)
    lg_ref[...] = (jnp.dot(hi, rhi_ref[...], preferred_element_type=F32)
                   + jnp.dot(hi, rlo_ref[...], preferred_element_type=F32)
                   + jnp.dot(lo, rhi_ref[...], preferred_element_type=F32))


def _mod_index(tm, seq, batch):
    return lambda i: jnp.minimum((i * tm) // seq, batch)


def _norm_mod(x, g, shift, scale, *, rows, seq, batch, router=None):
    m, d = x.shape
    tm = _pick(math.gcd(rows, seq), (256, 128, 64, 32, 16))
    midx = _mod_index(tm, seq, batch)
    in_specs = [pl.BlockSpec((tm, d), lambda i: (i, 0)),
                pl.BlockSpec((1, d), lambda i: (0, 0)),
                pl.BlockSpec((1, 1, d), lambda i: (midx(i), 0, 0)),
                pl.BlockSpec((1, 1, d), lambda i: (midx(i), 0, 0))]
    vmem = 2 * (_nbytes((tm, d), F32) + _nbytes((tm, d), BF16)) + 6 * _nbytes((1, d), F32)
    if router is None:
        return pl.pallas_call(
            _norm_mod_kernel,
            grid=(rows // tm,),
            in_specs=in_specs,
            out_specs=pl.BlockSpec((tm, d), lambda i: (i, 0)),
            out_shape=jax.ShapeDtypeStruct((rows, d), BF16),
            compiler_params=_params(("parallel",), vmem),
            name="norm_modulate",
        )(x, g.reshape(1, d), shift, scale)
    rhi, rlo = router
    vmem += 2 * (_nbytes((tm, d), F32) + _nbytes((tm, LANES), F32)) + 4 * _nbytes((d, LANES), BF16)
    return pl.pallas_call(
        _norm_mod_router_kernel,
        grid=(rows // tm,),
        in_specs=in_specs + [pl.BlockSpec((d, LANES), lambda i: (0, 0)),
                             pl.BlockSpec((d, LANES), lambda i: (0, 0))],
        out_specs=[pl.BlockSpec((tm, d), lambda i: (i, 0)),
                   pl.BlockSpec((tm, LANES), lambda i: (i, 0))],
        out_shape=[jax.ShapeDtypeStruct((rows, d), F32),
                   jax.ShapeDtypeStruct((rows, LANES), F32)],
        compiler_params=_params(("parallel",), vmem),
        name="norm_modulate_router",
    )(x, g.reshape(1, d), shift, scale, rhi, rlo)


def _head_op(x, gain, rope, rot_dim, scale):
    if gain is not None:
        x = _norm_rows(x, gain)
    if rot_dim:
        cos, up, down = rope
        half = rot_dim // 2
        x = x * cos + pltpu.roll(x, LANES - half, 1) * up + pltpu.roll(x, half, 1) * down
    return x if scale == 1.0 else x * scale


def _mm_kernel(*refs, n_w, nk, epi, x_parts=(False,), lat_tiles=0, head_norm=False, rot_dim=0, scale=1.0):
    refs = list(refs)
    x_refs = []
    for has_ctx in x_parts:
        x_refs.append((refs.pop(0), refs.pop(0) if has_ctx else None))
    x_dtype = x_refs[0][0].dtype
    w_refs = refs[:n_w]

    def load_x():
        on_lat = pl.program_id(0) < lat_tiles
        parts = [lat[...] if ctx is None else jnp.where(on_lat, lat[...], ctx[...]) for lat, ctx in x_refs]
        return parts[0] if len(parts) == 1 else jnp.concatenate(parts, axis=-1)

    def weight(w_ref, cols=slice(None)):
        w = w_ref[:, cols]
        return w if w.dtype == x_dtype else w.astype(x_dtype)

    pos = n_w
    if epi == "gres":
        res_ref, gate_ref = refs[pos], refs[pos + 1]
        pos += 2
    if epi == "heads":
        gain_ref = refs[pos] if head_norm else None
        pos += int(head_norm)
        rope_refs = refs[pos:pos + 3] if rot_dim else None
        pos += 3 if rot_dim else 0
    o_ref = refs[pos]
    acc_refs = refs[pos + 1:]
    k = pl.program_id(2)

    def epilogue(parts):
        if epi == "swiglu":
            g, u = parts
            out = g * jax.nn.sigmoid(g) * u
        elif epi == "gres":
            out = res_ref[...] + gate_ref[0] * parts[0]
        else:
            out = parts[0]
        o_ref[...] = out.astype(o_ref.dtype)

    def compute_heads():
        x = load_x()
        gain = gain_ref[...] if head_norm else None
        rope = tuple(r[...] for r in rope_refs) if rot_dim else None
        tn = o_ref.shape[1]
        sub = HEAD_SUBBLOCK if tn % HEAD_SUBBLOCK == 0 else tn
        starts = list(range(0, tn, sub))
        dot = lambda c: jnp.dot(x, weight(w_refs[0], slice(c, c + sub)), preferred_element_type=F32)
        part = dot(starts[0])
        for idx, c in enumerate(starts):
            nxt = dot(starts[idx + 1]) if idx + 1 < len(starts) else None
            for hh in range(sub // LANES):
                cols = slice(hh * LANES, (hh + 1) * LANES)
                o_ref[:, c + hh * LANES:c + (hh + 1) * LANES] = _head_op(
                    part[:, cols], gain, rope, rot_dim, scale).astype(o_ref.dtype)
            part = nxt

    def compute():
        if epi == "heads":
            compute_heads()
            return
        x = load_x()
        parts = [jnp.dot(x, weight(w), preferred_element_type=F32) for w in w_refs]
        if nk == 1:
            epilogue(parts)
            return

        @pl.when(k == 0)
        def _():
            for a, p in zip(acc_refs, parts):
                a[...] = p

        @pl.when(k > 0)
        def _():
            for a, p in zip(acc_refs, parts):
                a[...] += p

        @pl.when(k == nk - 1)
        def _():
            epilogue([a[...] for a in acc_refs])

    compute()


def _matmul(x, ws, *, out_dtype, tm, tn, tk=None, epi="none", res=None, gate=None, seq=None, batch=None,
            rows=None, head=None, name="matmul"):
    parts = x if isinstance(x, list) else [x]
    parts = [p if isinstance(p, tuple) else (p, None) for p in parts]
    x_dtype = parts[0][0].dtype
    kdim = sum(lat.shape[1] for lat, _ in parts)
    lat_tiles = parts[0][0].shape[0] // tm
    m = parts[0][0].shape[0] + (0 if parts[0][1] is None else parts[0][1].shape[0])
    rows = m if rows is None else rows
    n = ws[0].shape[-1]
    tk = kdim if tk is None else tk
    nk = kdim // tk
    assert rows % tm == 0 and n % tn == 0 and kdim % tk == 0 and (len(parts) == 1 or nk == 1)
    n_w = len(ws)

    in_specs, args = [], []
    for lat, ctx in parts:
        width = tk if len(parts) == 1 else lat.shape[1]
        if ctx is None:
            in_specs.append(pl.BlockSpec((tm, width), lambda i, j, k: (i, k)))
            args.append(lat)
        else:
            in_specs += [pl.BlockSpec((tm, width), lambda i, j, k: (jnp.minimum(i, lat_tiles - 1), k)),
                         pl.BlockSpec((tm, width), lambda i, j, k: (jnp.maximum(i - lat_tiles, 0), k))]
            args += [lat, ctx]
    in_specs += [pl.BlockSpec((tk, tn), lambda i, j, k: (k, j))] * n_w
    args += list(ws)
    n_x = len(args) - n_w
    vmem = 2 * (n_x * _nbytes((tm, tk), x_dtype) // len(parts) + n_w * _nbytes((tk, tn), ws[0].dtype)
                + _nbytes((tm, tn), out_dtype))
    vmem += _nbytes((tm, tk), x_dtype) if n_x > 1 else 0
    vmem += 2 * n_w * _nbytes((tm, tn), F32)
    if ws[0].dtype != x_dtype:
        vmem += n_w * _nbytes((tk, tn), x_dtype)
    if epi == "gres":
        midx = _mod_index(tm, seq, batch)
        in_specs += [pl.BlockSpec((tm, tn), lambda i, j, k: (i, j)),
                     pl.BlockSpec((1, 1, tn), lambda i, j, k: (midx(i), 0, j))]
        args += [res, gate]
        vmem += 2 * _nbytes((tm, tn), F32)
    head_kw = {}
    if epi == "heads":
        gain, rope, rot_dim = head.get("gain"), head.get("rope"), head.get("rot_dim", 0)
        head_kw = dict(head_norm=gain is not None, rot_dim=rot_dim, scale=head.get("scale", 1.0))
        if gain is not None:
            in_specs.append(pl.BlockSpec((1, LANES), lambda i, j, k: (0, 0)))
            args.append(gain.reshape(1, LANES))
        if rot_dim:
            lat_tiles, seq_tiles = head["lat_rows"] // tm, seq // tm
            assert rope[0].shape[0] >= seq + tm
            tab = lambda i, j, k: (jnp.where(i < lat_tiles, i % seq_tiles, seq_tiles), 0)
            in_specs += [pl.BlockSpec((tm, LANES), tab)] * 3
            args += list(rope)
            vmem += 2 * 3 * _nbytes((tm, LANES), F32)
        vmem += 6 * _nbytes((tm, LANES), F32)
    scratch = [pltpu.VMEM((tm, tn), F32) for _ in range(n_w)] if nk > 1 else []
    vmem += len(scratch) * _nbytes((tm, tn), F32)
    return pl.pallas_call(
        functools.partial(_mm_kernel, n_w=n_w, nk=nk, epi=epi, x_parts=tuple(ctx is not None for _, ctx in parts),
                          lat_tiles=lat_tiles, **head_kw),
        grid=(rows // tm, n // tn, nk),
        in_specs=in_specs,
        out_specs=pl.BlockSpec((tm, tn), lambda i, j, k: (i, j)),
        out_shape=jax.ShapeDtypeStruct((rows, n), out_dtype),
        scratch_shapes=scratch,
        compiler_params=_params(("parallel", "parallel", "arbitrary"), vmem),
        name=name,
    )(*args)


def _rope_tables(seq, grid_w, rot_dim, pad_rows):
    n_rows = seq // grid_w
    rows = jnp.repeat(jnp.arange(n_rows, dtype=F32), grid_w)
    cols = jnp.tile(jnp.arange(grid_w, dtype=F32), n_rows)
    n_freq = rot_dim // 4
    inv_freq = jnp.power(ROPE_THETA, -jnp.arange(n_freq, dtype=F32) / n_freq)
    ang = jnp.concatenate([rows[:, None] * inv_freq, cols[:, None] * inv_freq], axis=-1)
    cos, sin = jnp.cos(ang), jnp.sin(ang)
    zero = jnp.zeros_like(sin)
    reps = LANES // rot_dim
    cos_t = jnp.tile(jnp.concatenate([cos, cos], axis=-1), (1, reps))
    up_t = jnp.tile(jnp.concatenate([-sin, zero], axis=-1), (1, reps))
    down_t = jnp.tile(jnp.concatenate([zero, sin], axis=-1), (1, reps))
    ident = jnp.ones((pad_rows, LANES), F32)
    zpad = jnp.zeros((pad_rows, LANES), F32)
    return (jnp.concatenate([cos_t, ident]), jnp.concatenate([up_t, zpad]), jnp.concatenate([down_t, zpad]))


def _values_t_kernel(x_ref, o_ref, *, n_heads):
    for h in range(n_heads):
        cols = slice(h * LANES, (h + 1) * LANES)
        o_ref[0, cols, :] = x_ref[:, cols].astype(F32).T.astype(o_ref.dtype)


def _values_t(src, col0, n_heads, *, chunk, row0, rows, name):
    hb = math.gcd(col0, n_heads)
    width = hb * LANES
    assert row0 % chunk == 0 and rows % chunk == 0
    vmem = 2 * (_nbytes((chunk, width), src.dtype) + _nbytes((chunk, width), BF16)) + 4 * _nbytes((chunk, LANES), F32)
    return pl.pallas_call(
        functools.partial(_values_t_kernel, n_heads=hb),
        grid=(rows // chunk, n_heads // hb),
        in_specs=[pl.BlockSpec((chunk, width), lambda i, j: (row0 // chunk + i, col0 // hb + j))],
        out_specs=pl.BlockSpec((1, width, chunk), lambda i, j: (i, j, 0)),
        out_shape=jax.ShapeDtypeStruct((rows // chunk, n_heads * LANES, chunk), BF16),
        compiler_params=_params(("parallel", "parallel"), vmem),
        name=name,
    )(src)


def _kv_chunk(seq):
    return _pick(seq // 2, (512, 256, 128))


def _qk(q, k):
    return lax.dot_general(q, k, (((1,), (1,)), ((), ())), preferred_element_type=F32)


def _cat(refs, rows=None):
    parts = [r[...] if rows is None else r[rows, :] for r in refs]
    return parts[0] if len(parts) == 1 else jnp.concatenate(parts, axis=-1)


def _flash_kernel(*refs, mode, n_q, has_lat, tq, tqs, tkc, seq, window):
    refs = list(refs)
    q_refs = [refs.pop(0) for _ in range(n_q)]
    kc_refs = [refs.pop(0) for _ in range(n_q)]
    vc_ref = refs.pop(0)
    if has_lat:
        kl_refs = [refs.pop(0) for _ in range(n_q)]
        vl_ref = refs.pop(0)
    if mode == "window":
        sink_ref = refs.pop(0)
    if mode == "diff":
        lam_ref, subln_ref = refs.pop(0), refs.pop(0)
    o_ref = refs.pop(0)
    n_state = 2 if mode == "diff" else 1
    n_sub = tq // tqs
    chain_ids = [(sub, st) for sub in range(n_sub) for st in range(n_state)]
    per_chain = 7
    state_refs = {cid: tuple(refs[per_chain * i:][:3]) for i, cid in enumerate(chain_ids)}
    score_refs = {cid: tuple(refs[per_chain * i + 3:][:2]) for i, cid in enumerate(chain_ids)}
    smax_refs = {cid: tuple(refs[per_chain * i + 5:][:2]) for i, cid in enumerate(chain_ids)}
    qi = pl.program_id(2)

    def scores(k, only_sub=None):
        out = {}
        for sub, st in chain_ids:
            if only_sub is not None and sub != only_sub:
                continue
            q = _cat(q_refs, pl.ds(sub * tqs, tqs))
            if mode == "diff":
                lane = lax.broadcasted_iota(jnp.int32, q.shape, 1)
                keep = (lane < LANES // 2) if st == 0 else (lane >= LANES // 2)
                q = jnp.where(keep, q, jnp.zeros_like(q))
            out[sub, st] = _qk(k, q)
        return out

    def absorb(s_of, vt, *, first=False, mask=None):
        for cid, s in s_of.items():
            m_ref, l_ref, acc_ref = state_refs[cid]
            if isinstance(s, tuple):
                s, s_max = s
            else:
                if mask is not None:
                    s = jnp.where(mask, s, MASK_VALUE)
                s_max = jnp.max(s, axis=0, keepdims=True)
            if first and mode == "window":
                m_old = jnp.broadcast_to(sink_ref[0][:, :1], (1, tqs))
                m_new = jnp.maximum(m_old, s_max)
                l_old = jnp.exp2(m_old - m_new)
            elif first:
                m_new, l_old = s_max, None
            else:
                m_old = m_ref[...]
                m_new = jnp.maximum(m_old, s_max)
                alpha = jnp.exp2(m_old - m_new)
                l_old = alpha * l_ref[...]
            p = jnp.exp2(s - m_new)
            l_new = jnp.sum(p, axis=0, keepdims=True)
            pv = jnp.dot(vt, p.astype(vt.dtype), preferred_element_type=F32)
            m_ref[...] = m_new
            l_ref[...] = l_new if l_old is None else l_old + l_new
            acc_ref[...] = pv if first else alpha * acc_ref[...] + pv

    def lat_keys(c):
        return _cat(kl_refs, pl.ds(pl.multiple_of(c * tkc, tkc), tkc))

    def stash(slot, s_of):
        for cid, s in s_of.items():
            score_refs[cid][slot][...] = s
            smax_refs[cid][slot][...] = jnp.max(s, axis=0, keepdims=True)

    def fetch(slot):
        return {cid: (score_refs[cid][slot][...], smax_refs[cid][slot][...]) for cid in chain_ids}

    s_ctx = scores(_cat(kc_refs))
    if has_lat and mode == "window":
        tkw = tqs + 2 * window
        blocks = []
        for sub in range(n_sub):
            q_pos0 = qi * tq + sub * tqs
            start = pl.multiple_of(jnp.clip(q_pos0 - window, 0, seq - tkw), LANES)
            kp = start + lax.broadcasted_iota(jnp.int32, (tkw, tqs), 0)
            qp = q_pos0 + lax.broadcasted_iota(jnp.int32, (tkw, tqs), 1)
            blocks.append((scores(_cat(kl_refs, pl.ds(start, tkw)), only_sub=sub), start // LANES,
                           jnp.abs(qp - kp) <= window))
        absorb(s_ctx, vc_ref[0], first=True)
        for s_of, first_chunk, mask in blocks:
            vt = jnp.concatenate([vl_ref[first_chunk + j] for j in range(tkw // LANES)], axis=-1)
            absorb(s_of, vt, mask=mask)
    elif has_lat:
        n_chunks = seq // tkc
        stash(0, scores(lat_keys(0)))
        absorb(s_ctx, vc_ref[0], first=True)

        def body(i, carry):
            c = 2 * i
            stash(1, scores(lat_keys(c + 1)))
            absorb(fetch(0), vl_ref[c])
            stash(0, scores(lat_keys(c + 2)))
            absorb(fetch(1), vl_ref[c + 1])
            return carry

        lax.fori_loop(0, n_chunks // 2 - 1, body, 0)
        stash(1, scores(lat_keys(n_chunks - 1)))
        absorb(fetch(0), vl_ref[n_chunks - 2])
        absorb(fetch(1), vl_ref[n_chunks - 1])
    else:
        absorb(s_ctx, vc_ref[0], first=True)

    for sub in range(n_sub):
        rows = pl.ds(sub * tqs, tqs)
        outs = [(state_refs[sub, st][2][...] / state_refs[sub, st][1][...]).T for st in range(n_state)]
        if mode == "diff":
            lam_rows = lam_ref[...]
            lam = (jnp.exp(jnp.sum(lam_rows[0:1] * lam_rows[1:2], axis=-1, keepdims=True))
                   - jnp.exp(jnp.sum(lam_rows[2:3] * lam_rows[3:4], axis=-1, keepdims=True)) + D_LAMBDA_INIT)
            out = _norm_rows(outs[0] - lam * outs[1], subln_ref[...]) * (1.0 - D_LAMBDA_INIT)
        else:
            out = outs[0]
        o_ref[rows, :] = out.astype(o_ref.dtype)


def _flash(q_parts, k_parts, v, *, n_heads, group, batch, seq, ctx_len, lat_rows, ctx_queries=False,
           mode="softmax", sink=None, lam=None, subln=None, window=0, name="flash"):
    has_lat = not ctx_queries
    ctx_blk0 = lat_rows // ctx_len
    if ctx_queries:
        tq, nq = ctx_len, 1
        q_row = lambda b, h, i: ctx_blk0 + b
        out_rows = batch * ctx_len
        o_row = lambda b, h, i: b
    else:
        tq = _pick(seq, (1024, 512, 256, 128))
        nq = seq // tq
        q_row = lambda b, h, i: b * nq + i
        out_rows = lat_rows
        o_row = q_row
    tqs = min(tq, 256)
    tkc = _kv_chunk(seq)
    n_state = 2 if mode == "diff" else 1
    if mode == "window":
        assert seq >= tqs + 2 * window and window % 16 == 0
    n_q = len(q_parts)
    in_specs, args = [], []
    vmem = 0

    def add(arr, block, imap):
        nonlocal vmem
        in_specs.append(pl.BlockSpec(block, imap))
        args.append(arr)
        vmem += 2 * _nbytes(block, arr.dtype)

    for arr, c0 in q_parts:
        add(arr, (tq, LANES), lambda b, h, i, c0=c0: (q_row(b, h, i), c0 + h))
    for arr, c0, per_head in k_parts:
        add(arr, (ctx_len, LANES), lambda b, h, i, c0=c0, ph=per_head: (ctx_blk0 + b, c0 + (h // group) * ph))
    vt_lat, vt_ctx = v
    add(vt_ctx, (1, LANES, ctx_len), lambda b, h, i: (b, h // group, 0))
    if has_lat:
        for arr, c0, per_head in k_parts:
            add(arr, (seq, LANES), lambda b, h, i, c0=c0, ph=per_head: (b, c0 + (h // group) * ph))
        tkv = vt_lat.shape[-1]
        assert tkv == (LANES if mode == "window" else tkc)
        add(vt_lat, (seq // tkv, LANES, tkv), lambda b, h, i: (b, h // group, 0))
    if mode == "window":
        add(sink, (1, 1, LANES), lambda b, h, i: (h, 0, 0))
    if mode == "diff":
        add(lam, (4, LANES), lambda b, h, i: (0, 0))
        add(subln, (1, LANES), lambda b, h, i: (0, 0))
    n_sub = tq // tqs
    scratch = [pltpu.VMEM((1, tqs), F32), pltpu.VMEM((1, tqs), F32), pltpu.VMEM((LANES, tqs), F32),
               pltpu.VMEM((tkc, tqs), F32), pltpu.VMEM((tkc, tqs), F32),
               pltpu.VMEM((1, tqs), F32), pltpu.VMEM((1, tqs), F32)] * (n_state * n_sub)
    vmem += 2 * n_state * n_sub * _nbytes((tkc, tqs), F32)
    vmem += 2 * _nbytes((tq, LANES), BF16) + 3 * _nbytes((n_state, tq, LANES), F32)
    vmem += 4 * n_state * (tq // tqs) * _nbytes((tqs, max(tkc, ctx_len, tqs + 2 * window)), F32)
    kern = functools.partial(_flash_kernel, mode=mode, n_q=n_q, has_lat=has_lat, tq=tq, tqs=tqs, tkc=tkc,
                             seq=seq, window=window)
    return pl.pallas_call(
        kern,
        grid=(batch, n_heads, nq),
        in_specs=in_specs,
        out_specs=pl.BlockSpec((tq, LANES), lambda b, h, i: (o_row(b, h, i), h)),
        out_shape=jax.ShapeDtypeStruct((out_rows, n_heads * LANES), BF16),
        scratch_shapes=scratch,
        compiler_params=_params(("parallel", "parallel", "parallel"), vmem),
        name=name,
    )(*args)


def _route_kernel(lg_ref, info_ref, cnt_ref, base_ref, *, n_experts, tb):
    step = pl.program_id(0)

    @pl.when(step == 0)
    def _():
        base_ref[...] = jnp.zeros_like(base_ref)

    lane = lax.broadcasted_iota(jnp.int32, (tb, LANES), 1).astype(F32)
    logits = jnp.where(lane < n_experts, lg_ref[...], -jnp.inf)
    v1 = jnp.max(logits, axis=-1, keepdims=True)
    i1 = jnp.min(jnp.where(logits == v1, lane, float(LANES)), axis=-1, keepdims=True)
    hot1 = lane == i1
    rest = jnp.where(hot1, -jnp.inf, logits)
    v2 = jnp.max(rest, axis=-1, keepdims=True)
    i2 = jnp.min(jnp.where(rest == v2, lane, float(LANES)), axis=-1, keepdims=True)
    hot2 = lane == i2
    e = jnp.exp(v2 - v1)
    w1 = 1.0 / (1.0 + e)
    w2 = e / (1.0 + e)
    sel = jnp.where(hot1 | hot2, 1.0, 0.0)
    r = lax.broadcasted_iota(jnp.int32, (tb, tb), 0)
    c = lax.broadcasted_iota(jnp.int32, (tb, tb), 1)
    tri = jnp.where(c < r, 1.0, 0.0).astype(BF16)
    before = jnp.dot(tri, sel.astype(BF16), preferred_element_type=F32) + base_ref[...]
    rank1 = jnp.sum(jnp.where(hot1, before, 0.0), axis=-1, keepdims=True)
    rank2 = jnp.sum(jnp.where(hot2, before, 0.0), axis=-1, keepdims=True)
    base_ref[...] += jnp.sum(sel, axis=0, keepdims=True)
    cnt_ref[...] = jnp.broadcast_to(base_ref[...], cnt_ref.shape)
    info = jnp.zeros((tb, LANES), F32)
    for slot, val in enumerate((i1, i2, rank1, rank2, w1, w2)):
        info = jnp.where(lane == float(slot), val, info)
    info_ref[...] = info


def _route(logits, n_experts):
    n = logits.shape[0]
    tb = _pick(n, (256, 128, 64, 32, 16, 8))
    vmem = 4 * _nbytes((tb, LANES), F32) + 16 * _nbytes((tb, max(tb, LANES)), F32)
    return pl.pallas_call(
        functools.partial(_route_kernel, n_experts=n_experts, tb=tb),
        grid=(n // tb,),
        in_specs=[pl.BlockSpec((tb, LANES), lambda i: (i, 0))],
        out_specs=[pl.BlockSpec((tb, LANES), lambda i: (i, 0)),
                   pl.BlockSpec((8, LANES), lambda i: (0, 0))],
        out_shape=[jax.ShapeDtypeStruct((n, LANES), F32), jax.ShapeDtypeStruct((8, LANES), F32)],
        scratch_shapes=[pltpu.VMEM((1, LANES), F32)],
        compiler_params=_params(("arbitrary",), vmem),
        name="moe_route",
    )(logits)


def _row_copy(src_hbm, row, dst_vmem, slot, sem):
    return pltpu.make_async_copy(src_hbm.at[pl.ds(row, 1)], dst_vmem.at[pl.ds(slot, 1)], sem)


def _gather_kernel(src_ref, h_hbm, o_ref, buf, sem, *, tm):
    base = pl.program_id(0) * tm

    def issue(r, carry):
        _row_copy(h_hbm, src_ref[base + r], buf, r, sem).start()
        return carry

    lax.fori_loop(0, tm, issue, 0, unroll=DMA_LOOP_UNROLL)

    def drain(r, carry):
        _row_copy(h_hbm, 0, buf, r, sem).wait()
        return carry

    lax.fori_loop(0, tm, drain, 0, unroll=DMA_LOOP_UNROLL)
    o_ref[...] = buf[...].astype(o_ref.dtype)


def _gather_rows(src_rows, h, *, tm):
    p = src_rows.shape[0]
    d = h.shape[1]
    vmem = _nbytes((tm, d), F32) + 2 * _nbytes((tm, d), BF16)
    gs = pltpu.PrefetchScalarGridSpec(
        num_scalar_prefetch=1, grid=(p // tm,),
        in_specs=[pl.BlockSpec(memory_space=pl.ANY)],
        out_specs=pl.BlockSpec((tm, d), lambda i, src: (i, 0)),
        scratch_shapes=[pltpu.VMEM((tm, d), F32), pltpu.SemaphoreType.DMA(())])
    return pl.pallas_call(
        functools.partial(_gather_kernel, tm=tm), grid_spec=gs,
        out_shape=jax.ShapeDtypeStruct((p, d), BF16),
        compiler_params=_params(("arbitrary",), vmem),
        name="moe_gather",
    )(src_rows, h)


def _combine_kernel(d1_ref, d2_ref, y_hbm, x_ref, info_ref, gate_ref, g_ref, o_ref, buf1, buf2, sem, *, tb):
    base = pl.program_id(0) * tb

    def issue(r, carry):
        _row_copy(y_hbm, d1_ref[base + r], buf1, r, sem.at[0]).start()
        _row_copy(y_hbm, d2_ref[base + r], buf2, r, sem.at[1]).start()
        return carry

    lax.fori_loop(0, tb, issue, 0, unroll=DMA_LOOP_UNROLL)

    def drain(r, carry):
        _row_copy(y_hbm, 0, buf1, r, sem.at[0]).wait()
        _row_copy(y_hbm, 0, buf2, r, sem.at[1]).wait()
        return carry

    lax.fori_loop(0, tb, drain, 0, unroll=DMA_LOOP_UNROLL)
    info = info_ref[...]
    w1, w2 = info[:, 4:5], info[:, 5:6]
    x = x_ref[...] + gate_ref[0] * (w1 * buf1[...] + w2 * buf2[...])
    o_ref[...] = _norm_rows(x, g_ref[...])


def _combine(dest1, dest2, y, x, info, gate, final_norm, *, seq, batch):
    n, d = x.shape
    tb = _pick(seq, (256, 128, 64, 32, 16, 8))
    midx = _mod_index(tb, seq, batch)
    vmem = 2 * _nbytes((tb, d), F32) * 3 + 2 * _nbytes((tb, d), F32) + 2 * _nbytes((tb, LANES), F32)
    gs = pltpu.PrefetchScalarGridSpec(
        num_scalar_prefetch=2, grid=(n // tb,),
        in_specs=[pl.BlockSpec(memory_space=pl.ANY),
                  pl.BlockSpec((tb, d), lambda i, a, b: (i, 0)),
                  pl.BlockSpec((tb, LANES), lambda i, a, b: (i, 0)),
                  pl.BlockSpec((1, 1, d), lambda i, a, b: (midx(i), 0, 0)),
                  pl.BlockSpec((1, d), lambda i, a, b: (0, 0))],
        out_specs=pl.BlockSpec((tb, d), lambda i, a, b: (i, 0)),
        scratch_shapes=[pltpu.VMEM((tb, d), F32), pltpu.VMEM((tb, d), F32), pltpu.SemaphoreType.DMA((2,))])
    return pl.pallas_call(
        functools.partial(_combine_kernel, tb=tb), grid_spec=gs,
        out_shape=jax.ShapeDtypeStruct((n, d), F32),
        compiler_params=_params(("arbitrary",), vmem),
        name="moe_combine_norm",
    )(dest1, dest2, y, x, info, gate, final_norm.reshape(1, d))


def _expert_mm_kernel(te_ref, used_ref, x_ref, *refs, n_w, swiglu):
    w_refs, o_ref, wbf_refs = refs[:n_w], refs[n_w], refs[n_w + 1:]
    i = pl.program_id(1)
    live = i < used_ref[0]
    fresh = jnp.logical_or(i == 0, te_ref[i] != te_ref[jnp.maximum(i - 1, 0)])

    @pl.when(jnp.logical_and(live, fresh))
    def _():
        for w, wb in zip(w_refs, wbf_refs):
            wb[...] = w[0].astype(wb.dtype)

    @pl.when(live)
    def _():
        x = x_ref[...]
        parts = [jnp.dot(x, wb[...], preferred_element_type=F32) for wb in wbf_refs]
        out = parts[0] * jax.nn.sigmoid(parts[0]) * parts[1] if swiglu else parts[0]
        o_ref[...] = out.astype(o_ref.dtype)

    @pl.when(jnp.logical_not(live))
    def _():
        o_ref[...] = jnp.zeros_like(o_ref)


def _expert_matmul(x, ws, tile_expert, tiles_used, *, out_dtype, tm, tn, swiglu=False, name):
    rows, kdim = x.shape
    n = ws[0].shape[-1]
    n_w = len(ws)
    assert rows % tm == 0 and n % tn == 0
    vmem = 2 * (_nbytes((tm, kdim), x.dtype) + n_w * _nbytes((kdim, tn), ws[0].dtype) + _nbytes((tm, tn), out_dtype))
    vmem += n_w * (_nbytes((kdim, tn), BF16) + 2 * _nbytes((tm, tn), F32))
    gs = pltpu.PrefetchScalarGridSpec(
        num_scalar_prefetch=2, grid=(n // tn, rows // tm),
        in_specs=[pl.BlockSpec((tm, kdim), lambda j, i, te, nu: (i, 0))]
        + [pl.BlockSpec((1, kdim, tn), lambda j, i, te, nu: (te[i], 0, j))] * n_w,
        out_specs=pl.BlockSpec((tm, tn), lambda j, i, te, nu: (i, j)),
        scratch_shapes=[pltpu.VMEM((kdim, tn), BF16)] * n_w)
    return pl.pallas_call(
        functools.partial(_expert_mm_kernel, n_w=n_w, swiglu=swiglu), grid_spec=gs,
        out_shape=jax.ShapeDtypeStruct((rows, n), out_dtype),
        compiler_params=_params(("arbitrary", "arbitrary"), vmem),
        name=name,
    )(tile_expert, tiles_used, x, *ws)


def _moe(h_f32, logits, x, gate, final_norm, w_gate, w_up, w_down, *, dims):
    n, d = x.shape
    n_exp = dims.n_experts
    tm = _pick(n, (512, 256, 128, 64, 32, 16))
    p = 2 * n + n_exp * tm
    info, counts = _route(logits, n_exp)
    cnt = counts[0, :n_exp].astype(jnp.int32)
    padded = ((cnt + tm - 1) // tm) * tm
    ends = jnp.cumsum(padded)
    starts = ends - padded
    e1, e2 = info[:, 0].astype(jnp.int32), info[:, 1].astype(jnp.int32)
    dest1 = starts[e1] + info[:, 2].astype(jnp.int32)
    dest2 = starts[e2] + info[:, 3].astype(jnp.int32)
    token = jnp.arange(n, dtype=jnp.int32)
    src = jnp.zeros((p,), jnp.int32).at[dest1].set(token).at[dest2].set(token)
    tile_start = jnp.arange(p // tm, dtype=jnp.int32) * tm
    tile_expert = jnp.minimum(jnp.sum(tile_start[:, None] >= ends[None, :], axis=1), n_exp - 1).astype(jnp.int32)
    tiles_used = (ends[-1:] // tm).astype(jnp.int32)

    xs = _gather_rows(src, h_f32, tm=_pick(tm, (256, 128, 64, 32, 16)))
    f = w_gate.shape[-1]
    act = _expert_matmul(xs, [w_gate, w_up], tile_expert, tiles_used, out_dtype=BF16, tm=tm,
                         tn=_pick(f, (512, 256, 128)), swiglu=True, name="moe_gate_up")
    y = _expert_matmul(act, [w_down], tile_expert, tiles_used, out_dtype=F32, tm=tm,
                       tn=_pick(d, (1024, 512, 256, 128)), name="moe_down")
    return _combine(dest1, dest2, y, x, info, gate, final_norm, seq=dims.seq, batch=dims.batch)


def _pad_cols(w, n):
    return w if w.shape[-1] == n else jnp.pad(w, [(0, 0)] * (w.ndim - 1) + [(0, n - w.shape[-1])])


def _forward(dims, x, c, ctx, c_ctx,
             ada_w0, ada_b0, norm_mix0, norm_ffn0, w_in0, a_q_norm, a_k_norm, b_q_norm, b_kv_norm,
             w_uq, w_ukv, w_o0, ffn_w_gate, ffn_w_up, ffn_w_down,
             ada_w1, ada_b1, norm_mix1, norm_ffn1, w_in1, c_sink, d_lam_q1, d_lam_k1, d_lam_q2, d_lam_k2,
             d_subln, w_o1, router_w, moe_w_gate, moe_w_up, moe_w_down, final_norm):
    dm, bsz, seq, ctx_len = dims.d_model, dims.batch, dims.seq, dims.ctx_len
    nl, nc = bsz * seq, bsz * ctx_len
    nt = nl + nc
    ha, hka, hb = dims.a_heads, dims.a_kv_heads, dims.b_heads
    hc, hkc, hd = dims.c_heads, dims.c_kv_heads, dims.d_heads
    big = lambda n: _pick(n, (1024, 512, 256, 128, 64, 32, 16))
    tm_all, tm_lat = big(math.gcd(nt, seq)), big(math.gcd(nl, seq))

    xa = jnp.concatenate([x.reshape(nl, dm), ctx.reshape(nc, dm)], axis=0)
    cond = jnp.concatenate([c, c_ctx[None, :], jnp.zeros((8 - bsz - 1, dm), F32)], axis=0)

    def mods(ada_w, ada_b):
        m = _modulation(cond, ada_w, ada_b).reshape(8, N_MOD, 1, dm)
        return [m[:, i] for i in range(N_MOD)]

    rope128 = _rope_tables(seq, dims.grid_w, 128, tm_all)
    rope64 = _rope_tables(seq, dims.grid_w, 64, tm_all)
    flash = functools.partial(_flash, batch=bsz, seq=seq, ctx_len=ctx_len, lat_rows=nl)

    def project(src, w, *, rows=nt, out_dtype=BF16, name, **head):
        n = w.shape[1]
        tm = tm_all if rows == nt else tm_lat
        if head:
            return _matmul(src, [w.astype(BF16)], out_dtype=out_dtype, tm=tm, tn=_pick(n, (1024, 512, 256, 128)),
                           rows=rows, epi="heads", head=dict(head, lat_rows=nl), seq=seq, name=name)
        return _matmul(src, [w.astype(BF16)], out_dtype=out_dtype, tm=tm, tn=_pick(n, (512, 256, 128)),
                       rows=rows, name=name)

    def values_t(src, col0, n_heads, name, chunk=_kv_chunk(seq)):
        return (_values_t(src, col0, n_heads, chunk=chunk, row0=0, rows=nl, name=name),
                _values_t(src, col0, n_heads, chunk=ctx_len, row0=nl, rows=nc, name=name + "_ctx"))

    sh1, sc1, g1, sh2, sc2, g2 = mods(ada_w0, ada_b0)
    h = _norm_mod(xa, norm_mix0, sh1, sc1, rows=nt, seq=seq, batch=bsz)
    na_q, na_kv, rq, rkv = ha * HEAD_DIM, hka * HEAD_DIM, dims.b_q_rank, dims.b_kv_rank
    c_ak, c_av, c_cq, c_ckv, c_kr = na_q, na_q + na_kv, na_q + 2 * na_kv, na_q + 2 * na_kv + rq, na_q + 2 * na_kv + rq + rkv
    aq = project(h, w_in0[:, :c_ak], gain=a_q_norm, rope=rope128, rot_dim=128, scale=LOG2E * HEAD_DIM ** -0.5,
                 name="proj_aq")
    ak = project(h, w_in0[:, c_ak:c_av], gain=a_k_norm, rope=rope128, rot_dim=128, name="proj_ak")
    bk_rope = project(h, _pad_cols(w_in0[:, c_kr:], LANES), rope=rope64, rot_dim=64, name="proj_bk_rope")
    rest = project(h, jnp.concatenate([w_in0[:, c_cq:c_kr], w_in0[:, c_av:c_cq]], axis=1), out_dtype=F32,
                   name="proj_ranks_av")
    av = values_t(rest, (rq + rkv) // HEAD_DIM, hka, "vt_a")
    cq = _rank_norm(rest, 0, rq, b_q_norm)
    ckv = _rank_norm(rest, rq, rkv, b_kv_norm)
    b_scale = LOG2E * (HEAD_DIM + 64) ** -0.5
    uq = w_uq.reshape(rq, hb, HEAD_DIM + 64) * b_scale
    bq_nope = project(cq, uq[:, :, :HEAD_DIM].reshape(rq, hb * HEAD_DIM), name="b_up_q_nope")
    bq_rope = project(cq, _pad_cols(uq[:, :, HEAD_DIM:], LANES).reshape(rq, hb * LANES), rope=rope64, rot_dim=64,
                      name="b_up_q_rope")
    ukv = w_ukv.reshape(rkv, hb, 2 * HEAD_DIM)
    ukv = jnp.concatenate([ukv[:, :, :HEAD_DIM].reshape(rkv, hb * HEAD_DIM),
                           ukv[:, :, HEAD_DIM:].reshape(rkv, hb * HEAD_DIM)], axis=1)
    bkv = project(ckv, ukv, name="b_up_kv")

    a_lat = flash([(aq, 0)], [(ak, 0, 1)], av, n_heads=ha, group=ha // hka, name="attn_a")
    a_ctx = flash([(aq, 0)], [(ak, 0, 1)], av, n_heads=ha, group=ha // hka, ctx_queries=True, name="attn_a_ctx")
    b_args = ([(bq_nope, 0), (bq_rope, 0)], [(bkv, 0, 1), (bk_rope, 0, 0)], values_t(bkv, hb, hb, "vt_b"))
    b_lat = flash(*b_args, n_heads=hb, group=1, name="attn_b")
    b_ctx = flash(*b_args, n_heads=hb, group=1, ctx_queries=True, name="attn_b_ctx")
    xa = _matmul([(a_lat, a_ctx), (b_lat, b_ctx)], [w_o0], out_dtype=F32, tm=tm_all, tn=_pick(dm, (512, 256, 128)),
                 epi="gres", res=xa, gate=g1, seq=seq, batch=bsz, name="out_proj0")

    h = _norm_mod(xa, norm_ffn0, sh2, sc2, rows=nt, seq=seq, batch=bsz)
    ffn = dims.ffn_dim
    act = _matmul(h, [ffn_w_gate, ffn_w_up], out_dtype=BF16, tm=tm_all, tn=_pick(ffn, (256, 128)), epi="swiglu",
                  name="ffn_gate_up")
    tk_down = ffn // 2 if (ffn // 2) % LANES == 0 and ffn > 2048 else ffn
    xa = _matmul(act, [ffn_w_down.astype(BF16)], out_dtype=F32, tm=tm_all, tn=_pick(dm, (512, 256, 128)), tk=tk_down,
                 epi="gres", res=xa, gate=g2, seq=seq, batch=bsz, name="ffn_down")

    sh1, sc1, g1, sh2, sc2, g2 = mods(ada_w1, ada_b1)
    h = _norm_mod(xa, norm_mix1, sh1, sc1, rows=nt, seq=seq, batch=bsz)
    bounds = [0]
    for n_heads in (hc, hkc, hkc, hd, hd, hd):
        bounds.append(bounds[-1] + n_heads * HEAD_DIM)
    w_cq, w_ck, w_cv, w_dq, w_dk, w_dv = (w_in1[:, lo:hi] for lo, hi in zip(bounds[:-1], bounds[1:]))
    cq_ = project(h, w_cq, rows=nl, rope=rope128, rot_dim=128, scale=LOG2E * HEAD_DIM ** -0.5, name="proj_cq")
    ck_ = project(h, w_ck, rope=rope128, rot_dim=128, name="proj_ck")
    dq_ = project(h, w_dq, rows=nl, rope=rope64, rot_dim=64, scale=LOG2E * 64 ** -0.5, name="proj_dq")
    dk_ = project(h, w_dk, rope=rope64, rot_dim=64, name="proj_dk")
    values = project(h, jnp.concatenate([w_cv, w_dv], axis=1), name="proj_cv_dv")
    cv_ = values_t(values, 0, hkc, "vt_c", chunk=LANES)
    dv_ = values_t(values, hkc, hd, "vt_d")
    sink = jnp.broadcast_to(LOG2E * c_sink.astype(F32)[:, None, None], (hc, 1, LANES))
    lam = jnp.stack([_pad_cols(v.astype(F32)[None, :], LANES)[0] for v in (d_lam_q1, d_lam_k1, d_lam_q2, d_lam_k2)])
    c_out = flash([(cq_, 0)], [(ck_, 0, 1)], cv_, n_heads=hc, group=hc // hkc, mode="window", sink=sink,
                  window=dims.window, name="attn_c")
    d_out = flash([(dq_, 0)], [(dk_, 0, 1)], dv_, n_heads=hd, group=1, mode="diff", lam=lam,
                  subln=d_subln.reshape(1, LANES), name="attn_d")
    xl = _matmul([c_out, d_out], [w_o1], out_dtype=F32, tm=tm_lat, tn=_pick(dm, (512, 256, 128)),
                 epi="gres", res=xa, gate=g1, seq=seq, batch=bsz, rows=nl, name="out_proj1")

    rw = _pad_cols(router_w, LANES)
    rw_hi = rw.astype(BF16)
    rw_lo = (rw - rw_hi.astype(F32)).astype(BF16)
    hf32, logits = _norm_mod(xl, norm_ffn1, sh2, sc2, rows=nl, seq=seq, batch=bsz, router=(rw_hi, rw_lo))
    out = _moe(hf32, logits, xl, g2, final_norm, moe_w_gate, moe_w_up, moe_w_down, dims=dims)
    return out.reshape(bsz, seq, dm)


def _rank_norm_kernel(x_ref, g_ref, o_ref):
    o_ref[...] = _norm_rows(x_ref[...], g_ref[...]).astype(o_ref.dtype)


def _rank_norm(src, col0, width, gain):
    rows = src.shape[0]
    tm = _pick(rows, (256, 128, 64, 32, 16))
    assert col0 % width == 0
    vmem = 2 * (_nbytes((tm, width), F32) + _nbytes((tm, width), BF16))
    return pl.pallas_call(
        _rank_norm_kernel,
        grid=(rows // tm,),
        in_specs=[pl.BlockSpec((tm, width), lambda i: (i, col0 // width)),
                  pl.BlockSpec((1, width), lambda i: (0, 0))],
        out_specs=pl.BlockSpec((tm, width), lambda i: (i, 0)),
        out_shape=jax.ShapeDtypeStruct((rows, width), BF16),
        compiler_params=_params(("parallel",), vmem),
        name="rank_norm",
    )(src, gain.reshape(1, width))


_DIMS = Dims(d_model=4096, batch=4, seq=4096, ctx_len=256, grid_w=64, a_heads=16, a_kv_heads=4, b_heads=16,
             b_q_rank=1536, b_kv_rank=512, c_heads=16, c_kv_heads=4, window=128, d_heads=16, ffn_dim=11008,
             n_experts=8, expert_dim=3584)


def kernel(x, c, ctx, c_ctx, ada_w0, ada_b0, norm_mix0, norm_ffn0, w_in0, a_q_norm, a_k_norm, b_q_norm, b_kv_norm, w_uq, w_ukv, w_o0, ffn_w_gate, ffn_w_up, ffn_w_down, ada_w1, ada_b1, norm_mix1, norm_ffn1, w_in1, c_sink, d_lam_q1, d_lam_k1, d_lam_q2, d_lam_k2, d_subln, w_o1, router_w, moe_w_gate, moe_w_up, moe_w_down, final_norm):
    return _forward(_DIMS, x, c, ctx, c_ctx, ada_w0, ada_b0, norm_mix0, norm_ffn0, w_in0, a_q_norm, a_k_norm,
                    b_q_norm, b_kv_norm, w_uq, w_ukv, w_o0, ffn_w_gate, ffn_w_up, ffn_w_down, ada_w1, ada_b1,
                    norm_mix1, norm_ffn1, w_in1, c_sink, d_lam_q1, d_lam_k1, d_lam_q2, d_lam_k2, d_subln, w_o1,
                    router_w, moe_w_gate, moe_w_up, moe_w_down, final_norm)
```

```python
import functools
import math
from typing import NamedTuple

import jax
import jax.numpy as jnp
from jax import lax
from jax.experimental import pallas as pl
from jax.experimental.pallas import tpu as pltpu

F32 = jnp.float32
BF16 = jnp.bfloat16

LANES = 128
HEAD_DIM = 128
ROPE_THETA = 10000.0
NORM_EPS = 1e-6
MASK_VALUE = -1e30
LOG2E = math.log2(math.e)
N_MOD = 6
D_LAYER_INDEX = 1
D_LAMBDA_INIT = 0.8 - 0.6 * math.exp(-0.3 * D_LAYER_INDEX)
VMEM_HEADROOM = 6 << 20
DMA_LOOP_UNROLL = 8
HEAD_SUBBLOCK = 256


class Dims(NamedTuple):
    d_model: int
    batch: int
    seq: int
    ctx_len: int
    grid_w: int
    a_heads: int
    a_kv_heads: int
    b_heads: int
    b_q_rank: int
    b_kv_rank: int
    c_heads: int
    c_kv_heads: int
    window: int
    d_heads: int
    ffn_dim: int
    n_experts: int
    expert_dim: int


def _pick(n, prefs):
    for p in prefs:
        if n % p == 0:
            return p
    raise ValueError(f"no tile in {prefs} divides {n}")


def _params(sem, vmem_bytes):
    return pltpu.CompilerParams(dimension_semantics=sem, vmem_limit_bytes=int(vmem_bytes) + VMEM_HEADROOM)


def _nbytes(shape, dtype):
    return math.prod(shape) * jnp.dtype(dtype).itemsize


def _mod_kernel(c_ref, w_ref, b_ref, o_ref):
    c = c_ref[...]
    s = (c * jax.nn.sigmoid(c)).astype(BF16)
    o_ref[...] = jnp.dot(s, w_ref[...].astype(BF16), preferred_element_type=F32) + b_ref[...]


def _modulation(cond, ada_w, ada_b):
    rows, d = cond.shape
    n = ada_w.shape[1]
    tn = _pick(n, (512, 256, 128))
    vmem = 2 * (_nbytes((d, tn), F32) + _nbytes((rows, tn), F32) * 2) + _nbytes((rows, d), F32) * 2
    return pl.pallas_call(
        _mod_kernel,
        grid=(n // tn,),
        in_specs=[pl.BlockSpec((rows, d), lambda j: (0, 0)),
                  pl.BlockSpec((d, tn), lambda j: (0, j)),
                  pl.BlockSpec((1, tn), lambda j: (0, j))],
        out_specs=pl.BlockSpec((rows, tn), lambda j: (0, j)),
        out_shape=jax.ShapeDtypeStruct((rows, n), F32),
        compiler_params=_params(("parallel",), vmem),
        name="modulation",
    )(cond, ada_w, ada_b.reshape(1, n))


def _norm_rows(x, g):
    var = jnp.mean(x * x, axis=-1, keepdims=True)
    return x * lax.rsqrt(var + NORM_EPS) * g


def _norm_mod_kernel(x_ref, g_ref, sh_ref, sc_ref, o_ref):
    y = _norm_rows(x_ref[...], g_ref[...])
    o_ref[...] = (y * (1.0 + sc_ref[0]) + sh_ref[0]).astype(o_ref.dtype)


def _norm_mod_router_kernel(x_ref, g_ref, sh_ref, sc_ref, rhi_ref, rlo_ref, of_ref, lg_ref):
    y = _norm_rows(x_ref[...], g_ref[...])
    h = y * (1.0 + sc_ref[0]) + sh_ref[0]
    hi = h.astype(BF16)
    lo = (h - hi.astype(F32)).astype(BF16)
    half = h.shape[1] // 2
    left = lax.bitcast_convert_type(hi[:, :half].astype(F32), jnp.uint32)
    right = lax.bitcast_convert_type(hi[:, half:].astype(F32), jnp.uint32)
    of_ref[...] = lax.shift_right_logical(left, jnp.uint32(16)) | right
    lg_ref[...] = (jnp.dot(hi, rhi_ref[...], preferred_element_type=F32)
                   + jnp.dot(hi, rlo_ref[...], preferred_element_type=F32)
                   + jnp.dot(lo, rhi_ref[...], preferred_element_type=F32))


def _mod_index(tm, seq, batch):
    return lambda i: jnp.minimum((i * tm) // seq, batch)


def _norm_mod(x, g, shift, scale, *, rows, seq, batch, router=None):
    m, d = x.shape
    tm = _pick(math.gcd(rows, seq), (256, 128, 64, 32, 16))
    midx = _mod_index(tm, seq, batch)
    in_specs = [pl.BlockSpec((tm, d), lambda i: (i, 0)),
                pl.BlockSpec((1, d), lambda i: (0, 0)),
                pl.BlockSpec((1, 1, d), lambda i: (midx(i), 0, 0)),
                pl.BlockSpec((1, 1, d), lambda i: (midx(i), 0, 0))]
    vmem = 2 * (_nbytes((tm, d), F32) + _nbytes((tm, d), BF16)) + 6 * _nbytes((1, d), F32)
    if router is None:
        return pl.pallas_call(
            _norm_mod_kernel,
            grid=(rows // tm,),
            in_specs=in_specs,
            out_specs=pl.BlockSpec((tm, d), lambda i: (i, 0)),
            out_shape=jax.ShapeDtypeStruct((rows, d), BF16),
            compiler_params=_params(("parallel",), vmem),
            name="norm_modulate",
        )(x, g.reshape(1, d), shift, scale)
    rhi, rlo = router
    vmem += 2 * (_nbytes((tm, d), F32) + _nbytes((tm, LANES), F32)) + 4 * _nbytes((d, LANES), BF16)
    return pl.pallas_call(
        _norm_mod_router_kernel,
        grid=(rows // tm,),
        in_specs=in_specs + [pl.BlockSpec((d, LANES), lambda i: (0, 0)),
                             pl.BlockSpec((d, LANES), lambda i: (0, 0))],
        out_specs=[pl.BlockSpec((tm, d // 2), lambda i: (i, 0)),
                   pl.BlockSpec((tm, LANES), lambda i: (i, 0))],
        out_shape=[jax.ShapeDtypeStruct((rows, d // 2), jnp.uint32),
                   jax.ShapeDtypeStruct((rows, LANES), F32)],
        compiler_params=_params(("parallel",), vmem),
        name="norm_modulate_router",
    )(x, g.reshape(1, d), shift, scale, rhi, rlo)


def _head_op(x, gain, rope, rot_dim, scale):
    if gain is not None:
        x = _norm_rows(x, gain)
    if rot_dim:
        cos, up, down = rope
        half = rot_dim // 2
        x = x * cos + pltpu.roll(x, LANES - half, 1) * up + pltpu.roll(x, half, 1) * down
    return x if scale == 1.0 else x * scale


def _mm_kernel(*refs, n_w, nk, epi, x_parts=(False,), lat_tiles=0, head_norm=False, rot_dim=0, scale=1.0):
    refs = list(refs)
    x_refs = []
    for has_ctx in x_parts:
        x_refs.append((refs.pop(0), refs.pop(0) if has_ctx else None))
    x_dtype = x_refs[0][0].dtype
    w_refs = refs[:n_w]

    def load_x():
        on_lat = pl.program_id(0) < lat_tiles
        parts = [lat[...] if ctx is None else jnp.where(on_lat, lat[...], ctx[...]) for lat, ctx in x_refs]
        return parts[0] if len(parts) == 1 else jnp.concatenate(parts, axis=-1)

    def weight(w_ref, cols=slice(None)):
        w = w_ref[:, cols]
        return w if w.dtype == x_dtype else w.astype(x_dtype)

    pos = n_w
    if epi == "gres":
        res_ref, gate_ref = refs[pos], refs[pos + 1]
        pos += 2
    if epi == "heads":
        gain_ref = refs[pos] if head_norm else None
        pos += int(head_norm)
        rope_refs = refs[pos:pos + 3] if rot_dim else None
        pos += 3 if rot_dim else 0
    o_ref = refs[pos]
    acc_refs = refs[pos + 1:]
    k = pl.program_id(2)

    def epilogue(parts):
        if epi == "swiglu":
            g, u = parts
            out = g * jax.nn.sigmoid(g) * u
        elif epi == "gres":
            out = res_ref[...] + gate_ref[0] * parts[0]
        else:
            out = parts[0]
        o_ref[...] = out.astype(o_ref.dtype)

    def compute_heads():
        x = load_x()
        gain = gain_ref[...] if head_norm else None
        rope = tuple(r[...] for r in rope_refs) if rot_dim else None
        tn = o_ref.shape[1]
        sub = HEAD_SUBBLOCK if tn % HEAD_SUBBLOCK == 0 else tn
        starts = list(range(0, tn, sub))
        dot = lambda c: jnp.dot(x, weight(w_refs[0], slice(c, c + sub)), preferred_element_type=F32)
        part = dot(starts[0])
        for idx, c in enumerate(starts):
            nxt = dot(starts[idx + 1]) if idx + 1 < len(starts) else None
            for hh in range(sub // LANES):
                cols = slice(hh * LANES, (hh + 1) * LANES)
                o_ref[:, c + hh * LANES:c + (hh + 1) * LANES] = _head_op(
                    part[:, cols], gain, rope, rot_dim, scale).astype(o_ref.dtype)
            part = nxt

    def compute():
        if epi == "heads":
            compute_heads()
            return
        x = load_x()
        parts = [jnp.dot(x, weight(w), preferred_element_type=F32) for w in w_refs]
        if nk == 1:
            epilogue(parts)
            return

        @pl.when(k == 0)
        def _():
            for a, p in zip(acc_refs, parts):
                a[...] = p

        @pl.when(k > 0)
        def _():
            for a, p in zip(acc_refs, parts):
                a[...] += p

        @pl.when(k == nk - 1)
        def _():
            epilogue([a[...] for a in acc_refs])

    compute()


def _matmul(x, ws, *, out_dtype, tm, tn, tk=None, epi="none", res=None, gate=None, seq=None, batch=None,
            rows=None, head=None, name="matmul"):
    parts = x if isinstance(x, list) else [x]
    parts = [p if isinstance(p, tuple) else (p, None) for p in parts]
    x_dtype = parts[0][0].dtype
    kdim = sum(lat.shape[1] for lat, _ in parts)
    lat_tiles = parts[0][0].shape[0] // tm
    m = parts[0][0].shape[0] + (0 if parts[0][1] is None else parts[0][1].shape[0])
    rows = m if rows is None else rows
    n = ws[0].shape[-1]
    tk = kdim if tk is None else tk
    nk = kdim // tk
    assert rows % tm == 0 and n % tn == 0 and kdim % tk == 0 and (len(parts) == 1 or nk == 1)
    n_w = len(ws)

    in_specs, args = [], []
    for lat, ctx in parts:
        width = tk if len(parts) == 1 else lat.shape[1]
        if ctx is None:
            in_specs.append(pl.BlockSpec((tm, width), lambda i, j, k: (i, k)))
            args.append(lat)
        else:
            in_specs += [pl.BlockSpec((tm, width), lambda i, j, k: (jnp.minimum(i, lat_tiles - 1), k)),
                         pl.BlockSpec((tm, width), lambda i, j, k: (jnp.maximum(i - lat_tiles, 0), k))]
            args += [lat, ctx]
    in_specs += [pl.BlockSpec((tk, tn), lambda i, j, k: (k, j))] * n_w
    args += list(ws)
    n_x = len(args) - n_w
    vmem = 2 * (n_x * _nbytes((tm, tk), x_dtype) // len(parts) + n_w * _nbytes((tk, tn), ws[0].dtype)
                + _nbytes((tm, tn), out_dtype))
    vmem += _nbytes((tm, tk), x_dtype) if n_x > 1 else 0
    vmem += 2 * n_w * _nbytes((tm, tn), F32)
    if ws[0].dtype != x_dtype:
        vmem += n_w * _nbytes((tk, tn), x_dtype)
    if epi == "gres":
        midx = _mod_index(tm, seq, batch)
        in_specs += [pl.BlockSpec((tm, tn), lambda i, j, k: (i, j)),
                     pl.BlockSpec((1, 1, tn), lambda i, j, k: (midx(i), 0, j))]
        args += [res, gate]
        vmem += 2 * _nbytes((tm, tn), F32)
    head_kw = {}
    if epi == "heads":
        gain, rope, rot_dim = head.get("gain"), head.get("rope"), head.get("rot_dim", 0)
        head_kw = dict(head_norm=gain is not None, rot_dim=rot_dim, scale=head.get("scale", 1.0))
        if gain is not None:
            in_specs.append(pl.BlockSpec((1, LANES), lambda i, j, k: (0, 0)))
            args.append(gain.reshape(1, LANES))
        if rot_dim:
            lat_tiles, seq_tiles = head["lat_rows"] // tm, seq // tm
            assert rope[0].shape[0] >= seq + tm
            tab = lambda i, j, k: (jnp.where(i < lat_tiles, i % seq_tiles, seq_tiles), 0)
            in_specs += [pl.BlockSpec((tm, LANES), tab)] * 3
            args += list(rope)
            vmem += 2 * 3 * _nbytes((tm, LANES), F32)
        vmem += 6 * _nbytes((tm, LANES), F32)
    scratch = [pltpu.VMEM((tm, tn), F32) for _ in range(n_w)] if nk > 1 else []
    vmem += len(scratch) * _nbytes((tm, tn), F32)
    return pl.pallas_call(
        functools.partial(_mm_kernel, n_w=n_w, nk=nk, epi=epi, x_parts=tuple(ctx is not None for _, ctx in parts),
                          lat_tiles=lat_tiles, **head_kw),
        grid=(rows // tm, n // tn, nk),
        in_specs=in_specs,
        out_specs=pl.BlockSpec((tm, tn), lambda i, j, k: (i, j)),
        out_shape=jax.ShapeDtypeStruct((rows, n), out_dtype),
        scratch_shapes=scratch,
        compiler_params=_params(("parallel", "parallel", "arbitrary"), vmem),
        name=name,
    )(*args)


def _rope_tables(seq, grid_w, rot_dim, pad_rows):
    n_rows = seq // grid_w
    rows = jnp.repeat(jnp.arange(n_rows, dtype=F32), grid_w)
    cols = jnp.tile(jnp.arange(grid_w, dtype=F32), n_rows)
    n_freq = rot_dim // 4
    inv_freq = jnp.power(ROPE_THETA, -jnp.arange(n_freq, dtype=F32) / n_freq)
    ang = jnp.concatenate([rows[:, None] * inv_freq, cols[:, None] * inv_freq], axis=-1)
    cos, sin = jnp.cos(ang), jnp.sin(ang)
    zero = jnp.zeros_like(sin)
    reps = LANES // rot_dim
    cos_t = jnp.tile(jnp.concatenate([cos, cos], axis=-1), (1, reps))
    up_t = jnp.tile(jnp.concatenate([-sin, zero], axis=-1), (1, reps))
    down_t = jnp.tile(jnp.concatenate([zero, sin], axis=-1), (1, reps))
    ident = jnp.ones((pad_rows, LANES), F32)
    zpad = jnp.zeros((pad_rows, LANES), F32)
    return (jnp.concatenate([cos_t, ident]), jnp.concatenate([up_t, zpad]), jnp.concatenate([down_t, zpad]))


def _values_t_kernel(x_ref, o_ref, *, n_heads):
    for h in range(n_heads):
        cols = slice(h * LANES, (h + 1) * LANES)
        o_ref[0, cols, :] = x_ref[:, cols].astype(F32).T.astype(o_ref.dtype)


def _values_t(src, col0, n_heads, *, chunk, row0, rows, name):
    hb = math.gcd(col0, n_heads)
    width = hb * LANES
    assert row0 % chunk == 0 and rows % chunk == 0
    vmem = 2 * (_nbytes((chunk, width), src.dtype) + _nbytes((chunk, width), BF16)) + 4 * _nbytes((chunk, LANES), F32)
    return pl.pallas_call(
        functools.partial(_values_t_kernel, n_heads=hb),
        grid=(rows // chunk, n_heads // hb),
        in_specs=[pl.BlockSpec((chunk, width), lambda i, j: (row0 // chunk + i, col0 // hb + j))],
        out_specs=pl.BlockSpec((1, width, chunk), lambda i, j: (i, j, 0)),
        out_shape=jax.ShapeDtypeStruct((rows // chunk, n_heads * LANES, chunk), BF16),
        compiler_params=_params(("parallel", "parallel"), vmem),
        name=name,
    )(src)


def _kv_chunk(seq):
    return _pick(seq // 2, (512, 256, 128))


def _qk(q, k):
    return lax.dot_general(q, k, (((1,), (1,)), ((), ())), preferred_element_type=F32)


def _cat(refs, rows=None):
    parts = [r[...] if rows is None else r[rows, :] for r in refs]
    return parts[0] if len(parts) == 1 else jnp.concatenate(parts, axis=-1)


def _flash_kernel(*refs, mode, n_q, has_lat, tq, tqs, tkc, seq, window):
    refs = list(refs)
    q_refs = [refs.pop(0) for _ in range(n_q)]
    kc_refs = [refs.pop(0) for _ in range(n_q)]
    vc_ref = refs.pop(0)
    if has_lat:
        kl_refs = [refs.pop(0) for _ in range(n_q)]
        vl_ref = refs.pop(0)
    if mode == "window":
        sink_ref = refs.pop(0)
    if mode == "diff":
        lam_ref, subln_ref = refs.pop(0), refs.pop(0)
    o_ref = refs.pop(0)
    n_state = 2 if mode == "diff" else 1
    n_sub = tq // tqs
    chain_ids = [(sub, st) for sub in range(n_sub) for st in range(n_state)]
    per_chain = 7
    state_refs = {cid: tuple(refs[per_chain * i:][:3]) for i, cid in enumerate(chain_ids)}
    score_refs = {cid: tuple(refs[per_chain * i + 3:][:2]) for i, cid in enumerate(chain_ids)}
    smax_refs = {cid: tuple(refs[per_chain * i + 5:][:2]) for i, cid in enumerate(chain_ids)}
    qi = pl.program_id(2)

    def scores(k, only_sub=None):
        out = {}
        for sub, st in chain_ids:
            if only_sub is not None and sub != only_sub:
                continue
            q = _cat(q_refs, pl.ds(sub * tqs, tqs))
            if mode == "diff":
                lane = lax.broadcasted_iota(jnp.int32, q.shape, 1)
                keep = (lane < LANES // 2) if st == 0 else (lane >= LANES // 2)
                q = jnp.where(keep, q, jnp.zeros_like(q))
            out[sub, st] = _qk(k, q)
        return out

    def absorb(s_of, vt, *, first=False, mask=None):
        for cid, s in s_of.items():
            m_ref, l_ref, acc_ref = state_refs[cid]
            if isinstance(s, tuple):
                s, s_max = s
            else:
                if mask is not None:
                    s = jnp.where(mask, s, MASK_VALUE)
                s_max = jnp.max(s, axis=0, keepdims=True)
            if first and mode == "window":
                m_old = jnp.broadcast_to(sink_ref[0][:, :1], (1, tqs))
                m_new = jnp.maximum(m_old, s_max)
                l_old = jnp.exp2(m_old - m_new)
            elif first:
                m_new, l_old = s_max, None
            else:
                m_old = m_ref[...]
                m_new = jnp.maximum(m_old, s_max)
                alpha = jnp.exp2(m_old - m_new)
                l_old = alpha * l_ref[...]
            p = jnp.exp2(s - m_new)
            l_new = jnp.sum(p, axis=0, keepdims=True)
            pv = jnp.dot(vt, p.astype(vt.dtype), preferred_element_type=F32)
            m_ref[...] = m_new
            l_ref[...] = l_new if l_old is None else l_old + l_new
            acc_ref[...] = pv if first else alpha * acc_ref[...] + pv

    def lat_keys(c):
        return _cat(kl_refs, pl.ds(pl.multiple_of(c * tkc, tkc), tkc))

    def stash(slot, s_of):
        for cid, s in s_of.items():
            score_refs[cid][slot][...] = s
            smax_refs[cid][slot][...] = jnp.max(s, axis=0, keepdims=True)

    def fetch(slot):
        return {cid: (score_refs[cid][slot][...], smax_refs[cid][slot][...]) for cid in chain_ids}

    s_ctx = scores(_cat(kc_refs))
    if has_lat and mode == "window":
        tkw = tqs + 2 * window
        blocks = []
        for sub in range(n_sub):
            q_pos0 = qi * tq + sub * tqs
            start = pl.multiple_of(jnp.clip(q_pos0 - window, 0, seq - tkw), LANES)
            kp = start + lax.broadcasted_iota(jnp.int32, (tkw, tqs), 0)
            qp = q_pos0 + lax.broadcasted_iota(jnp.int32, (tkw, tqs), 1)
            blocks.append((scores(_cat(kl_refs, pl.ds(start, tkw)), only_sub=sub), start // LANES,
                           jnp.abs(qp - kp) <= window))
        absorb(s_ctx, vc_ref[0], first=True)
        for s_of, first_chunk, mask in blocks:
            vt = jnp.concatenate([vl_ref[first_chunk + j] for j in range(tkw // LANES)], axis=-1)
            absorb(s_of, vt, mask=mask)
    elif has_lat:
        n_chunks = seq // tkc
        stash(0, scores(lat_keys(0)))
        absorb(s_ctx, vc_ref[0], first=True)

        def body(i, carry):
            c = 2 * i
            stash(1, scores(lat_keys(c + 1)))
            absorb(fetch(0), vl_ref[c])
            stash(0, scores(lat_keys(c + 2)))
            absorb(fetch(1), vl_ref[c + 1])
            return carry

        lax.fori_loop(0, n_chunks // 2 - 1, body, 0)
        stash(1, scores(lat_keys(n_chunks - 1)))
        absorb(fetch(0), vl_ref[n_chunks - 2])
        absorb(fetch(1), vl_ref[n_chunks - 1])
    else:
        absorb(s_ctx, vc_ref[0], first=True)

    for sub in range(n_sub):
        rows = pl.ds(sub * tqs, tqs)
        outs = [(state_refs[sub, st][2][...] / state_refs[sub, st][1][...]).T for st in range(n_state)]
        if mode == "diff":
            lam_rows = lam_ref[...]
            lam = (jnp.exp(jnp.sum(lam_rows[0:1] * lam_rows[1:2], axis=-1, keepdims=True))
                   - jnp.exp(jnp.sum(lam_rows[2:3] * lam_rows[3:4], axis=-1, keepdims=True)) + D_LAMBDA_INIT)
            out = _norm_rows(outs[0] - lam * outs[1], subln_ref[...]) * (1.0 - D_LAMBDA_INIT)
        else:
            out = outs[0]
        o_ref[rows, :] = out.astype(o_ref.dtype)


def _flash(q_parts, k_parts, v, *, n_heads, group, batch, seq, ctx_len, lat_rows, ctx_queries=False,
           mode="softmax", sink=None, lam=None, subln=None, window=0, name="flash"):
    has_lat = not ctx_queries
    ctx_blk0 = lat_rows // ctx_len
    if ctx_queries:
        tq, nq = ctx_len, 1
        q_row = lambda b, h, i: ctx_blk0 + b
        out_rows = batch * ctx_len
        o_row = lambda b, h, i: b
    else:
        tq = _pick(seq, (1024, 512, 256, 128))
        nq = seq // tq
        q_row = lambda b, h, i: b * nq + i
        out_rows = lat_rows
        o_row = q_row
    tqs = min(tq, 256)
    tkc = _kv_chunk(seq)
    n_state = 2 if mode == "diff" else 1
    if mode == "window":
        assert seq >= tqs + 2 * window and window % 16 == 0
    n_q = len(q_parts)
    in_specs, args = [], []
    vmem = 0

    def add(arr, block, imap):
        nonlocal vmem
        in_specs.append(pl.BlockSpec(block, imap))
        args.append(arr)
        vmem += 2 * _nbytes(block, arr.dtype)

    for arr, c0 in q_parts:
        add(arr, (tq, LANES), lambda b, h, i, c0=c0: (q_row(b, h, i), c0 + h))
    for arr, c0, per_head in k_parts:
        add(arr, (ctx_len, LANES), lambda b, h, i, c0=c0, ph=per_head: (ctx_blk0 + b, c0 + (h // group) * ph))
    vt_lat, vt_ctx = v
    add(vt_ctx, (1, LANES, ctx_len), lambda b, h, i: (b, h // group, 0))
    if has_lat:
        for arr, c0, per_head in k_parts:
            add(arr, (seq, LANES), lambda b, h, i, c0=c0, ph=per_head: (b, c0 + (h // group) * ph))
        tkv = vt_lat.shape[-1]
        assert tkv == (LANES if mode == "window" else tkc)
        add(vt_lat, (seq // tkv, LANES, tkv), lambda b, h, i: (b, h // group, 0))
    if mode == "window":
        add(sink, (1, 1, LANES), lambda b, h, i: (h, 0, 0))
    if mode == "diff":
        add(lam, (4, LANES), lambda b, h, i: (0, 0))
        add(subln, (1, LANES), lambda b, h, i: (0, 0))
    n_sub = tq // tqs
    scratch = [pltpu.VMEM((1, tqs), F32), pltpu.VMEM((1, tqs), F32), pltpu.VMEM((LANES, tqs), F32),
               pltpu.VMEM((tkc, tqs), F32), pltpu.VMEM((tkc, tqs), F32),
               pltpu.VMEM((1, tqs), F32), pltpu.VMEM((1, tqs), F32)] * (n_state * n_sub)
    vmem += 2 * n_state * n_sub * _nbytes((tkc, tqs), F32)
    vmem += 2 * _nbytes((tq, LANES), BF16) + 3 * _nbytes((n_state, tq, LANES), F32)
    vmem += 4 * n_state * (tq // tqs) * _nbytes((tqs, max(tkc, ctx_len, tqs + 2 * window)), F32)
    kern = functools.partial(_flash_kernel, mode=mode, n_q=n_q, has_lat=has_lat, tq=tq, tqs=tqs, tkc=tkc,
                             seq=seq, window=window)
    return pl.pallas_call(
        kern,
        grid=(batch, n_heads, nq),
        in_specs=in_specs,
        out_specs=pl.BlockSpec((tq, LANES), lambda b, h, i: (o_row(b, h, i), h)),
        out_shape=jax.ShapeDtypeStruct((out_rows, n_heads * LANES), BF16),
        scratch_shapes=scratch,
        compiler_params=_params(("parallel", "parallel", "parallel"), vmem),
        name=name,
    )(*args)


def _route_kernel(lg_ref, info_ref, cnt_ref, base_ref, *, n_experts, tb):
    step = pl.program_id(0)

    @pl.when(step == 0)
    def _():
        base_ref[...] = jnp.zeros_like(base_ref)

    lane = lax.broadcasted_iota(jnp.int32, (tb, LANES), 1).astype(F32)
    logits = jnp.where(lane < n_experts, lg_ref[...], -jnp.inf)
    v1 = jnp.max(logits, axis=-1, keepdims=True)
    i1 = jnp.min(jnp.where(logits == v1, lane, float(LANES)), axis=-1, keepdims=True)
    hot1 = lane == i1
    rest = jnp.where(hot1, -jnp.inf, logits)
    v2 = jnp.max(rest, axis=-1, keepdims=True)
    i2 = jnp.min(jnp.where(rest == v2, lane, float(LANES)), axis=-1, keepdims=True)
    hot2 = lane == i2
    e = jnp.exp(v2 - v1)
    w1 = 1.0 / (1.0 + e)
    w2 = e / (1.0 + e)
    sel = jnp.where(hot1 | hot2, 1.0, 0.0)
    r = lax.broadcasted_iota(jnp.int32, (tb, tb), 0)
    c = lax.broadcasted_iota(jnp.int32, (tb, tb), 1)
    tri = jnp.where(c < r, 1.0, 0.0).astype(BF16)
    before = jnp.dot(tri, sel.astype(BF16), preferred_element_type=F32) + base_ref[...]
    rank1 = jnp.sum(jnp.where(hot1, before, 0.0), axis=-1, keepdims=True)
    rank2 = jnp.sum(jnp.where(hot2, before, 0.0), axis=-1, keepdims=True)
    base_ref[...] += jnp.sum(sel, axis=0, keepdims=True)
    cnt_ref[...] = jnp.broadcast_to(base_ref[...], cnt_ref.shape)
    info = jnp.zeros((tb, LANES), F32)
    for slot, val in enumerate((i1, i2, rank1, rank2, w1, w2)):
        info = jnp.where(lane == float(slot), val, info)
    info_ref[...] = info


def _route(logits, n_experts):
    n = logits.shape[0]
    tb = _pick(n, (256, 128, 64, 32, 16, 8))
    vmem = 4 * _nbytes((tb, LANES), F32) + 16 * _nbytes((tb, max(tb, LANES)), F32)
    return pl.pallas_call(
        functools.partial(_route_kernel, n_experts=n_experts, tb=tb),
        grid=(n // tb,),
        in_specs=[pl.BlockSpec((tb, LANES), lambda i: (i, 0))],
        out_specs=[pl.BlockSpec((tb, LANES), lambda i: (i, 0)),
                   pl.BlockSpec((8, LANES), lambda i: (0, 0))],
        out_shape=[jax.ShapeDtypeStruct((n, LANES), F32), jax.ShapeDtypeStruct((8, LANES), F32)],
        scratch_shapes=[pltpu.VMEM((1, LANES), F32)],
        compiler_params=_params(("arbitrary",), vmem),
        name="moe_route",
    )(logits)


def _row_copy(src_hbm, row, dst_vmem, slot, sem):
    return pltpu.make_async_copy(src_hbm.at[pl.ds(row, 1)], dst_vmem.at[pl.ds(slot, 1)], sem)


def _gather_kernel(src_ref, h_hbm, o_ref, buf, sem, *, tm):
    base = pl.program_id(0) * tm

    def issue(r, carry):
        _row_copy(h_hbm, src_ref[base + r], buf, r, sem).start()
        return carry

    lax.fori_loop(0, tm, issue, 0, unroll=DMA_LOOP_UNROLL)

    def drain(r, carry):
        _row_copy(h_hbm, 0, buf, r, sem).wait()
        return carry

    lax.fori_loop(0, tm, drain, 0, unroll=DMA_LOOP_UNROLL)
    packed = buf[...]
    half = packed.shape[1]
    left = lax.bitcast_convert_type(lax.shift_left(packed, jnp.uint32(16)), F32)
    right = lax.bitcast_convert_type(packed & jnp.uint32(0xFFFF0000), F32)
    o_ref[:, :half] = left.astype(o_ref.dtype)
    o_ref[:, half:] = right.astype(o_ref.dtype)


def _gather_rows(src_rows, h, *, tm):
    p = src_rows.shape[0]
    d = 2 * h.shape[1]
    vmem = 3 * _nbytes((tm, d // 2), jnp.uint32) + 2 * _nbytes((tm, d), BF16)
    gs = pltpu.PrefetchScalarGridSpec(
        num_scalar_prefetch=1, grid=(p // tm,),
        in_specs=[pl.BlockSpec(memory_space=pl.ANY)],
        out_specs=pl.BlockSpec((tm, d), lambda i, src: (i, 0)),
        scratch_shapes=[pltpu.VMEM((tm, d // 2), jnp.uint32), pltpu.SemaphoreType.DMA(())])
    return pl.pallas_call(
        functools.partial(_gather_kernel, tm=tm), grid_spec=gs,
        out_shape=jax.ShapeDtypeStruct((p, d), BF16),
        compiler_params=_params(("arbitrary",), vmem),
        name="moe_gather",
    )(src_rows, h)


def _combine_kernel(d1_ref, d2_ref, y_hbm, x_ref, info_ref, gate_ref, g_ref, o_ref, buf1, buf2, sem, *, tb):
    base = pl.program_id(0) * tb

    def issue(r, carry):
        _row_copy(y_hbm, d1_ref[base + r], buf1, r, sem.at[0]).start()
        _row_copy(y_hbm, d2_ref[base + r], buf2, r, sem.at[1]).start()
        return carry

    lax.fori_loop(0, tb, issue, 0, unroll=DMA_LOOP_UNROLL)

    def drain(r, carry):
        _row_copy(y_hbm, 0, buf1, r, sem.at[0]).wait()
        _row_copy(y_hbm, 0, buf2, r, sem.at[1]).wait()
        return carry

    lax.fori_loop(0, tb, drain, 0, unroll=DMA_LOOP_UNROLL)
    info = info_ref[...]
    w1, w2 = info[:, 4:5], info[:, 5:6]
    x = x_ref[...] + gate_ref[0] * (w1 * buf1[...] + w2 * buf2[...])
    o_ref[...] = _norm_rows(x, g_ref[...])


def _combine(dest1, dest2, y, x, info, gate, final_norm, *, seq, batch):
    n, d = x.shape
    tb = _pick(seq, (256, 128, 64, 32, 16, 8))
    midx = _mod_index(tb, seq, batch)
    vmem = 2 * _nbytes((tb, d), F32) * 3 + 2 * _nbytes((tb, d), F32) + 2 * _nbytes((tb, LANES), F32)
    gs = pltpu.PrefetchScalarGridSpec(
        num_scalar_prefetch=2, grid=(n // tb,),
        in_specs=[pl.BlockSpec(memory_space=pl.ANY),
                  pl.BlockSpec((tb, d), lambda i, a, b: (i, 0)),
                  pl.BlockSpec((tb, LANES), lambda i, a, b: (i, 0)),
                  pl.BlockSpec((1, 1, d), lambda i, a, b: (midx(i), 0, 0)),
                  pl.BlockSpec((1, d), lambda i, a, b: (0, 0))],
        out_specs=pl.BlockSpec((tb, d), lambda i, a, b: (i, 0)),
        scratch_shapes=[pltpu.VMEM((tb, d), F32), pltpu.VMEM((tb, d), F32), pltpu.SemaphoreType.DMA((2,))])
    return pl.pallas_call(
        functools.partial(_combine_kernel, tb=tb), grid_spec=gs,
        out_shape=jax.ShapeDtypeStruct((n, d), F32),
        compiler_params=_params(("arbitrary",), vmem),
        name="moe_combine_norm",
    )(dest1, dest2, y, x, info, gate, final_norm.reshape(1, d))


def _expert_mm_kernel(te_ref, used_ref, x_ref, *refs, n_w, swiglu):
    w_refs, o_ref, wbf_refs = refs[:n_w], refs[n_w], refs[n_w + 1:]
    i = pl.program_id(1)
    live = i < used_ref[0]
    fresh = jnp.logical_or(i == 0, te_ref[i] != te_ref[jnp.maximum(i - 1, 0)])

    @pl.when(jnp.logical_and(live, fresh))
    def _():
        for w, wb in zip(w_refs, wbf_refs):
            wb[...] = w[0].astype(wb.dtype)

    @pl.when(live)
    def _():
        x = x_ref[...]
        parts = [jnp.dot(x, wb[...], preferred_element_type=F32) for wb in wbf_refs]
        out = parts[0] * jax.nn.sigmoid(parts[0]) * parts[1] if swiglu else parts[0]
        o_ref[...] = out.astype(o_ref.dtype)

    @pl.when(jnp.logical_not(live))
    def _():
        o_ref[...] = jnp.zeros_like(o_ref)


def _expert_matmul(x, ws, tile_expert, tiles_used, *, out_dtype, tm, tn, swiglu=False, name):
    rows, kdim = x.shape
    n = ws[0].shape[-1]
    n_w = len(ws)
    assert rows % tm == 0 and n % tn == 0
    vmem = 2 * (_nbytes((tm, kdim), x.dtype) + n_w * _nbytes((kdim, tn), ws[0].dtype) + _nbytes((tm, tn), out_dtype))
    vmem += n_w * (_nbytes((kdim, tn), BF16) + 2 * _nbytes((tm, tn), F32))
    gs = pltpu.PrefetchScalarGridSpec(
        num_scalar_prefetch=2, grid=(n // tn, rows // tm),
        in_specs=[pl.BlockSpec((tm, kdim), lambda j, i, te, nu: (i, 0))]
        + [pl.BlockSpec((1, kdim, tn), lambda j, i, te, nu: (te[i], 0, j))] * n_w,
        out_specs=pl.BlockSpec((tm, tn), lambda j, i, te, nu: (i, j)),
        scratch_shapes=[pltpu.VMEM((kdim, tn), BF16)] * n_w)
    return pl.pallas_call(
        functools.partial(_expert_mm_kernel, n_w=n_w, swiglu=swiglu), grid_spec=gs,
        out_shape=jax.ShapeDtypeStruct((rows, n), out_dtype),
        compiler_params=_params(("arbitrary", "arbitrary"), vmem),
        name=name,
    )(tile_expert, tiles_used, x, *ws)


def _moe(h_packed, logits, x, gate, final_norm, w_gate, w_up, w_down, *, dims):
    n, d = x.shape
    n_exp = dims.n_experts
    tm = _pick(n, (512, 256, 128, 64, 32, 16))
    p = 2 * n + n_exp * tm
    info, counts = _route(logits, n_exp)
    cnt = counts[0, :n_exp].astype(jnp.int32)
    padded = ((cnt + tm - 1) // tm) * tm
    ends = jnp.cumsum(padded)
    starts = ends - padded
    e1, e2 = info[:, 0].astype(jnp.int32), info[:, 1].astype(jnp.int32)
    dest1 = starts[e1] + info[:, 2].astype(jnp.int32)
    dest2 = starts[e2] + info[:, 3].astype(jnp.int32)
    token = jnp.arange(n, dtype=jnp.int32)
    src = jnp.zeros((p,), jnp.int32).at[dest1].set(token).at[dest2].set(token)
    tile_start = jnp.arange(p // tm, dtype=jnp.int32) * tm
    tile_expert = jnp.minimum(jnp.sum(tile_start[:, None] >= ends[None, :], axis=1), n_exp - 1).astype(jnp.int32)
    tiles_used = (ends[-1:] // tm).astype(jnp.int32)

    xs = _gather_rows(src, h_packed, tm=_pick(tm, (256, 128, 64, 32, 16)))
    f = w_gate.shape[-1]
    act = _expert_matmul(xs, [w_gate, w_up], tile_expert, tiles_used, out_dtype=BF16, tm=tm,
                         tn=_pick(f, (512, 256, 128)), swiglu=True, name="moe_gate_up")
    y = _expert_matmul(act, [w_down], tile_expert, tiles_used, out_dtype=F32, tm=tm,
                       tn=_pick(d, (1024, 512, 256, 128)), name="moe_down")
    return _combine(dest1, dest2, y, x, info, gate, final_norm, seq=dims.seq, batch=dims.batch)


def _pad_cols(w, n):
    return w if w.shape[-1] == n else jnp.pad(w, [(0, 0)] * (w.ndim - 1) + [(0, n - w.shape[-1])])


def _forward(dims, x, c, ctx, c_ctx,
             ada_w0, ada_b0, norm_mix0, norm_ffn0, w_in0, a_q_norm, a_k_norm, b_q_norm, b_kv_norm,
             w_uq, w_ukv, w_o0, ffn_w_gate, ffn_w_up, ffn_w_down,
             ada_w1, ada_b1, norm_mix1, norm_ffn1, w_in1, c_sink, d_lam_q1, d_lam_k1, d_lam_q2, d_lam_k2,
             d_subln, w_o1, router_w, moe_w_gate, moe_w_up, moe_w_down, final_norm):
    dm, bsz, seq, ctx_len = dims.d_model, dims.batch, dims.seq, dims.ctx_len
    nl, nc = bsz * seq, bsz * ctx_len
    nt = nl + nc
    ha, hka, hb = dims.a_heads, dims.a_kv_heads, dims.b_heads
    hc, hkc, hd = dims.c_heads, dims.c_kv_heads, dims.d_heads
    big = lambda n: _pick(n, (1024, 512, 256, 128, 64, 32, 16))
    tm_all, tm_lat = big(math.gcd(nt, seq)), big(math.gcd(nl, seq))

    xa = jnp.concatenate([x.reshape(nl, dm), ctx.reshape(nc, dm)], axis=0)
    cond = jnp.concatenate([c, c_ctx[None, :], jnp.zeros((8 - bsz - 1, dm), F32)], axis=0)

    def mods(ada_w, ada_b):
        m = _modulation(cond, ada_w, ada_b).reshape(8, N_MOD, 1, dm)
        return [m[:, i] for i in range(N_MOD)]

    rope128 = _rope_tables(seq, dims.grid_w, 128, tm_all)
    rope64 = _rope_tables(seq, dims.grid_w, 64, tm_all)
    flash = functools.partial(_flash, batch=bsz, seq=seq, ctx_len=ctx_len, lat_rows=nl)

    def project(src, w, *, rows=nt, out_dtype=BF16, name, **head):
        n = w.shape[1]
        tm = tm_all if rows == nt else tm_lat
        if head:
            return _matmul(src, [w.astype(BF16)], out_dtype=out_dtype, tm=tm, tn=_pick(n, (1024, 512, 256, 128)),
                           rows=rows, epi="heads", head=dict(head, lat_rows=nl), seq=seq, name=name)
        return _matmul(src, [w.astype(BF16)], out_dtype=out_dtype, tm=tm, tn=_pick(n, (512, 256, 128)),
                       rows=rows, name=name)

    def values_t(src, col0, n_heads, name, chunk=_kv_chunk(seq)):
        return (_values_t(src, col0, n_heads, chunk=chunk, row0=0, rows=nl, name=name),
                _values_t(src, col0, n_heads, chunk=ctx_len, row0=nl, rows=nc, name=name + "_ctx"))

    sh1, sc1, g1, sh2, sc2, g2 = mods(ada_w0, ada_b0)
    h = _norm_mod(xa, norm_mix0, sh1, sc1, rows=nt, seq=seq, batch=bsz)
    na_q, na_kv, rq, rkv = ha * HEAD_DIM, hka * HEAD_DIM, dims.b_q_rank, dims.b_kv_rank
    c_ak, c_av, c_cq, c_ckv, c_kr = na_q, na_q + na_kv, na_q + 2 * na_kv, na_q + 2 * na_kv + rq, na_q + 2 * na_kv + rq + rkv
    aq = project(h, w_in0[:, :c_ak], gain=a_q_norm, rope=rope128, rot_dim=128, scale=LOG2E * HEAD_DIM ** -0.5,
                 name="proj_aq")
    ak = project(h, w_in0[:, c_ak:c_av], gain=a_k_norm, rope=rope128, rot_dim=128, name="proj_ak")
    bk_rope = project(h, _pad_cols(w_in0[:, c_kr:], LANES), rope=rope64, rot_dim=64, name="proj_bk_rope")
    rest = project(h, jnp.concatenate([w_in0[:, c_cq:c_kr], w_in0[:, c_av:c_cq]], axis=1), out_dtype=F32,
                   name="proj_ranks_av")
    av = values_t(rest, (rq + rkv) // HEAD_DIM, hka, "vt_a")
    cq = _rank_norm(rest, 0, rq, b_q_norm)
    ckv = _rank_norm(rest, rq, rkv, b_kv_norm)
    b_scale = LOG2E * (HEAD_DIM + 64) ** -0.5
    uq = w_uq.reshape(rq, hb, HEAD_DIM + 64) * b_scale
    bq_nope = project(cq, uq[:, :, :HEAD_DIM].reshape(rq, hb * HEAD_DIM), name="b_up_q_nope")
    bq_rope = project(cq, _pad_cols(uq[:, :, HEAD_DIM:], LANES).reshape(rq, hb * LANES), rope=rope64, rot_dim=64,
                      name="b_up_q_rope")
    ukv = w_ukv.reshape(rkv, hb, 2 * HEAD_DIM)
    ukv = jnp.concatenate([ukv[:, :, :HEAD_DIM].reshape(rkv, hb * HEAD_DIM),
                           ukv[:, :, HEAD_DIM:].reshape(rkv, hb * HEAD_DIM)], axis=1)
    bkv = project(ckv, ukv, name="b_up_kv")

    a_lat = flash([(aq, 0)], [(ak, 0, 1)], av, n_heads=ha, group=ha // hka, name="attn_a")
    a_ctx = flash([(aq, 0)], [(ak, 0, 1)], av, n_heads=ha, group=ha // hka, ctx_queries=True, name="attn_a_ctx")
    b_args = ([(bq_nope, 0), (bq_rope, 0)], [(bkv, 0, 1), (bk_rope, 0, 0)], values_t(bkv, hb, hb, "vt_b"))
    b_lat = flash(*b_args, n_heads=hb, group=1, name="attn_b")
    b_ctx = flash(*b_args, n_heads=hb, group=1, ctx_queries=True, name="attn_b_ctx")
    xa = _matmul([(a_lat, a_ctx), (b_lat, b_ctx)], [w_o0], out_dtype=F32, tm=tm_all, tn=_pick(dm, (512, 256, 128)),
                 epi="gres", res=xa, gate=g1, seq=seq, batch=bsz, name="out_proj0")

    h = _norm_mod(xa, norm_ffn0, sh2, sc2, rows=nt, seq=seq, batch=bsz)
    ffn = dims.ffn_dim
    act = _matmul(h, [ffn_w_gate, ffn_w_up], out_dtype=BF16, tm=tm_all, tn=_pick(ffn, (256, 128)), epi="swiglu",
                  name="ffn_gate_up")
    tk_down = ffn // 2 if (ffn // 2) % LANES == 0 and ffn > 2048 else ffn
    xa = _matmul(act, [ffn_w_down.astype(BF16)], out_dtype=F32, tm=tm_all, tn=_pick(dm, (512, 256, 128)), tk=tk_down,
                 epi="gres", res=xa, gate=g2, seq=seq, batch=bsz, name="ffn_down")

    sh1, sc1, g1, sh2, sc2, g2 = mods(ada_w1, ada_b1)
    h = _norm_mod(xa, norm_mix1, sh1, sc1, rows=nt, seq=seq, batch=bsz)
    bounds = [0]
    for n_heads in (hc, hkc, hkc, hd, hd, hd):
        bounds.append(bounds[-1] + n_heads * HEAD_DIM)
    w_cq, w_ck, w_cv, w_dq, w_dk, w_dv = (w_in1[:, lo:hi] for lo, hi in zip(bounds[:-1], bounds[1:]))
    cq_ = project(h, w_cq, rows=nl, rope=rope128, rot_dim=128, scale=LOG2E * HEAD_DIM ** -0.5, name="proj_cq")
    ck_ = project(h, w_ck, rope=rope128, rot_dim=128, name="proj_ck")
    dq_ = project(h, w_dq, rows=nl, rope=rope64, rot_dim=64, scale=LOG2E * 64 ** -0.5, name="proj_dq")
    dk_ = project(h, w_dk, rope=rope64, rot_dim=64, name="proj_dk")
    values = project(h, jnp.concatenate([w_cv, w_dv], axis=1), name="proj_cv_dv")
    cv_ = values_t(values, 0, hkc, "vt_c", chunk=LANES)
    dv_ = values_t(values, hkc, hd, "vt_d")
    sink = jnp.broadcast_to(LOG2E * c_sink.astype(F32)[:, None, None], (hc, 1, LANES))
    lam = jnp.stack([_pad_cols(v.astype(F32)[None, :], LANES)[0] for v in (d_lam_q1, d_lam_k1, d_lam_q2, d_lam_k2)])
    c_out = flash([(cq_, 0)], [(ck_, 0, 1)], cv_, n_heads=hc, group=hc // hkc, mode="window", sink=sink,
                  window=dims.window, name="attn_c")
    d_out = flash([(dq_, 0)], [(dk_, 0, 1)], dv_, n_heads=hd, group=1, mode="diff", lam=lam,
                  subln=d_subln.reshape(1, LANES), name="attn_d")
    xl = _matmul([c_out, d_out], [w_o1], out_dtype=F32, tm=tm_lat, tn=_pick(dm, (512, 256, 128)),
                 epi="gres", res=xa, gate=g1, seq=seq, batch=bsz, rows=nl, name="out_proj1")

    rw = _pad_cols(router_w, LANES)
    rw_hi = rw.astype(BF16)
    rw_lo = (rw - rw_hi.astype(F32)).astype(BF16)
    h_packed, logits = _norm_mod(xl, norm_ffn1, sh2, sc2, rows=nl, seq=seq, batch=bsz, router=(rw_hi, rw_lo))
    out = _moe(h_packed, logits, xl, g2, final_norm, moe_w_gate, moe_w_up, moe_w_down, dims=dims)
    return out.reshape(bsz, seq, dm)


def _rank_norm_kernel(x_ref, g_ref, o_ref):
    o_ref[...] = _norm_rows(x_ref[...], g_ref[...]).astype(o_ref.dtype)


def _rank_norm(src, col0, width, gain):
    rows = src.shape[0]
    tm = _pick(rows, (256, 128, 64, 32, 16))
    assert col0 % width == 0
    vmem = 2 * (_nbytes((tm, width), F32) + _nbytes((tm, width), BF16))
    return pl.pallas_call(
        _rank_norm_kernel,
        grid=(rows // tm,),
        in_specs=[pl.BlockSpec((tm, width), lambda i: (i, col0 // width)),
                  pl.BlockSpec((1, width), lambda i: (0, 0))],
        out_specs=pl.BlockSpec((tm, width), lambda i: (i, 0)),
        out_shape=jax.ShapeDtypeStruct((rows, width), BF16),
        compiler_params=_params(("parallel",), vmem),
        name="rank_norm",
    )(src, gain.reshape(1, width))


_DIMS = Dims(d_model=4096, batch=4, seq=4096, ctx_len=256, grid_w=64, a_heads=16, a_kv_heads=4, b_heads=16,
             b_q_rank=1536, b_kv_rank=512, c_heads=16, c_kv_heads=4, window=128, d_heads=16, ffn_dim=11008,
             n_experts=8, expert_dim=3584)


def kernel(x, c, ctx, c_ctx, ada_w0, ada_b0, norm_mix0, norm_ffn0, w_in0, a_q_norm, a_k_norm, b_q_norm, b_kv_norm, w_uq, w_ukv, w_o0, ffn_w_gate, ffn_w_up, ffn_w_down, ada_w1, ada_b1, norm_mix1, norm_ffn1, w_in1, c_sink, d_lam_q1, d_lam_k1, d_lam_q2, d_lam_k2, d_subln, w_o1, router_w, moe_w_gate, moe_w_up, moe_w_down, final_norm):
    return _forward(_DIMS, x, c, ctx, c_ctx, ada_w0, ada_b0, norm_mix0, norm_ffn0, w_in0, a_q_norm, a_k_norm,
                    b_q_norm, b_kv_norm, w_uq, w_ukv, w_o0, ffn_w_gate, ffn_w_up, ffn_w_down, ada_w1, ada_b1,
                    norm_mix1, norm_ffn1, w_in1, c_sink, d_lam_q1, d_lam_k1, d_lam_q2, d_lam_k2, d_subln, w_o1,
                    router_w, moe_w_gate, moe_w_up, moe_w_down, final_norm)
```

```python
import functools
import math
from typing import NamedTuple

import jax
import jax.numpy as jnp
from jax import lax
from jax.experimental import pallas as pl
from jax.experimental.pallas import tpu as pltpu

F32 = jnp.float32
BF16 = jnp.bfloat16

LANES = 128
SUBLANES = 8
HEAD_DIM = 128
ROPE_THETA = 10000.0
NORM_EPS = 1e-6
MASK_VALUE = -1e30
LOG2E = math.log2(math.e)
N_MOD = 6
D_LAYER_INDEX = 1
D_LAMBDA_INIT = 0.8 - 0.6 * math.exp(-0.3 * D_LAYER_INDEX)
VMEM_HEADROOM = 6 << 20
DMA_LOOP_UNROLL = 8
HEAD_SUBBLOCK = 256


class Dims(NamedTuple):
    d_model: int
    batch: int
    seq: int
    ctx_len: int
    grid_w: int
    a_heads: int
    a_kv_heads: int
    b_heads: int
    b_q_rank: int
    b_kv_rank: int
    c_heads: int
    c_kv_heads: int
    window: int
    d_heads: int
    ffn_dim: int
    n_experts: int
    expert_dim: int


def _pick(n, prefs):
    for p in prefs:
        if n % p == 0:
            return p
    raise ValueError(f"no tile in {prefs} divides {n}")


def _params(sem, vmem_bytes):
    return pltpu.CompilerParams(dimension_semantics=sem, vmem_limit_bytes=int(vmem_bytes) + VMEM_HEADROOM)


def _nbytes(shape, dtype):
    return math.prod(shape) * jnp.dtype(dtype).itemsize


def _mod_kernel(c_ref, w_ref, b_ref, o_ref):
    c = c_ref[...]
    s = (c * jax.nn.sigmoid(c)).astype(BF16)
    o_ref[...] = jnp.dot(s, w_ref[...].astype(BF16), preferred_element_type=F32) + b_ref[...]


def _modulation(cond, ada_w, ada_b):
    rows, d = cond.shape
    n = ada_w.shape[1]
    tn = _pick(n, (512, 256, 128))
    vmem = 2 * (_nbytes((d, tn), F32) + _nbytes((rows, tn), F32) * 2) + _nbytes((rows, d), F32) * 2
    return pl.pallas_call(
        _mod_kernel,
        grid=(n // tn,),
        in_specs=[pl.BlockSpec((rows, d), lambda j: (0, 0)),
                  pl.BlockSpec((d, tn), lambda j: (0, j)),
                  pl.BlockSpec((1, tn), lambda j: (0, j))],
        out_specs=pl.BlockSpec((rows, tn), lambda j: (0, j)),
        out_shape=jax.ShapeDtypeStruct((rows, n), F32),
        compiler_params=_params(("parallel",), vmem),
        name="modulation",
    )(cond, ada_w, ada_b.reshape(1, n))


def _norm_rows(x, g):
    var = jnp.mean(x * x, axis=-1, keepdims=True)
    return x * lax.rsqrt(var + NORM_EPS) * g


def _norm_mod_kernel(x_ref, g_ref, sh_ref, sc_ref, o_ref):
    y = _norm_rows(x_ref[...], g_ref[...])
    o_ref[...] = (y * (1.0 + sc_ref[0]) + sh_ref[0]).astype(o_ref.dtype)


def _norm_mod_pair_kernel(lat_ref, ctx_ref, g_ref, sh_ref, sc_ref, o_ref, *, lat_tiles):
    x = jnp.where(pl.program_id(0) < lat_tiles, lat_ref[...], ctx_ref[...])
    y = _norm_rows(x, g_ref[...])
    o_ref[...] = (y * (1.0 + sc_ref[0]) + sh_ref[0]).astype(o_ref.dtype)


def _norm_mod_router_kernel(x_ref, g_ref, sh_ref, sc_ref, rhi_ref, rlo_ref, of_ref, lg_ref):
    y = _norm_rows(x_ref[...], g_ref[...])
    h = y * (1.0 + sc_ref[0]) + sh_ref[0]
    hi = h.astype(BF16)
    lo = (h - hi.astype(F32)).astype(BF16)
    half = h.shape[1] // 2
    left = lax.bitcast_convert_type(hi[:, :half].astype(F32), jnp.uint32)
    right = lax.bitcast_convert_type(hi[:, half:].astype(F32), jnp.uint32)
    of_ref[...] = lax.shift_right_logical(left, jnp.uint32(16)) | right
    lg_ref[...] = (jnp.dot(hi, rhi_ref[...], preferred_element_type=F32)
                   + jnp.dot(hi, rlo_ref[...], preferred_element_type=F32)
                   + jnp.dot(lo, rhi_ref[...], preferred_element_type=F32))


def _mod_index(tm, seq, batch):
    return lambda i: jnp.minimum((i * tm) // seq, batch)


def _norm_mod(x, g, shift, scale, *, rows, seq, batch, router=None):
    pair = isinstance(x, tuple)
    d = (x[0] if pair else x).shape[1]
    tm = _pick(math.gcd(rows, seq), (256, 128, 64, 32, 16))
    midx = _mod_index(tm, seq, batch)
    tail_specs = [pl.BlockSpec((1, d), lambda i: (0, 0)),
                  pl.BlockSpec((1, 1, d), lambda i: (midx(i), 0, 0)),
                  pl.BlockSpec((1, 1, d), lambda i: (midx(i), 0, 0))]
    vmem = 2 * (_nbytes((tm, d), F32) + _nbytes((tm, d), BF16)) + 6 * _nbytes((1, d), F32)
    if pair:
        lat_tiles = x[0].shape[0] // tm
        assert x[0].shape[0] % tm == 0 and x[1].shape[0] % tm == 0 and router is None
        x_specs = [pl.BlockSpec((tm, d), lambda i: (jnp.minimum(i, lat_tiles - 1), 0)),
                   pl.BlockSpec((tm, d), lambda i: (jnp.maximum(i - lat_tiles, 0), 0))]
        return pl.pallas_call(
            functools.partial(_norm_mod_pair_kernel, lat_tiles=lat_tiles),
            grid=(rows // tm,),
            in_specs=x_specs + tail_specs,
            out_specs=pl.BlockSpec((tm, d), lambda i: (i, 0)),
            out_shape=jax.ShapeDtypeStruct((rows, d), BF16),
            compiler_params=_params(("parallel",), vmem + 2 * _nbytes((tm, d), F32)),
            name="norm_modulate_inputs",
        )(*x, g.reshape(1, d), shift, scale)
    in_specs = [pl.BlockSpec((tm, d), lambda i: (i, 0))] + tail_specs
    if router is None:
        return pl.pallas_call(
            _norm_mod_kernel,
            grid=(rows // tm,),
            in_specs=in_specs,
            out_specs=pl.BlockSpec((tm, d), lambda i: (i, 0)),
            out_shape=jax.ShapeDtypeStruct((rows, d), BF16),
            compiler_params=_params(("parallel",), vmem),
            name="norm_modulate",
        )(x, g.reshape(1, d), shift, scale)
    rhi, rlo = router
    vmem += 2 * (_nbytes((tm, d), F32) + _nbytes((tm, LANES), F32)) + 4 * _nbytes((d, LANES), BF16)
    return pl.pallas_call(
        _norm_mod_router_kernel,
        grid=(rows // tm,),
        in_specs=in_specs + [pl.BlockSpec((d, LANES), lambda i: (0, 0)),
                             pl.BlockSpec((d, LANES), lambda i: (0, 0))],
        out_specs=[pl.BlockSpec((tm, d // 2), lambda i: (i, 0)),
                   pl.BlockSpec((tm, LANES), lambda i: (i, 0))],
        out_shape=[jax.ShapeDtypeStruct((rows, d // 2), jnp.uint32),
                   jax.ShapeDtypeStruct((rows, LANES), F32)],
        compiler_params=_params(("parallel",), vmem),
        name="norm_modulate_router",
    )(x, g.reshape(1, d), shift, scale, rhi, rlo)


def _head_op(x, gain, rope, rot_dim, scale):
    if gain is not None:
        x = _norm_rows(x, gain)
    if rot_dim:
        cos, up, down = rope
        half = rot_dim // 2
        x = x * cos + pltpu.roll(x, LANES - half, 1) * up + pltpu.roll(x, half, 1) * down
    return x if scale == 1.0 else x * scale


def _mm_kernel(*refs, n_w, nk, epi, x_parts=(False,), res_pair=False, lat_tiles=0, head_norm=False, rot_dim=0,
               scale=1.0):
    refs = list(refs)
    x_refs = []
    for has_ctx in x_parts:
        x_refs.append((refs.pop(0), refs.pop(0) if has_ctx else None))
    x_dtype = x_refs[0][0].dtype
    w_refs = refs[:n_w]

    def load_x():
        on_lat = pl.program_id(0) < lat_tiles
        parts = [lat[...] if ctx is None else jnp.where(on_lat, lat[...], ctx[...]) for lat, ctx in x_refs]
        return parts[0] if len(parts) == 1 else jnp.concatenate(parts, axis=-1)

    def weight(w_ref, cols=slice(None)):
        w = w_ref[:, cols]
        return w if w.dtype == x_dtype else w.astype(x_dtype)

    pos = n_w
    if epi == "gres":
        res_ref, res_ctx_ref = refs[pos], (refs[pos + 1] if res_pair else None)
        gate_ref = refs[pos + 1 + int(res_pair)]
        pos += 2 + int(res_pair)
    if epi == "heads":
        gain_ref = refs[pos] if head_norm else None
        pos += int(head_norm)
        rope_refs = refs[pos:pos + 3] if rot_dim else None
        pos += 3 if rot_dim else 0
    o_ref = refs[pos]
    acc_refs = refs[pos + 1:]
    k = pl.program_id(2)

    def epilogue(parts):
        if epi == "swiglu":
            g, u = parts
            out = g * jax.nn.sigmoid(g) * u
        elif epi == "gres":
            res = res_ref[...]
            if res_pair:
                res = jnp.where(pl.program_id(0) < lat_tiles, res, res_ctx_ref[...])
            out = res + gate_ref[0] * parts[0]
        else:
            out = parts[0]
        o_ref[...] = out.astype(o_ref.dtype)

    def compute_heads():
        x = load_x()
        gain = gain_ref[...] if head_norm else None
        rope = tuple(r[...] for r in rope_refs) if rot_dim else None
        tn = o_ref.shape[1]
        sub = HEAD_SUBBLOCK if tn % HEAD_SUBBLOCK == 0 else tn
        starts = list(range(0, tn, sub))
        dot = lambda c: jnp.dot(x, weight(w_refs[0], slice(c, c + sub)), preferred_element_type=F32)
        part = dot(starts[0])
        for idx, c in enumerate(starts):
            nxt = dot(starts[idx + 1]) if idx + 1 < len(starts) else None
            for hh in range(sub // LANES):
                cols = slice(hh * LANES, (hh + 1) * LANES)
                o_ref[:, c + hh * LANES:c + (hh + 1) * LANES] = _head_op(
                    part[:, cols], gain, rope, rot_dim, scale).astype(o_ref.dtype)
            part = nxt

    def compute():
        if epi == "heads":
            compute_heads()
            return
        x = load_x()
        parts = [jnp.dot(x, weight(w), preferred_element_type=F32) for w in w_refs]
        if nk == 1:
            epilogue(parts)
            return

        @pl.when(k == 0)
        def _():
            for a, p in zip(acc_refs, parts):
                a[...] = p

        @pl.when(k > 0)
        def _():
            for a, p in zip(acc_refs, parts):
                a[...] += p

        @pl.when(k == nk - 1)
        def _():
            epilogue([a[...] for a in acc_refs])

    compute()


def _matmul(x, ws, *, out_dtype, tm, tn, tk=None, epi="none", res=None, gate=None, seq=None, batch=None,
            rows=None, head=None, name="matmul"):
    parts = x if isinstance(x, list) else [x]
    parts = [p if isinstance(p, tuple) else (p, None) for p in parts]
    x_dtype = parts[0][0].dtype
    kdim = sum(lat.shape[1] for lat, _ in parts)
    lat_tiles = parts[0][0].shape[0] // tm
    m = parts[0][0].shape[0] + (0 if parts[0][1] is None else parts[0][1].shape[0])
    rows = m if rows is None else rows
    n = ws[0].shape[-1]
    tk = kdim if tk is None else tk
    nk = kdim // tk
    assert rows % tm == 0 and n % tn == 0 and kdim % tk == 0 and (len(parts) == 1 or nk == 1)
    n_w = len(ws)

    in_specs, args = [], []
    for lat, ctx in parts:
        width = tk if len(parts) == 1 else lat.shape[1]
        if ctx is None:
            in_specs.append(pl.BlockSpec((tm, width), lambda i, j, k: (i, k)))
            args.append(lat)
        else:
            in_specs += [pl.BlockSpec((tm, width), lambda i, j, k: (jnp.minimum(i, lat_tiles - 1), k)),
                         pl.BlockSpec((tm, width), lambda i, j, k: (jnp.maximum(i - lat_tiles, 0), k))]
            args += [lat, ctx]
    in_specs += [pl.BlockSpec((tk, tn), lambda i, j, k: (k, j))] * n_w
    args += list(ws)
    n_x = len(args) - n_w
    vmem = 2 * (n_x * _nbytes((tm, tk), x_dtype) // len(parts) + n_w * _nbytes((tk, tn), ws[0].dtype)
                + _nbytes((tm, tn), out_dtype))
    vmem += _nbytes((tm, tk), x_dtype) if n_x > 1 else 0
    vmem += 2 * n_w * _nbytes((tm, tn), F32)
    if ws[0].dtype != x_dtype:
        vmem += n_w * _nbytes((tk, tn), x_dtype)
    res_pair = isinstance(res, tuple)
    if epi == "gres":
        midx = _mod_index(tm, seq, batch)
        if res_pair:
            in_specs += [pl.BlockSpec((tm, tn), lambda i, j, k: (jnp.minimum(i, lat_tiles - 1), j)),
                         pl.BlockSpec((tm, tn), lambda i, j, k: (jnp.maximum(i - lat_tiles, 0), j))]
            args += list(res)
        else:
            in_specs.append(pl.BlockSpec((tm, tn), lambda i, j, k: (i, j)))
            args.append(res)
        in_specs.append(pl.BlockSpec((1, 1, tn), lambda i, j, k: (midx(i), 0, j)))
        args.append(gate)
        vmem += 2 * (1 + int(res_pair)) * _nbytes((tm, tn), F32)
    head_kw = {}
    if epi == "heads":
        gain, rope, rot_dim = head.get("gain"), head.get("rope"), head.get("rot_dim", 0)
        head_kw = dict(head_norm=gain is not None, rot_dim=rot_dim, scale=head.get("scale", 1.0))
        if gain is not None:
            in_specs.append(pl.BlockSpec((1, LANES), lambda i, j, k: (0, 0)))
            args.append(gain.reshape(1, LANES))
        if rot_dim:
            lat_tiles, seq_tiles = head["lat_rows"] // tm, seq // tm
            assert rope[0].shape[0] >= seq + tm
            tab = lambda i, j, k: (jnp.where(i < lat_tiles, i % seq_tiles, seq_tiles), 0)
            in_specs += [pl.BlockSpec((tm, LANES), tab)] * 3
            args += list(rope)
            vmem += 2 * 3 * _nbytes((tm, LANES), F32)
        vmem += 6 * _nbytes((tm, LANES), F32)
    scratch = [pltpu.VMEM((tm, tn), F32) for _ in range(n_w)] if nk > 1 else []
    vmem += len(scratch) * _nbytes((tm, tn), F32)
    return pl.pallas_call(
        functools.partial(_mm_kernel, n_w=n_w, nk=nk, epi=epi, x_parts=tuple(ctx is not None for _, ctx in parts),
                          res_pair=res_pair, lat_tiles=lat_tiles, **head_kw),
        grid=(rows // tm, n // tn, nk),
        in_specs=in_specs,
        out_specs=pl.BlockSpec((tm, tn), lambda i, j, k: (i, j)),
        out_shape=jax.ShapeDtypeStruct((rows, n), out_dtype),
        scratch_shapes=scratch,
        compiler_params=_params(("parallel", "parallel", "arbitrary"), vmem),
        name=name,
    )(*args)


def _rope_tables(seq, grid_w, rot_dim, pad_rows):
    n_rows = seq // grid_w
    rows = jnp.repeat(jnp.arange(n_rows, dtype=F32), grid_w)
    cols = jnp.tile(jnp.arange(grid_w, dtype=F32), n_rows)
    n_freq = rot_dim // 4
    inv_freq = jnp.power(ROPE_THETA, -jnp.arange(n_freq, dtype=F32) / n_freq)
    ang = jnp.concatenate([rows[:, None] * inv_freq, cols[:, None] * inv_freq], axis=-1)
    cos, sin = jnp.cos(ang), jnp.sin(ang)
    zero = jnp.zeros_like(sin)
    reps = LANES // rot_dim
    cos_t = jnp.tile(jnp.concatenate([cos, cos], axis=-1), (1, reps))
    up_t = jnp.tile(jnp.concatenate([-sin, zero], axis=-1), (1, reps))
    down_t = jnp.tile(jnp.concatenate([zero, sin], axis=-1), (1, reps))
    ident = jnp.ones((pad_rows, LANES), F32)
    zpad = jnp.zeros((pad_rows, LANES), F32)
    return (jnp.concatenate([cos_t, ident]), jnp.concatenate([up_t, zpad]), jnp.concatenate([down_t, zpad]))


def _values_t_kernel(x_ref, o_ref, *, n_heads):
    for h in range(n_heads):
        cols = slice(h * LANES, (h + 1) * LANES)
        o_ref[0, cols, :] = x_ref[:, cols].astype(F32).T.astype(o_ref.dtype)


def _values_t(src, col0, n_heads, *, chunk, row0, rows, name):
    hb = math.gcd(col0, n_heads)
    width = hb * LANES
    assert row0 % chunk == 0 and rows % chunk == 0
    vmem = 2 * (_nbytes((chunk, width), src.dtype) + _nbytes((chunk, width), BF16)) + 4 * _nbytes((chunk, LANES), F32)
    return pl.pallas_call(
        functools.partial(_values_t_kernel, n_heads=hb),
        grid=(rows // chunk, n_heads // hb),
        in_specs=[pl.BlockSpec((chunk, width), lambda i, j: (row0 // chunk + i, col0 // hb + j))],
        out_specs=pl.BlockSpec((1, width, chunk), lambda i, j: (i, j, 0)),
        out_shape=jax.ShapeDtypeStruct((rows // chunk, n_heads * LANES, chunk), BF16),
        compiler_params=_params(("parallel", "parallel"), vmem),
        name=name,
    )(src)


def _kv_chunk(seq):
    return _pick(seq // 2, (512, 256, 128))


def _qk(q, k):
    return lax.dot_general(q, k, (((1,), (1,)), ((), ())), preferred_element_type=F32)


def _cat(refs, rows=None):
    parts = [r[...] if rows is None else r[rows, :] for r in refs]
    return parts[0] if len(parts) == 1 else jnp.concatenate(parts, axis=-1)


def _flash_kernel(*refs, mode, n_q, has_lat, tq, tqs, tkc, seq, window):
    refs = list(refs)
    q_refs = [refs.pop(0) for _ in range(n_q)]
    kc_refs = [refs.pop(0) for _ in range(n_q)]
    vc_ref = refs.pop(0)
    if has_lat:
        kl_refs = [refs.pop(0) for _ in range(n_q)]
        vl_ref = refs.pop(0)
    if mode == "window":
        sink_ref = refs.pop(0)
    if mode == "diff":
        lam_ref, subln_ref = refs.pop(0), refs.pop(0)
    o_ref = refs.pop(0)
    n_state = 2 if mode == "diff" else 1
    n_sub = tq // tqs
    chain_ids = [(sub, st) for sub in range(n_sub) for st in range(n_state)]
    per_chain = 7
    state_refs = {cid: tuple(refs[per_chain * i:][:3]) for i, cid in enumerate(chain_ids)}
    score_refs = {cid: tuple(refs[per_chain * i + 3:][:2]) for i, cid in enumerate(chain_ids)}
    smax_refs = {cid: tuple(refs[per_chain * i + 5:][:2]) for i, cid in enumerate(chain_ids)}
    qi = pl.program_id(2)

    def scores(k, only_sub=None):
        out = {}
        for sub, st in chain_ids:
            if only_sub is not None and sub != only_sub:
                continue
            q = _cat(q_refs, pl.ds(sub * tqs, tqs))
            if mode == "diff":
                lane = lax.broadcasted_iota(jnp.int32, q.shape, 1)
                keep = (lane < LANES // 2) if st == 0 else (lane >= LANES // 2)
                q = jnp.where(keep, q, jnp.zeros_like(q))
            out[sub, st] = _qk(k, q)
        return out

    def absorb(s_of, vt, *, first=False, mask=None):
        for cid, s in s_of.items():
            m_ref, l_ref, acc_ref = state_refs[cid]
            if isinstance(s, tuple):
                s, s_max = s
            else:
                if mask is not None:
                    s = jnp.where(mask, s, MASK_VALUE)
                s_max = jnp.max(s, axis=0, keepdims=True)
            if first and mode == "window":
                m_old = jnp.broadcast_to(sink_ref[0][:, :1], (1, tqs))
                m_new = jnp.maximum(m_old, s_max)
                l_old = jnp.exp2(m_old - m_new)
            elif first:
                m_new, l_old = s_max, None
            else:
                m_old = m_ref[...]
                m_new = jnp.maximum(m_old, s_max)
                alpha = jnp.exp2(m_old - m_new)
                l_old = alpha * l_ref[...]
            p = jnp.exp2(s - m_new)
            l_new = jnp.sum(p, axis=0, keepdims=True)
            pv = jnp.dot(vt, p.astype(vt.dtype), preferred_element_type=F32)
            m_ref[...] = m_new
            l_ref[...] = l_new if l_old is None else l_old + l_new
            acc_ref[...] = pv if first else alpha * acc_ref[...] + pv

    def lat_keys(c):
        return _cat(kl_refs, pl.ds(pl.multiple_of(c * tkc, tkc), tkc))

    def stash(slot, s_of):
        for cid, s in s_of.items():
            score_refs[cid][slot][...] = s
            smax_refs[cid][slot][...] = jnp.max(s, axis=0, keepdims=True)

    def fetch(slot):
        return {cid: (score_refs[cid][slot][...], smax_refs[cid][slot][...]) for cid in chain_ids}

    s_ctx = scores(_cat(kc_refs))
    if has_lat and mode == "window":
        tkw = tqs + 2 * window
        blocks = []
        for sub in range(n_sub):
            q_pos0 = qi * tq + sub * tqs
            start = pl.multiple_of(jnp.clip(q_pos0 - window, 0, seq - tkw), LANES)
            kp = start + lax.broadcasted_iota(jnp.int32, (tkw, tqs), 0)
            qp = q_pos0 + lax.broadcasted_iota(jnp.int32, (tkw, tqs), 1)
            blocks.append((scores(_cat(kl_refs, pl.ds(start, tkw)), only_sub=sub), start // LANES,
                           jnp.abs(qp - kp) <= window))
        absorb(s_ctx, vc_ref[0], first=True)
        for s_of, first_chunk, mask in blocks:
            vt = jnp.concatenate([vl_ref[first_chunk + j] for j in range(tkw // LANES)], axis=-1)
            absorb(s_of, vt, mask=mask)
    elif has_lat:
        n_chunks = seq // tkc
        stash(0, scores(lat_keys(0)))
        absorb(s_ctx, vc_ref[0], first=True)

        def body(i, carry):
            c = 2 * i
            stash(1, scores(lat_keys(c + 1)))
            absorb(fetch(0), vl_ref[c])
            stash(0, scores(lat_keys(c + 2)))
            absorb(fetch(1), vl_ref[c + 1])
            return carry

        lax.fori_loop(0, n_chunks // 2 - 1, body, 0)
        stash(1, scores(lat_keys(n_chunks - 1)))
        absorb(fetch(0), vl_ref[n_chunks - 2])
        absorb(fetch(1), vl_ref[n_chunks - 1])
    else:
        absorb(s_ctx, vc_ref[0], first=True)

    for sub in range(n_sub):
        rows = pl.ds(sub * tqs, tqs)
        outs = [(state_refs[sub, st][2][...] / state_refs[sub, st][1][...]).T for st in range(n_state)]
        if mode == "diff":
            lam_rows = lam_ref[...]
            lam = (jnp.exp(jnp.sum(lam_rows[0:1] * lam_rows[1:2], axis=-1, keepdims=True))
                   - jnp.exp(jnp.sum(lam_rows[2:3] * lam_rows[3:4], axis=-1, keepdims=True)) + D_LAMBDA_INIT)
            out = _norm_rows(outs[0] - lam * outs[1], subln_ref[...]) * (1.0 - D_LAMBDA_INIT)
        else:
            out = outs[0]
        o_ref[rows, :] = out.astype(o_ref.dtype)


def _flash(q_parts, k_parts, v, *, n_heads, group, batch, seq, ctx_len, lat_rows, ctx_queries=False,
           mode="softmax", sink=None, lam=None, subln=None, window=0, name="flash"):
    has_lat = not ctx_queries
    ctx_blk0 = lat_rows // ctx_len
    if ctx_queries:
        tq, nq = ctx_len, 1
        q_row = lambda b, h, i: ctx_blk0 + b
        out_rows = batch * ctx_len
        o_row = lambda b, h, i: b
    else:
        tq = _pick(seq, (1024, 512, 256, 128))
        nq = seq // tq
        q_row = lambda b, h, i: b * nq + i
        out_rows = lat_rows
        o_row = q_row
    tqs = min(tq, 256)
    tkc = _kv_chunk(seq)
    n_state = 2 if mode == "diff" else 1
    if mode == "window":
        assert seq >= tqs + 2 * window and window % 16 == 0
    n_q = len(q_parts)
    in_specs, args = [], []
    vmem = 0

    def add(arr, block, imap):
        nonlocal vmem
        in_specs.append(pl.BlockSpec(block, imap))
        args.append(arr)
        vmem += 2 * _nbytes(block, arr.dtype)

    for arr, c0 in q_parts:
        add(arr, (tq, LANES), lambda b, h, i, c0=c0: (q_row(b, h, i), c0 + h))
    for arr, c0, per_head in k_parts:
        add(arr, (ctx_len, LANES), lambda b, h, i, c0=c0, ph=per_head: (ctx_blk0 + b, c0 + (h // group) * ph))
    vt_lat, vt_ctx = v
    add(vt_ctx, (1, LANES, ctx_len), lambda b, h, i: (b, h // group, 0))
    if has_lat:
        for arr, c0, per_head in k_parts:
            add(arr, (seq, LANES), lambda b, h, i, c0=c0, ph=per_head: (b, c0 + (h // group) * ph))
        tkv = vt_lat.shape[-1]
        assert tkv == (LANES if mode == "window" else tkc)
        add(vt_lat, (seq // tkv, LANES, tkv), lambda b, h, i: (b, h // group, 0))
    if mode == "window":
        add(sink, (1, 1, LANES), lambda b, h, i: (h, 0, 0))
    if mode == "diff":
        add(lam, (4, LANES), lambda b, h, i: (0, 0))
        add(subln, (1, LANES), lambda b, h, i: (0, 0))
    n_sub = tq // tqs
    scratch = [pltpu.VMEM((1, tqs), F32), pltpu.VMEM((1, tqs), F32), pltpu.VMEM((LANES, tqs), F32),
               pltpu.VMEM((tkc, tqs), F32), pltpu.VMEM((tkc, tqs), F32),
               pltpu.VMEM((1, tqs), F32), pltpu.VMEM((1, tqs), F32)] * (n_state * n_sub)
    vmem += 2 * n_state * n_sub * _nbytes((tkc, tqs), F32)
    vmem += 2 * _nbytes((tq, LANES), BF16) + 3 * _nbytes((n_state, tq, LANES), F32)
    vmem += 4 * n_state * (tq // tqs) * _nbytes((tqs, max(tkc, ctx_len, tqs + 2 * window)), F32)
    kern = functools.partial(_flash_kernel, mode=mode, n_q=n_q, has_lat=has_lat, tq=tq, tqs=tqs, tkc=tkc,
                             seq=seq, window=window)
    return pl.pallas_call(
        kern,
        grid=(batch, n_heads, nq),
        in_specs=in_specs,
        out_specs=pl.BlockSpec((tq, LANES), lambda b, h, i: (o_row(b, h, i), h)),
        out_shape=jax.ShapeDtypeStruct((out_rows, n_heads * LANES), BF16),
        scratch_shapes=scratch,
        compiler_params=_params(("parallel", "parallel", "parallel"), vmem),
        name=name,
    )(*args)


def _route_kernel(lg_ref, info_ref, cnt_ref, base_ref, *, n_experts, tb):
    step = pl.program_id(0)

    @pl.when(step == 0)
    def _():
        base_ref[...] = jnp.zeros_like(base_ref)

    lane = lax.broadcasted_iota(jnp.int32, (tb, LANES), 1).astype(F32)
    logits = jnp.where(lane < n_experts, lg_ref[...], -jnp.inf)
    v1 = jnp.max(logits, axis=-1, keepdims=True)
    i1 = jnp.min(jnp.where(logits == v1, lane, float(LANES)), axis=-1, keepdims=True)
    hot1 = lane == i1
    rest = jnp.where(hot1, -jnp.inf, logits)
    v2 = jnp.max(rest, axis=-1, keepdims=True)
    i2 = jnp.min(jnp.where(rest == v2, lane, float(LANES)), axis=-1, keepdims=True)
    hot2 = lane == i2
    e = jnp.exp(v2 - v1)
    w1 = 1.0 / (1.0 + e)
    w2 = e / (1.0 + e)
    sel = jnp.where(hot1 | hot2, 1.0, 0.0)
    r = lax.broadcasted_iota(jnp.int32, (tb, tb), 0)
    c = lax.broadcasted_iota(jnp.int32, (tb, tb), 1)
    tri = jnp.where(c < r, 1.0, 0.0).astype(BF16)
    before = jnp.dot(tri, sel.astype(BF16), preferred_element_type=F32) + base_ref[...]
    rank1 = jnp.sum(jnp.where(hot1, before, 0.0), axis=-1, keepdims=True)
    rank2 = jnp.sum(jnp.where(hot2, before, 0.0), axis=-1, keepdims=True)
    base_ref[...] += jnp.sum(sel, axis=0, keepdims=True)
    cnt_ref[...] = jnp.broadcast_to(base_ref[...], cnt_ref.shape)
    info = jnp.zeros((tb, LANES), F32)
    for slot, val in enumerate((i1, i2, rank1, rank2, w1, w2)):
        info = jnp.where(lane == float(slot), val, info)
    info_ref[...] = info


def _route(logits, n_experts):
    n = logits.shape[0]
    tb = _pick(n, (256, 128, 64, 32, 16, 8))
    vmem = 4 * _nbytes((tb, LANES), F32) + 16 * _nbytes((tb, max(tb, LANES)), F32)
    return pl.pallas_call(
        functools.partial(_route_kernel, n_experts=n_experts, tb=tb),
        grid=(n // tb,),
        in_specs=[pl.BlockSpec((tb, LANES), lambda i: (i, 0))],
        out_specs=[pl.BlockSpec((tb, LANES), lambda i: (i, 0)),
                   pl.BlockSpec((SUBLANES, LANES), lambda i: (0, 0))],
        out_shape=[jax.ShapeDtypeStruct((n, LANES), F32), jax.ShapeDtypeStruct((SUBLANES, LANES), F32)],
        scratch_shapes=[pltpu.VMEM((1, LANES), F32)],
        compiler_params=_params(("arbitrary",), vmem),
        name="moe_route",
    )(logits)


def _row_copy(src_hbm, row, dst_vmem, slot, sem):
    return pltpu.make_async_copy(src_hbm.at[pl.ds(row, 1)], dst_vmem.at[pl.ds(slot, 1)], sem)


def _gather_kernel(src_ref, h_hbm, o_ref, buf, sem, *, tm):
    base = pl.program_id(0) * tm

    def issue(r, carry):
        _row_copy(h_hbm, src_ref[base + r], buf, r, sem).start()
        return carry

    lax.fori_loop(0, tm, issue, 0, unroll=DMA_LOOP_UNROLL)

    def drain(r, carry):
        _row_copy(h_hbm, 0, buf, r, sem).wait()
        return carry

    lax.fori_loop(0, tm, drain, 0, unroll=DMA_LOOP_UNROLL)
    packed = buf[...]
    half = packed.shape[1]
    left = lax.bitcast_convert_type(lax.shift_left(packed, jnp.uint32(16)), F32)
    right = lax.bitcast_convert_type(packed & jnp.uint32(0xFFFF0000), F32)
    o_ref[:, :half] = left.astype(o_ref.dtype)
    o_ref[:, half:] = right.astype(o_ref.dtype)


def _gather_rows(src_rows, h, *, tm):
    p = src_rows.shape[0]
    d = 2 * h.shape[1]
    vmem = 3 * _nbytes((tm, d // 2), jnp.uint32) + 2 * _nbytes((tm, d), BF16)
    gs = pltpu.PrefetchScalarGridSpec(
        num_scalar_prefetch=1, grid=(p // tm,),
        in_specs=[pl.BlockSpec(memory_space=pl.ANY)],
        out_specs=pl.BlockSpec((tm, d), lambda i, src: (i, 0)),
        scratch_shapes=[pltpu.VMEM((tm, d // 2), jnp.uint32), pltpu.SemaphoreType.DMA(())])
    return pl.pallas_call(
        functools.partial(_gather_kernel, tm=tm), grid_spec=gs,
        out_shape=jax.ShapeDtypeStruct((p, d), BF16),
        compiler_params=_params(("arbitrary",), vmem),
        name="moe_gather",
    )(src_rows, h)


def _combine_kernel(d1_ref, d2_ref, y_hbm, x_ref, info_ref, gate_ref, g_ref, o_ref, buf1, buf2, sem, *, tb):
    base = pl.program_id(0) * tb

    def issue(r, carry):
        _row_copy(y_hbm, d1_ref[base + r], buf1, r, sem.at[0]).start()
        _row_copy(y_hbm, d2_ref[base + r], buf2, r, sem.at[1]).start()
        return carry

    lax.fori_loop(0, tb, issue, 0, unroll=DMA_LOOP_UNROLL)

    def drain(r, carry):
        _row_copy(y_hbm, 0, buf1, r, sem.at[0]).wait()
        _row_copy(y_hbm, 0, buf2, r, sem.at[1]).wait()
        return carry

    lax.fori_loop(0, tb, drain, 0, unroll=DMA_LOOP_UNROLL)
    info = info_ref[...]
    w1, w2 = info[:, 4:5], info[:, 5:6]
    x = x_ref[...] + gate_ref[0] * (w1 * buf1[...] + w2 * buf2[...])
    o_ref[...] = _norm_rows(x, g_ref[...])


def _combine(dest1, dest2, y, x, info, gate, final_norm, *, seq, batch):
    n, d = x.shape
    tb = _pick(seq, (256, 128, 64, 32, 16, 8))
    midx = _mod_index(tb, seq, batch)
    vmem = 2 * _nbytes((tb, d), F32) * 3 + 2 * _nbytes((tb, d), F32) + 2 * _nbytes((tb, LANES), F32)
    gs = pltpu.PrefetchScalarGridSpec(
        num_scalar_prefetch=2, grid=(n // tb,),
        in_specs=[pl.BlockSpec(memory_space=pl.ANY),
                  pl.BlockSpec((tb, d), lambda i, a, b: (i, 0)),
                  pl.BlockSpec((tb, LANES), lambda i, a, b: (i, 0)),
                  pl.BlockSpec((1, 1, d), lambda i, a, b: (midx(i), 0, 0)),
                  pl.BlockSpec((1, d), lambda i, a, b: (0, 0))],
        out_specs=pl.BlockSpec((tb, d), lambda i, a, b: (i, 0)),
        scratch_shapes=[pltpu.VMEM((tb, d), F32), pltpu.VMEM((tb, d), F32), pltpu.SemaphoreType.DMA((2,))])
    return pl.pallas_call(
        functools.partial(_combine_kernel, tb=tb), grid_spec=gs,
        out_shape=jax.ShapeDtypeStruct((n, d), F32),
        compiler_params=_params(("arbitrary",), vmem),
        name="moe_combine_norm",
    )(dest1, dest2, y, x, info, gate, final_norm.reshape(1, d))


def _expert_mm_kernel(te_ref, used_ref, x_ref, *refs, n_w, swiglu):
    w_refs, o_ref, wbf_refs = refs[:n_w], refs[n_w], refs[n_w + 1:]
    i = pl.program_id(1)
    live = i < used_ref[0]
    fresh = jnp.logical_or(i == 0, te_ref[i] != te_ref[jnp.maximum(i - 1, 0)])

    @pl.when(jnp.logical_and(live, fresh))
    def _():
        for w, wb in zip(w_refs, wbf_refs):
            wb[...] = w[0].astype(wb.dtype)

    @pl.when(live)
    def _():
        x = x_ref[...]
        parts = [jnp.dot(x, wb[...], preferred_element_type=F32) for wb in wbf_refs]
        out = parts[0] * jax.nn.sigmoid(parts[0]) * parts[1] if swiglu else parts[0]
        o_ref[...] = out.astype(o_ref.dtype)

    @pl.when(jnp.logical_not(live))
    def _():
        o_ref[...] = jnp.zeros_like(o_ref)


def _expert_matmul(x, ws, tile_expert, tiles_used, *, out_dtype, tm, tn, swiglu=False, name):
    rows, kdim = x.shape
    n = ws[0].shape[-1]
    n_w = len(ws)
    assert rows % tm == 0 and n % tn == 0
    vmem = 2 * (_nbytes((tm, kdim), x.dtype) + n_w * _nbytes((kdim, tn), ws[0].dtype) + _nbytes((tm, tn), out_dtype))
    vmem += n_w * (_nbytes((kdim, tn), BF16) + 2 * _nbytes((tm, tn), F32))
    gs = pltpu.PrefetchScalarGridSpec(
        num_scalar_prefetch=2, grid=(n // tn, rows // tm),
        in_specs=[pl.BlockSpec((tm, kdim), lambda j, i, te, nu: (i, 0))]
        + [pl.BlockSpec((1, kdim, tn), lambda j, i, te, nu: (te[i], 0, j))] * n_w,
        out_specs=pl.BlockSpec((tm, tn), lambda j, i, te, nu: (i, j)),
        scratch_shapes=[pltpu.VMEM((kdim, tn), BF16)] * n_w)
    return pl.pallas_call(
        functools.partial(_expert_mm_kernel, n_w=n_w, swiglu=swiglu), grid_spec=gs,
        out_shape=jax.ShapeDtypeStruct((rows, n), out_dtype),
        compiler_params=_params(("arbitrary", "arbitrary"), vmem),
        name=name,
    )(tile_expert, tiles_used, x, *ws)


def _moe(h_packed, logits, x, gate, final_norm, w_gate, w_up, w_down, *, dims):
    n, d = x.shape
    n_exp = dims.n_experts
    tm = _pick(n, (512, 256, 128, 64, 32, 16))
    p = 2 * n + n_exp * tm
    info, counts = _route(logits, n_exp)
    cnt = counts[0, :n_exp].astype(jnp.int32)
    padded = ((cnt + tm - 1) // tm) * tm
    ends = jnp.cumsum(padded)
    starts = ends - padded
    e1, e2 = info[:, 0].astype(jnp.int32), info[:, 1].astype(jnp.int32)
    dest1 = starts[e1] + info[:, 2].astype(jnp.int32)
    dest2 = starts[e2] + info[:, 3].astype(jnp.int32)
    token = jnp.arange(n, dtype=jnp.int32)
    src = jnp.zeros((p,), jnp.int32).at[dest1].set(token).at[dest2].set(token)
    tile_start = jnp.arange(p // tm, dtype=jnp.int32) * tm
    tile_expert = jnp.minimum(jnp.sum(tile_start[:, None] >= ends[None, :], axis=1), n_exp - 1).astype(jnp.int32)
    tiles_used = (ends[-1:] // tm).astype(jnp.int32)

    xs = _gather_rows(src, h_packed, tm=_pick(tm, (256, 128, 64, 32, 16)))
    f = w_gate.shape[-1]
    act = _expert_matmul(xs, [w_gate, w_up], tile_expert, tiles_used, out_dtype=BF16, tm=tm,
                         tn=_pick(f, (512, 256, 128)), swiglu=True, name="moe_gate_up")
    y = _expert_matmul(act, [w_down], tile_expert, tiles_used, out_dtype=F32, tm=tm,
                       tn=_pick(d, (1024, 512, 256, 128)), name="moe_down")
    return _combine(dest1, dest2, y, x, info, gate, final_norm, seq=dims.seq, batch=dims.batch)


def _pad_cols(w, n):
    return w if w.shape[-1] == n else jnp.pad(w, [(0, 0)] * (w.ndim - 1) + [(0, n - w.shape[-1])])


def _forward(dims, x, c, ctx, c_ctx,
             ada_w0, ada_b0, norm_mix0, norm_ffn0, w_in0, a_q_norm, a_k_norm, b_q_norm, b_kv_norm,
             w_uq, w_ukv, w_o0, ffn_w_gate, ffn_w_up, ffn_w_down,
             ada_w1, ada_b1, norm_mix1, norm_ffn1, w_in1, c_sink, d_lam_q1, d_lam_k1, d_lam_q2, d_lam_k2,
             d_subln, w_o1, router_w, moe_w_gate, moe_w_up, moe_w_down, final_norm):
    dm, bsz, seq, ctx_len = dims.d_model, dims.batch, dims.seq, dims.ctx_len
    nl, nc = bsz * seq, bsz * ctx_len
    nt = nl + nc
    ha, hka, hb = dims.a_heads, dims.a_kv_heads, dims.b_heads
    hc, hkc, hd = dims.c_heads, dims.c_kv_heads, dims.d_heads
    big = lambda n: _pick(n, (1024, 512, 256, 128, 64, 32, 16))
    tm_all, tm_lat = big(math.gcd(nt, seq)), big(math.gcd(nl, seq))

    x_rows, ctx_rows = x.reshape(nl, dm), ctx.reshape(nc, dm)
    n_cond = -(-(bsz + 1) // SUBLANES) * SUBLANES
    cond = jnp.concatenate([c, c_ctx[None, :], jnp.zeros((n_cond - bsz - 1, dm), F32)], axis=0)

    def mods(ada_w, ada_b):
        m = _modulation(cond, ada_w, ada_b).reshape(n_cond, N_MOD, 1, dm)
        return [m[:, i] for i in range(N_MOD)]

    rope128 = _rope_tables(seq, dims.grid_w, 128, tm_all)
    rope64 = _rope_tables(seq, dims.grid_w, 64, tm_all)
    flash = functools.partial(_flash, batch=bsz, seq=seq, ctx_len=ctx_len, lat_rows=nl)

    def project(src, w, *, rows=nt, out_dtype=BF16, name, **head):
        n = w.shape[1]
        tm = tm_all if rows == nt else tm_lat
        if head:
            return _matmul(src, [w.astype(BF16)], out_dtype=out_dtype, tm=tm, tn=_pick(n, (1024, 512, 256, 128)),
                           rows=rows, epi="heads", head=dict(head, lat_rows=nl), seq=seq, name=name)
        return _matmul(src, [w.astype(BF16)], out_dtype=out_dtype, tm=tm, tn=_pick(n, (512, 256, 128)),
                       rows=rows, name=name)

    def values_t(src, col0, n_heads, name, chunk=_kv_chunk(seq)):
        return (_values_t(src, col0, n_heads, chunk=chunk, row0=0, rows=nl, name=name),
                _values_t(src, col0, n_heads, chunk=ctx_len, row0=nl, rows=nc, name=name + "_ctx"))

    sh1, sc1, g1, sh2, sc2, g2 = mods(ada_w0, ada_b0)
    h = _norm_mod((x_rows, ctx_rows), norm_mix0, sh1, sc1, rows=nt, seq=seq, batch=bsz)
    na_q, na_kv, rq, rkv = ha * HEAD_DIM, hka * HEAD_DIM, dims.b_q_rank, dims.b_kv_rank
    c_ak, c_av, c_cq, c_ckv, c_kr = na_q, na_q + na_kv, na_q + 2 * na_kv, na_q + 2 * na_kv + rq, na_q + 2 * na_kv + rq + rkv
    aq = project(h, w_in0[:, :c_ak], gain=a_q_norm, rope=rope128, rot_dim=128, scale=LOG2E * HEAD_DIM ** -0.5,
                 name="proj_aq")
    ak = project(h, w_in0[:, c_ak:c_av], gain=a_k_norm, rope=rope128, rot_dim=128, name="proj_ak")
    bk_rope = project(h, _pad_cols(w_in0[:, c_kr:], LANES), rope=rope64, rot_dim=64, name="proj_bk_rope")
    rest = project(h, jnp.concatenate([w_in0[:, c_cq:c_kr], w_in0[:, c_av:c_cq]], axis=1), out_dtype=F32,
                   name="proj_ranks_av")
    av = values_t(rest, (rq + rkv) // HEAD_DIM, hka, "vt_a")
    cq = _rank_norm(rest, 0, rq, b_q_norm)
    ckv = _rank_norm(rest, rq, rkv, b_kv_norm)
    b_scale = LOG2E * (HEAD_DIM + 64) ** -0.5
    uq = w_uq.reshape(rq, hb, HEAD_DIM + 64) * b_scale
    bq_nope = project(cq, uq[:, :, :HEAD_DIM].reshape(rq, hb * HEAD_DIM), name="b_up_q_nope")
    bq_rope = project(cq, _pad_cols(uq[:, :, HEAD_DIM:], LANES).reshape(rq, hb * LANES), rope=rope64, rot_dim=64,
                      name="b_up_q_rope")
    ukv = w_ukv.reshape(rkv, hb, 2 * HEAD_DIM)
    ukv = jnp.concatenate([ukv[:, :, :HEAD_DIM].reshape(rkv, hb * HEAD_DIM),
                           ukv[:, :, HEAD_DIM:].reshape(rkv, hb * HEAD_DIM)], axis=1)
    bkv = project(ckv, ukv, name="b_up_kv")

    a_lat = flash([(aq, 0)], [(ak, 0, 1)], av, n_heads=ha, group=ha // hka, name="attn_a")
    a_ctx = flash([(aq, 0)], [(ak, 0, 1)], av, n_heads=ha, group=ha // hka, ctx_queries=True, name="attn_a_ctx")
    b_args = ([(bq_nope, 0), (bq_rope, 0)], [(bkv, 0, 1), (bk_rope, 0, 0)], values_t(bkv, hb, hb, "vt_b"))
    b_lat = flash(*b_args, n_heads=hb, group=1, name="attn_b")
    b_ctx = flash(*b_args, n_heads=hb, group=1, ctx_queries=True, name="attn_b_ctx")
    xa = _matmul([(a_lat, a_ctx), (b_lat, b_ctx)], [w_o0], out_dtype=F32, tm=tm_all, tn=_pick(dm, (512, 256, 128)),
                 epi="gres", res=(x_rows, ctx_rows), gate=g1, seq=seq, batch=bsz, name="out_proj0")

    h = _norm_mod(xa, norm_ffn0, sh2, sc2, rows=nt, seq=seq, batch=bsz)
    ffn = dims.ffn_dim
    act = _matmul(h, [ffn_w_gate, ffn_w_up], out_dtype=BF16, tm=tm_all, tn=_pick(ffn, (256, 128)), epi="swiglu",
                  name="ffn_gate_up")
    xa = _matmul(act, [ffn_w_down.astype(BF16)], out_dtype=F32, tm=tm_all // 2 if ffn > 4096 else tm_all,
                 tn=_pick(dm, (512, 256, 128)), epi="gres", res=xa, gate=g2, seq=seq, batch=bsz, name="ffn_down")

    sh1, sc1, g1, sh2, sc2, g2 = mods(ada_w1, ada_b1)
    h = _norm_mod(xa, norm_mix1, sh1, sc1, rows=nt, seq=seq, batch=bsz)
    bounds = [0]
    for n_heads in (hc, hkc, hkc, hd, hd, hd):
        bounds.append(bounds[-1] + n_heads * HEAD_DIM)
    w_cq, w_ck, w_cv, w_dq, w_dk, w_dv = (w_in1[:, lo:hi] for lo, hi in zip(bounds[:-1], bounds[1:]))
    cq_ = project(h, w_cq, rows=nl, rope=rope128, rot_dim=128, scale=LOG2E * HEAD_DIM ** -0.5, name="proj_cq")
    ck_ = project(h, w_ck, rope=rope128, rot_dim=128, name="proj_ck")
    dq_ = project(h, w_dq, rows=nl, rope=rope64, rot_dim=64, scale=LOG2E * 64 ** -0.5, name="proj_dq")
    dk_ = project(h, w_dk, rope=rope64, rot_dim=64, name="proj_dk")
    values = project(h, jnp.concatenate([w_cv, w_dv], axis=1), name="proj_cv_dv")
    cv_ = values_t(values, 0, hkc, "vt_c", chunk=LANES)
    dv_ = values_t(values, hkc, hd, "vt_d")
    sink = jnp.broadcast_to(LOG2E * c_sink.astype(F32)[:, None, None], (hc, 1, LANES))
    lam = jnp.stack([_pad_cols(v.astype(F32)[None, :], LANES)[0] for v in (d_lam_q1, d_lam_k1, d_lam_q2, d_lam_k2)])
    c_out = flash([(cq_, 0)], [(ck_, 0, 1)], cv_, n_heads=hc, group=hc // hkc, mode="window", sink=sink,
                  window=dims.window, name="attn_c")
    d_out = flash([(dq_, 0)], [(dk_, 0, 1)], dv_, n_heads=hd, group=1, mode="diff", lam=lam,
                  subln=d_subln.reshape(1, LANES), name="attn_d")
    xl = _matmul([c_out, d_out], [w_o1], out_dtype=F32, tm=tm_lat, tn=_pick(dm, (512, 256, 128)),
                 epi="gres", res=xa, gate=g1, seq=seq, batch=bsz, rows=nl, name="out_proj1")

    rw = _pad_cols(router_w, LANES)
    rw_hi = rw.astype(BF16)
    rw_lo = (rw - rw_hi.astype(F32)).astype(BF16)
    h_packed, logits = _norm_mod(xl, norm_ffn1, sh2, sc2, rows=nl, seq=seq, batch=bsz, router=(rw_hi, rw_lo))
    out = _moe(h_packed, logits, xl, g2, final_norm, moe_w_gate, moe_w_up, moe_w_down, dims=dims)
    return out.reshape(bsz, seq, dm)


def _rank_norm_kernel(x_ref, g_ref, o_ref):
    o_ref[...] = _norm_rows(x_ref[...], g_ref[...]).astype(o_ref.dtype)


def _rank_norm(src, col0, width, gain):
    rows = src.shape[0]
    tm = _pick(rows, (256, 128, 64, 32, 16))
    assert col0 % width == 0
    vmem = 2 * (_nbytes((tm, width), F32) + _nbytes((tm, width), BF16))
    return pl.pallas_call(
        _rank_norm_kernel,
        grid=(rows // tm,),
        in_specs=[pl.BlockSpec((tm, width), lambda i: (i, col0 // width)),
                  pl.BlockSpec((1, width), lambda i: (0, 0))],
        out_specs=pl.BlockSpec((tm, width), lambda i: (i, 0)),
        out_shape=jax.ShapeDtypeStruct((rows, width), BF16),
        compiler_params=_params(("parallel",), vmem),
        name="rank_norm",
    )(src, gain.reshape(1, width))


_DIMS = Dims(d_model=4096, batch=4, seq=4096, ctx_len=256, grid_w=64, a_heads=16, a_kv_heads=4, b_heads=16,
             b_q_rank=1536, b_kv_rank=512, c_heads=16, c_kv_heads=4, window=128, d_heads=16, ffn_dim=11008,
             n_experts=8, expert_dim=3584)


def kernel(x, c, ctx, c_ctx, ada_w0, ada_b0, norm_mix0, norm_ffn0, w_in0, a_q_norm, a_k_norm, b_q_norm, b_kv_norm, w_uq, w_ukv, w_o0, ffn_w_gate, ffn_w_up, ffn_w_down, ada_w1, ada_b1, norm_mix1, norm_ffn1, w_in1, c_sink, d_lam_q1, d_lam_k1, d_lam_q2, d_lam_k2, d_subln, w_o1, router_w, moe_w_gate, moe_w_up, moe_w_down, final_norm):
    return _forward(_DIMS, x, c, ctx, c_ctx, ada_w0, ada_b0, norm_mix0, norm_ffn0, w_in0, a_q_norm, a_k_norm,
                    b_q_norm, b_kv_norm, w_uq, w_ukv, w_o0, ffn_w_gate, ffn_w_up, ffn_w_down, ada_w1, ada_b1,
                    norm_mix1, norm_ffn1, w_in1, c_sink, d_lam_q1, d_lam_k1, d_lam_q2, d_lam_k2, d_subln, w_o1,
                    router_w, moe_w_gate, moe_w_up, moe_w_down, final_norm)
```

```python
import functools
import math
from typing import NamedTuple

import jax
import jax.numpy as jnp
from jax import lax
from jax.experimental import pallas as pl
from jax.experimental.pallas import tpu as pltpu

F32 = jnp.float32
BF16 = jnp.bfloat16

LANES = 128
SUBLANES = 8
HEAD_DIM = 128
ROPE_THETA = 10000.0
NORM_EPS = 1e-6
MASK_VALUE = -1e30
LOG2E = math.log2(math.e)
N_MOD = 6
D_LAYER_INDEX = 1
D_LAMBDA_INIT = 0.8 - 0.6 * math.exp(-0.3 * D_LAYER_INDEX)
VMEM_HEADROOM = 6 << 20
DMA_LOOP_UNROLL = 8
HEAD_SUBBLOCK = 256


class Dims(NamedTuple):
    d_model: int
    batch: int
    seq: int
    ctx_len: int
    grid_w: int
    a_heads: int
    a_kv_heads: int
    b_heads: int
    b_q_rank: int
    b_kv_rank: int
    c_heads: int
    c_kv_heads: int
    window: int
    d_heads: int
    ffn_dim: int
    n_experts: int
    expert_dim: int


def _pick(n, prefs):
    for p in prefs:
        if n % p == 0:
            return p
    raise ValueError(f"no tile in {prefs} divides {n}")


def _params(sem, vmem_bytes):
    return pltpu.CompilerParams(dimension_semantics=sem, vmem_limit_bytes=int(vmem_bytes) + VMEM_HEADROOM)


def _nbytes(shape, dtype):
    return math.prod(shape) * jnp.dtype(dtype).itemsize


def _mod_kernel(c_ref, w_ref, b_ref, o_ref):
    c = c_ref[...]
    s = (c * jax.nn.sigmoid(c)).astype(BF16)
    o_ref[...] = jnp.dot(s, w_ref[...].astype(BF16), preferred_element_type=F32) + b_ref[...]


def _modulation(cond, ada_w, ada_b):
    rows, d = cond.shape
    n = ada_w.shape[1]
    tn = _pick(n, (512, 256, 128))
    vmem = 2 * (_nbytes((d, tn), F32) + _nbytes((rows, tn), F32) * 2) + _nbytes((rows, d), F32) * 2
    return pl.pallas_call(
        _mod_kernel,
        grid=(n // tn,),
        in_specs=[pl.BlockSpec((rows, d), lambda j: (0, 0)),
                  pl.BlockSpec((d, tn), lambda j: (0, j)),
                  pl.BlockSpec((1, tn), lambda j: (0, j))],
        out_specs=pl.BlockSpec((rows, tn), lambda j: (0, j)),
        out_shape=jax.ShapeDtypeStruct((rows, n), F32),
        compiler_params=_params(("parallel",), vmem),
        name="modulation",
    )(cond, ada_w, ada_b.reshape(1, n))


def _norm_rows(x, g):
    var = jnp.mean(x * x, axis=-1, keepdims=True)
    return x * lax.rsqrt(var + NORM_EPS) * g


def _norm_mod_kernel(x_ref, g_ref, sh_ref, sc_ref, o_ref):
    y = _norm_rows(x_ref[...], g_ref[...])
    o_ref[...] = (y * (1.0 + sc_ref[0]) + sh_ref[0]).astype(o_ref.dtype)


def _norm_mod_pair_kernel(lat_ref, ctx_ref, g_ref, sh_ref, sc_ref, o_ref, *, lat_tiles):
    x = jnp.where(pl.program_id(0) < lat_tiles, lat_ref[...], ctx_ref[...])
    y = _norm_rows(x, g_ref[...])
    o_ref[...] = (y * (1.0 + sc_ref[0]) + sh_ref[0]).astype(o_ref.dtype)


def _norm_mod_router_kernel(x_ref, g_ref, sh_ref, sc_ref, rhi_ref, rlo_ref, of_ref, lg_ref):
    y = _norm_rows(x_ref[...], g_ref[...])
    h = y * (1.0 + sc_ref[0]) + sh_ref[0]
    hi = h.astype(BF16)
    lo = (h - hi.astype(F32)).astype(BF16)
    half = h.shape[1] // 2
    left = lax.bitcast_convert_type(hi[:, :half].astype(F32), jnp.uint32)
    right = lax.bitcast_convert_type(hi[:, half:].astype(F32), jnp.uint32)
    of_ref[...] = lax.shift_right_logical(left, jnp.uint32(16)) | right
    lg_ref[...] = (jnp.dot(hi, rhi_ref[...], preferred_element_type=F32)
                   + jnp.dot(hi, rlo_ref[...], preferred_element_type=F32)
                   + jnp.dot(lo, rhi_ref[...], preferred_element_type=F32))


def _mod_index(tm, seq, batch):
    return lambda i: jnp.minimum((i * tm) // seq, batch)


def _norm_mod(x, g, shift, scale, *, rows, seq, batch, router=None):
    pair = isinstance(x, tuple)
    d = (x[0] if pair else x).shape[1]
    tm = _pick(math.gcd(rows, seq), (256, 128, 64, 32, 16))
    midx = _mod_index(tm, seq, batch)
    tail_specs = [pl.BlockSpec((1, d), lambda i: (0, 0)),
                  pl.BlockSpec((1, 1, d), lambda i: (midx(i), 0, 0)),
                  pl.BlockSpec((1, 1, d), lambda i: (midx(i), 0, 0))]
    vmem = 2 * (_nbytes((tm, d), F32) + _nbytes((tm, d), BF16)) + 6 * _nbytes((1, d), F32)
    if pair:
        lat_tiles = x[0].shape[0] // tm
        assert x[0].shape[0] % tm == 0 and x[1].shape[0] % tm == 0 and router is None
        x_specs = [pl.BlockSpec((tm, d), lambda i: (jnp.minimum(i, lat_tiles - 1), 0)),
                   pl.BlockSpec((tm, d), lambda i: (jnp.maximum(i - lat_tiles, 0), 0))]
        return pl.pallas_call(
            functools.partial(_norm_mod_pair_kernel, lat_tiles=lat_tiles),
            grid=(rows // tm,),
            in_specs=x_specs + tail_specs,
            out_specs=pl.BlockSpec((tm, d), lambda i: (i, 0)),
            out_shape=jax.ShapeDtypeStruct((rows, d), BF16),
            compiler_params=_params(("parallel",), vmem + 2 * _nbytes((tm, d), F32)),
            name="norm_modulate_inputs",
        )(*x, g.reshape(1, d), shift, scale)
    in_specs = [pl.BlockSpec((tm, d), lambda i: (i, 0))] + tail_specs
    if router is None:
        return pl.pallas_call(
            _norm_mod_kernel,
            grid=(rows // tm,),
            in_specs=in_specs,
            out_specs=pl.BlockSpec((tm, d), lambda i: (i, 0)),
            out_shape=jax.ShapeDtypeStruct((rows, d), BF16),
            compiler_params=_params(("parallel",), vmem),
            name="norm_modulate",
        )(x, g.reshape(1, d), shift, scale)
    rhi, rlo = router
    vmem += 2 * (_nbytes((tm, d), F32) + _nbytes((tm, LANES), F32)) + 4 * _nbytes((d, LANES), BF16)
    return pl.pallas_call(
        _norm_mod_router_kernel,
        grid=(rows // tm,),
        in_specs=in_specs + [pl.BlockSpec((d, LANES), lambda i: (0, 0)),
                             pl.BlockSpec((d, LANES), lambda i: (0, 0))],
        out_specs=[pl.BlockSpec((tm, d // 2), lambda i: (i, 0)),
                   pl.BlockSpec((tm, LANES), lambda i: (i, 0))],
        out_shape=[jax.ShapeDtypeStruct((rows, d // 2), jnp.uint32),
                   jax.ShapeDtypeStruct((rows, LANES), F32)],
        compiler_params=_params(("parallel",), vmem),
        name="norm_modulate_router",
    )(x, g.reshape(1, d), shift, scale, rhi, rlo)


def _head_op(x, gain, rope, rot_dim, scale):
    if gain is not None:
        x = _norm_rows(x, gain)
    if rot_dim:
        cos, up, down = rope
        half = rot_dim // 2
        x = x * cos + pltpu.roll(x, LANES - half, 1) * up + pltpu.roll(x, half, 1) * down
    return x if scale == 1.0 else x * scale


def _mm_kernel(*refs, n_w, nk, epi, x_parts=(False,), res_pair=False, lat_tiles=0, head_norm=False, rot_dim=0,
               scale=1.0):
    refs = list(refs)
    x_refs = []
    for has_ctx in x_parts:
        x_refs.append((refs.pop(0), refs.pop(0) if has_ctx else None))
    x_dtype = x_refs[0][0].dtype
    w_refs = refs[:n_w]

    def load_x():
        on_lat = pl.program_id(0) < lat_tiles
        parts = [lat[...] if ctx is None else jnp.where(on_lat, lat[...], ctx[...]) for lat, ctx in x_refs]
        return parts[0] if len(parts) == 1 else jnp.concatenate(parts, axis=-1)

    def weight(w_ref, cols=slice(None)):
        w = w_ref[:, cols]
        return w if w.dtype == x_dtype else w.astype(x_dtype)

    pos = n_w
    if epi == "gres":
        res_ref, res_ctx_ref = refs[pos], (refs[pos + 1] if res_pair else None)
        gate_ref = refs[pos + 1 + int(res_pair)]
        pos += 2 + int(res_pair)
    if epi == "heads":
        gain_ref = refs[pos] if head_norm else None
        pos += int(head_norm)
        rope_refs = refs[pos:pos + 3] if rot_dim else None
        pos += 3 if rot_dim else 0
    o_ref = refs[pos]
    acc_refs = refs[pos + 1:]
    k = pl.program_id(2)

    def epilogue(parts):
        if epi == "swiglu":
            g, u = parts
            out = g * jax.nn.sigmoid(g) * u
        elif epi == "gres":
            res = res_ref[...]
            if res_pair:
                res = jnp.where(pl.program_id(0) < lat_tiles, res, res_ctx_ref[...])
            out = res + gate_ref[0] * parts[0]
        else:
            out = parts[0]
        o_ref[...] = out.astype(o_ref.dtype)

    def compute_heads():
        x = load_x()
        gain = gain_ref[...] if head_norm else None
        rope = tuple(r[...] for r in rope_refs) if rot_dim else None
        tn = o_ref.shape[1]
        sub = HEAD_SUBBLOCK if tn % HEAD_SUBBLOCK == 0 else tn
        starts = list(range(0, tn, sub))
        dot = lambda c: jnp.dot(x, weight(w_refs[0], slice(c, c + sub)), preferred_element_type=F32)
        part = dot(starts[0])
        for idx, c in enumerate(starts):
            nxt = dot(starts[idx + 1]) if idx + 1 < len(starts) else None
            for hh in range(sub // LANES):
                cols = slice(hh * LANES, (hh + 1) * LANES)
                o_ref[:, c + hh * LANES:c + (hh + 1) * LANES] = _head_op(
                    part[:, cols], gain, rope, rot_dim, scale).astype(o_ref.dtype)
            part = nxt

    def compute():
        if epi == "heads":
            compute_heads()
            return
        x = load_x()
        parts = [jnp.dot(x, weight(w), preferred_element_type=F32) for w in w_refs]
        if nk == 1:
            epilogue(parts)
            return

        @pl.when(k == 0)
        def _():
            for a, p in zip(acc_refs, parts):
                a[...] = p

        @pl.when(k > 0)
        def _():
            for a, p in zip(acc_refs, parts):
                a[...] += p

        @pl.when(k == nk - 1)
        def _():
            epilogue([a[...] for a in acc_refs])

    compute()


def _matmul(x, ws, *, out_dtype, tm, tn, tk=None, epi="none", res=None, gate=None, seq=None, batch=None,
            rows=None, head=None, name="matmul"):
    parts = x if isinstance(x, list) else [x]
    parts = [p if isinstance(p, tuple) else (p, None) for p in parts]
    x_dtype = parts[0][0].dtype
    kdim = sum(lat.shape[1] for lat, _ in parts)
    lat_tiles = parts[0][0].shape[0] // tm
    m = parts[0][0].shape[0] + (0 if parts[0][1] is None else parts[0][1].shape[0])
    rows = m if rows is None else rows
    n = ws[0].shape[-1]
    tk = kdim if tk is None else tk
    nk = kdim // tk
    assert rows % tm == 0 and n % tn == 0 and kdim % tk == 0 and (len(parts) == 1 or nk == 1)
    n_w = len(ws)

    in_specs, args = [], []
    for lat, ctx in parts:
        width = tk if len(parts) == 1 else lat.shape[1]
        if ctx is None:
            in_specs.append(pl.BlockSpec((tm, width), lambda i, j, k: (i, k)))
            args.append(lat)
        else:
            in_specs += [pl.BlockSpec((tm, width), lambda i, j, k: (jnp.minimum(i, lat_tiles - 1), k)),
                         pl.BlockSpec((tm, width), lambda i, j, k: (jnp.maximum(i - lat_tiles, 0), k))]
            args += [lat, ctx]
    in_specs += [pl.BlockSpec((tk, tn), lambda i, j, k: (k, j))] * n_w
    args += list(ws)
    n_x = len(args) - n_w
    vmem = 2 * (n_x * _nbytes((tm, tk), x_dtype) // len(parts) + n_w * _nbytes((tk, tn), ws[0].dtype)
                + _nbytes((tm, tn), out_dtype))
    vmem += _nbytes((tm, tk), x_dtype) if n_x > 1 else 0
    vmem += 2 * n_w * _nbytes((tm, tn), F32)
    if ws[0].dtype != x_dtype:
        vmem += n_w * _nbytes((tk, tn), x_dtype)
    res_pair = isinstance(res, tuple)
    if epi == "gres":
        midx = _mod_index(tm, seq, batch)
        if res_pair:
            in_specs += [pl.BlockSpec((tm, tn), lambda i, j, k: (jnp.minimum(i, lat_tiles - 1), j)),
                         pl.BlockSpec((tm, tn), lambda i, j, k: (jnp.maximum(i - lat_tiles, 0), j))]
            args += list(res)
        else:
            in_specs.append(pl.BlockSpec((tm, tn), lambda i, j, k: (i, j)))
            args.append(res)
        in_specs.append(pl.BlockSpec((1, 1, tn), lambda i, j, k: (midx(i), 0, j)))
        args.append(gate)
        vmem += 2 * (1 + int(res_pair)) * _nbytes((tm, tn), F32)
    head_kw = {}
    if epi == "heads":
        gain, rope, rot_dim = head.get("gain"), head.get("rope"), head.get("rot_dim", 0)
        head_kw = dict(head_norm=gain is not None, rot_dim=rot_dim, scale=head.get("scale", 1.0))
        if gain is not None:
            in_specs.append(pl.BlockSpec((1, LANES), lambda i, j, k: (0, 0)))
            args.append(gain.reshape(1, LANES))
        if rot_dim:
            lat_tiles, seq_tiles = head["lat_rows"] // tm, seq // tm
            assert rope[0].shape[0] >= seq + tm
            tab = lambda i, j, k: (jnp.where(i < lat_tiles, i % seq_tiles, seq_tiles), 0)
            in_specs += [pl.BlockSpec((tm, LANES), tab)] * 3
            args += list(rope)
            vmem += 2 * 3 * _nbytes((tm, LANES), F32)
        vmem += 6 * _nbytes((tm, LANES), F32)
    scratch = [pltpu.VMEM((tm, tn), F32) for _ in range(n_w)] if nk > 1 else []
    vmem += len(scratch) * _nbytes((tm, tn), F32)
    return pl.pallas_call(
        functools.partial(_mm_kernel, n_w=n_w, nk=nk, epi=epi, x_parts=tuple(ctx is not None for _, ctx in parts),
                          res_pair=res_pair, lat_tiles=lat_tiles, **head_kw),
        grid=(rows // tm, n // tn, nk),
        in_specs=in_specs,
        out_specs=pl.BlockSpec((tm, tn), lambda i, j, k: (i, j)),
        out_shape=jax.ShapeDtypeStruct((rows, n), out_dtype),
        scratch_shapes=scratch,
        compiler_params=_params(("parallel", "parallel", "arbitrary"), vmem),
        name=name,
    )(*args)


def _rope_tables(seq, grid_w, rot_dim, pad_rows):
    n_rows = seq // grid_w
    rows = jnp.repeat(jnp.arange(n_rows, dtype=F32), grid_w)
    cols = jnp.tile(jnp.arange(grid_w, dtype=F32), n_rows)
    n_freq = rot_dim // 4
    inv_freq = jnp.power(ROPE_THETA, -jnp.arange(n_freq, dtype=F32) / n_freq)
    ang = jnp.concatenate([rows[:, None] * inv_freq, cols[:, None] * inv_freq], axis=-1)
    cos, sin = jnp.cos(ang), jnp.sin(ang)
    zero = jnp.zeros_like(sin)
    reps = LANES // rot_dim
    cos_t = jnp.tile(jnp.concatenate([cos, cos], axis=-1), (1, reps))
    up_t = jnp.tile(jnp.concatenate([-sin, zero], axis=-1), (1, reps))
    down_t = jnp.tile(jnp.concatenate([zero, sin], axis=-1), (1, reps))
    ident = jnp.ones((pad_rows, LANES), F32)
    zpad = jnp.zeros((pad_rows, LANES), F32)
    return (jnp.concatenate([cos_t, ident]), jnp.concatenate([up_t, zpad]), jnp.concatenate([down_t, zpad]))


def _values_t_kernel(x_ref, o_ref, *, n_heads):
    for h in range(n_heads):
        cols = slice(h * LANES, (h + 1) * LANES)
        o_ref[0, cols, :] = x_ref[:, cols].astype(F32).T.astype(o_ref.dtype)


def _values_t(src, col0, n_heads, *, chunk, row0, rows, name):
    hb = math.gcd(col0, n_heads)
    width = hb * LANES
    assert row0 % chunk == 0 and rows % chunk == 0
    vmem = 2 * (_nbytes((chunk, width), src.dtype) + _nbytes((chunk, width), BF16)) + 4 * _nbytes((chunk, LANES), F32)
    return pl.pallas_call(
        functools.partial(_values_t_kernel, n_heads=hb),
        grid=(rows // chunk, n_heads // hb),
        in_specs=[pl.BlockSpec((chunk, width), lambda i, j: (row0 // chunk + i, col0 // hb + j))],
        out_specs=pl.BlockSpec((1, width, chunk), lambda i, j: (i, j, 0)),
        out_shape=jax.ShapeDtypeStruct((rows // chunk, n_heads * LANES, chunk), BF16),
        compiler_params=_params(("parallel", "parallel"), vmem),
        name=name,
    )(src)


def _kv_chunk(seq):
    return _pick(seq // 2, (512, 256, 128))


def _qk(q, k):
    return lax.dot_general(q, k, (((1,), (1,)), ((), ())), preferred_element_type=F32)


def _cat(refs, rows=None):
    parts = [r[...] if rows is None else r[rows, :] for r in refs]
    return parts[0] if len(parts) == 1 else jnp.concatenate(parts, axis=-1)


def _flash_kernel(*refs, mode, n_q, has_lat, tq, tqs, tkc, seq, window):
    refs = list(refs)
    q_refs = [refs.pop(0) for _ in range(n_q)]
    kc_refs = [refs.pop(0) for _ in range(n_q)]
    vc_ref = refs.pop(0)
    if has_lat:
        kl_refs = [refs.pop(0) for _ in range(n_q)]
        vl_ref = refs.pop(0)
    if mode == "window":
        sink_ref = refs.pop(0)
    if mode == "diff":
        lam_ref, subln_ref = refs.pop(0), refs.pop(0)
    o_ref = refs.pop(0)
    n_state = 2 if mode == "diff" else 1
    n_sub = tq // tqs
    chain_ids = [(sub, st) for sub in range(n_sub) for st in range(n_state)]
    per_chain = 7
    state_refs = {cid: tuple(refs[per_chain * i:][:3]) for i, cid in enumerate(chain_ids)}
    score_refs = {cid: tuple(refs[per_chain * i + 3:][:2]) for i, cid in enumerate(chain_ids)}
    smax_refs = {cid: tuple(refs[per_chain * i + 5:][:2]) for i, cid in enumerate(chain_ids)}
    qi = pl.program_id(2)

    def scores(k, only_sub=None):
        out = {}
        for sub, st in chain_ids:
            if only_sub is not None and sub != only_sub:
                continue
            q = _cat(q_refs, pl.ds(sub * tqs, tqs))
            if mode == "diff":
                lane = lax.broadcasted_iota(jnp.int32, q.shape, 1)
                keep = (lane < LANES // 2) if st == 0 else (lane >= LANES // 2)
                q = jnp.where(keep, q, jnp.zeros_like(q))
            out[sub, st] = _qk(k, q)
        return out

    def absorb(s_of, vt, *, first=False, mask=None):
        for cid, s in s_of.items():
            m_ref, l_ref, acc_ref = state_refs[cid]
            if isinstance(s, tuple):
                s, s_max = s
            else:
                if mask is not None:
                    s = jnp.where(mask, s, MASK_VALUE)
                s_max = jnp.max(s, axis=0, keepdims=True)
            if first and mode == "window":
                m_old = jnp.broadcast_to(sink_ref[0][:, :1], (1, tqs))
                m_new = jnp.maximum(m_old, s_max)
                l_old = jnp.exp2(m_old - m_new)
            elif first:
                m_new, l_old = s_max, None
            else:
                m_old = m_ref[...]
                m_new = jnp.maximum(m_old, s_max)
                alpha = jnp.exp2(m_old - m_new)
                l_old = alpha * l_ref[...]
            p = jnp.exp2(s - m_new)
            l_new = jnp.sum(p, axis=0, keepdims=True)
            pv = jnp.dot(vt, p.astype(vt.dtype), preferred_element_type=F32)
            m_ref[...] = m_new
            l_ref[...] = l_new if l_old is None else l_old + l_new
            acc_ref[...] = pv if first else alpha * acc_ref[...] + pv

    def lat_keys(c):
        return _cat(kl_refs, pl.ds(pl.multiple_of(c * tkc, tkc), tkc))

    def stash(slot, s_of):
        for cid, s in s_of.items():
            score_refs[cid][slot][...] = s
            smax_refs[cid][slot][...] = jnp.max(s, axis=0, keepdims=True)

    def fetch(slot):
        return {cid: (score_refs[cid][slot][...], smax_refs[cid][slot][...]) for cid in chain_ids}

    s_ctx = scores(_cat(kc_refs))
    if has_lat and mode == "window":
        tkw = tqs + 2 * window
        blocks = []
        for sub in range(n_sub):
            q_pos0 = qi * tq + sub * tqs
            start = pl.multiple_of(jnp.clip(q_pos0 - window, 0, seq - tkw), LANES)
            kp = start + lax.broadcasted_iota(jnp.int32, (tkw, tqs), 0)
            qp = q_pos0 + lax.broadcasted_iota(jnp.int32, (tkw, tqs), 1)
            blocks.append((scores(_cat(kl_refs, pl.ds(start, tkw)), only_sub=sub), start // LANES,
                           jnp.abs(qp - kp) <= window))
        absorb(s_ctx, vc_ref[0], first=True)
        for s_of, first_chunk, mask in blocks:
            vt = jnp.concatenate([vl_ref[first_chunk + j] for j in range(tkw // LANES)], axis=-1)
            absorb(s_of, vt, mask=mask)
    elif has_lat:
        n_chunks = seq // tkc
        stash(0, scores(lat_keys(0)))
        absorb(s_ctx, vc_ref[0], first=True)

        def body(i, carry):
            c = 2 * i
            stash(1, scores(lat_keys(c + 1)))
            absorb(fetch(0), vl_ref[c])
            stash(0, scores(lat_keys(c + 2)))
            absorb(fetch(1), vl_ref[c + 1])
            return carry

        lax.fori_loop(0, n_chunks // 2 - 1, body, 0)
        stash(1, scores(lat_keys(n_chunks - 1)))
        absorb(fetch(0), vl_ref[n_chunks - 2])
        absorb(fetch(1), vl_ref[n_chunks - 1])
    else:
        absorb(s_ctx, vc_ref[0], first=True)

    for sub in range(n_sub):
        rows = pl.ds(sub * tqs, tqs)
        outs = [(state_refs[sub, st][2][...] / state_refs[sub, st][1][...]).T for st in range(n_state)]
        if mode == "diff":
            lam_rows = lam_ref[...]
            lam = (jnp.exp(jnp.sum(lam_rows[0:1] * lam_rows[1:2], axis=-1, keepdims=True))
                   - jnp.exp(jnp.sum(lam_rows[2:3] * lam_rows[3:4], axis=-1, keepdims=True)) + D_LAMBDA_INIT)
            out = _norm_rows(outs[0] - lam * outs[1], subln_ref[...]) * (1.0 - D_LAMBDA_INIT)
        else:
            out = outs[0]
        o_ref[rows, :] = out.astype(o_ref.dtype)


def _flash(q_parts, k_parts, v, *, n_heads, group, batch, seq, ctx_len, lat_rows, ctx_queries=False,
           mode="softmax", sink=None, lam=None, subln=None, window=0, name="flash"):
    has_lat = not ctx_queries
    ctx_blk0 = lat_rows // ctx_len
    if ctx_queries:
        tq, nq = ctx_len, 1
        q_row = lambda b, h, i: ctx_blk0 + b
        out_rows = batch * ctx_len
        o_row = lambda b, h, i: b
    else:
        tq = _pick(seq, (1024, 512, 256, 128))
        nq = seq // tq
        q_row = lambda b, h, i: b * nq + i
        out_rows = lat_rows
        o_row = q_row
    tqs = min(tq, 256)
    tkc = _kv_chunk(seq)
    n_state = 2 if mode == "diff" else 1
    if mode == "window":
        assert seq >= tqs + 2 * window and window % 16 == 0
    n_q = len(q_parts)
    in_specs, args = [], []
    vmem = 0

    def add(arr, block, imap):
        nonlocal vmem
        in_specs.append(pl.BlockSpec(block, imap))
        args.append(arr)
        vmem += 2 * _nbytes(block, arr.dtype)

    for arr, c0 in q_parts:
        add(arr, (tq, LANES), lambda b, h, i, c0=c0: (q_row(b, h, i), c0 + h))
    for arr, c0, per_head in k_parts:
        add(arr, (ctx_len, LANES), lambda b, h, i, c0=c0, ph=per_head: (ctx_blk0 + b, c0 + (h // group) * ph))
    vt_lat, vt_ctx = v
    add(vt_ctx, (1, LANES, ctx_len), lambda b, h, i: (b, h // group, 0))
    if has_lat:
        for arr, c0, per_head in k_parts:
            add(arr, (seq, LANES), lambda b, h, i, c0=c0, ph=per_head: (b, c0 + (h // group) * ph))
        tkv = vt_lat.shape[-1]
        assert tkv == (LANES if mode == "window" else tkc)
        add(vt_lat, (seq // tkv, LANES, tkv), lambda b, h, i: (b, h // group, 0))
    if mode == "window":
        add(sink, (1, 1, LANES), lambda b, h, i: (h, 0, 0))
    if mode == "diff":
        add(lam, (4, LANES), lambda b, h, i: (0, 0))
        add(subln, (1, LANES), lambda b, h, i: (0, 0))
    n_sub = tq // tqs
    scratch = [pltpu.VMEM((1, tqs), F32), pltpu.VMEM((1, tqs), F32), pltpu.VMEM((LANES, tqs), F32),
               pltpu.VMEM((tkc, tqs), F32), pltpu.VMEM((tkc, tqs), F32),
               pltpu.VMEM((1, tqs), F32), pltpu.VMEM((1, tqs), F32)] * (n_state * n_sub)
    vmem += 2 * n_state * n_sub * _nbytes((tkc, tqs), F32)
    vmem += 2 * _nbytes((tq, LANES), BF16) + 3 * _nbytes((n_state, tq, LANES), F32)
    vmem += 4 * n_state * (tq // tqs) * _nbytes((tqs, max(tkc, ctx_len, tqs + 2 * window)), F32)
    kern = functools.partial(_flash_kernel, mode=mode, n_q=n_q, has_lat=has_lat, tq=tq, tqs=tqs, tkc=tkc,
                             seq=seq, window=window)
    return pl.pallas_call(
        kern,
        grid=(batch, n_heads, nq),
        in_specs=in_specs,
        out_specs=pl.BlockSpec((tq, LANES), lambda b, h, i: (o_row(b, h, i), h)),
        out_shape=jax.ShapeDtypeStruct((out_rows, n_heads * LANES), BF16),
        scratch_shapes=scratch,
        compiler_params=_params(("parallel", "parallel", "parallel"), vmem),
        name=name,
    )(*args)


def _route_kernel(lg_ref, info_ref, cnt_ref, base_ref, *, n_experts, tb):
    step = pl.program_id(0)

    @pl.when(step == 0)
    def _():
        base_ref[...] = jnp.zeros_like(base_ref)

    lane = lax.broadcasted_iota(jnp.int32, (tb, LANES), 1).astype(F32)
    logits = jnp.where(lane < n_experts, lg_ref[...], -jnp.inf)
    v1 = jnp.max(logits, axis=-1, keepdims=True)
    i1 = jnp.min(jnp.where(logits == v1, lane, float(LANES)), axis=-1, keepdims=True)
    hot1 = lane == i1
    rest = jnp.where(hot1, -jnp.inf, logits)
    v2 = jnp.max(rest, axis=-1, keepdims=True)
    i2 = jnp.min(jnp.where(rest == v2, lane, float(LANES)), axis=-1, keepdims=True)
    hot2 = lane == i2
    e = jnp.exp(v2 - v1)
    w1 = 1.0 / (1.0 + e)
    w2 = e / (1.0 + e)
    sel = jnp.where(hot1 | hot2, 1.0, 0.0)
    r = lax.broadcasted_iota(jnp.int32, (tb, tb), 0)
    c = lax.broadcasted_iota(jnp.int32, (tb, tb), 1)
    tri = jnp.where(c < r, 1.0, 0.0).astype(BF16)
    before = jnp.dot(tri, sel.astype(BF16), preferred_element_type=F32) + base_ref[...]
    rank1 = jnp.sum(jnp.where(hot1, before, 0.0), axis=-1, keepdims=True)
    rank2 = jnp.sum(jnp.where(hot2, before, 0.0), axis=-1, keepdims=True)
    base_ref[...] += jnp.sum(sel, axis=0, keepdims=True)
    cnt_ref[...] = jnp.broadcast_to(base_ref[...], cnt_ref.shape)
    info = jnp.zeros((tb, LANES), F32)
    for slot, val in enumerate((i1, i2, rank1, rank2, w1, w2)):
        info = jnp.where(lane == float(slot), val, info)
    info_ref[...] = info


def _route(logits, n_experts):
    n = logits.shape[0]
    tb = _pick(n, (256, 128, 64, 32, 16, 8))
    vmem = 4 * _nbytes((tb, LANES), F32) + 16 * _nbytes((tb, max(tb, LANES)), F32)
    return pl.pallas_call(
        functools.partial(_route_kernel, n_experts=n_experts, tb=tb),
        grid=(n // tb,),
        in_specs=[pl.BlockSpec((tb, LANES), lambda i: (i, 0))],
        out_specs=[pl.BlockSpec((tb, LANES), lambda i: (i, 0)),
                   pl.BlockSpec((SUBLANES, LANES), lambda i: (0, 0))],
        out_shape=[jax.ShapeDtypeStruct((n, LANES), F32), jax.ShapeDtypeStruct((SUBLANES, LANES), F32)],
        scratch_shapes=[pltpu.VMEM((1, LANES), F32)],
        compiler_params=_params(("arbitrary",), vmem),
        name="moe_route",
    )(logits)


def _row_copy(src_hbm, row, dst_vmem, slot, sem):
    return pltpu.make_async_copy(src_hbm.at[pl.ds(row, 1)], dst_vmem.at[pl.ds(slot, 1)], sem)


def _gather_kernel(src_ref, h_hbm, o_ref, buf, sem, *, tm, n_tiles):
    i = pl.program_id(0)

    def issue_tile(tile, slot):
        def issue(r, carry):
            _row_copy(h_hbm, src_ref[tile * tm + r], buf.at[slot], r, sem.at[slot]).start()
            return carry

        lax.fori_loop(0, tm, issue, 0, unroll=DMA_LOOP_UNROLL)

    @pl.when(i == 0)
    def _():
        issue_tile(0, 0)

    @pl.when(i + 1 < n_tiles)
    def _():
        issue_tile(i + 1, (i + 1) % 2)

    slot = i % 2

    def drain(r, carry):
        _row_copy(h_hbm, 0, buf.at[slot], r, sem.at[slot]).wait()
        return carry

    lax.fori_loop(0, tm, drain, 0, unroll=DMA_LOOP_UNROLL)
    packed = buf[slot]
    half = packed.shape[1]
    left = lax.bitcast_convert_type(lax.shift_left(packed, jnp.uint32(16)), F32)
    right = lax.bitcast_convert_type(packed & jnp.uint32(0xFFFF0000), F32)
    o_ref[:, :half] = left.astype(o_ref.dtype)
    o_ref[:, half:] = right.astype(o_ref.dtype)


def _gather_rows(src_rows, h, *, tm):
    p = src_rows.shape[0]
    d = 2 * h.shape[1]
    vmem = 4 * _nbytes((tm, d // 2), jnp.uint32) + 2 * _nbytes((tm, d), BF16)
    gs = pltpu.PrefetchScalarGridSpec(
        num_scalar_prefetch=1, grid=(p // tm,),
        in_specs=[pl.BlockSpec(memory_space=pl.ANY)],
        out_specs=pl.BlockSpec((tm, d), lambda i, src: (i, 0)),
        scratch_shapes=[pltpu.VMEM((2, tm, d // 2), jnp.uint32), pltpu.SemaphoreType.DMA((2,))])
    return pl.pallas_call(
        functools.partial(_gather_kernel, tm=tm, n_tiles=p // tm), grid_spec=gs,
        out_shape=jax.ShapeDtypeStruct((p, d), BF16),
        compiler_params=_params(("arbitrary",), vmem),
        name="moe_gather",
    )(src_rows, h)


def _combine_kernel(d1_ref, d2_ref, y_hbm, x_ref, info_ref, gate_ref, g_ref, o_ref, buf1, buf2, sem, *, tb):
    base = pl.program_id(0) * tb

    def issue(r, carry):
        _row_copy(y_hbm, d1_ref[base + r], buf1, r, sem.at[0]).start()
        _row_copy(y_hbm, d2_ref[base + r], buf2, r, sem.at[1]).start()
        return carry

    lax.fori_loop(0, tb, issue, 0, unroll=DMA_LOOP_UNROLL)

    def drain(r, carry):
        _row_copy(y_hbm, 0, buf1, r, sem.at[0]).wait()
        _row_copy(y_hbm, 0, buf2, r, sem.at[1]).wait()
        return carry

    lax.fori_loop(0, tb, drain, 0, unroll=DMA_LOOP_UNROLL)
    info = info_ref[...]
    w1, w2 = info[:, 4:5], info[:, 5:6]
    x = x_ref[...] + gate_ref[0] * (w1 * buf1[...] + w2 * buf2[...])
    o_ref[...] = _norm_rows(x, g_ref[...])


def _combine(dest1, dest2, y, x, info, gate, final_norm, *, seq, batch):
    n, d = x.shape
    tb = _pick(seq, (256, 128, 64, 32, 16, 8))
    midx = _mod_index(tb, seq, batch)
    vmem = 2 * _nbytes((tb, d), F32) * 3 + 2 * _nbytes((tb, d), F32) + 2 * _nbytes((tb, LANES), F32)
    gs = pltpu.PrefetchScalarGridSpec(
        num_scalar_prefetch=2, grid=(n // tb,),
        in_specs=[pl.BlockSpec(memory_space=pl.ANY),
                  pl.BlockSpec((tb, d), lambda i, a, b: (i, 0)),
                  pl.BlockSpec((tb, LANES), lambda i, a, b: (i, 0)),
                  pl.BlockSpec((1, 1, d), lambda i, a, b: (midx(i), 0, 0)),
                  pl.BlockSpec((1, d), lambda i, a, b: (0, 0))],
        out_specs=pl.BlockSpec((tb, d), lambda i, a, b: (i, 0)),
        scratch_shapes=[pltpu.VMEM((tb, d), F32), pltpu.VMEM((tb, d), F32), pltpu.SemaphoreType.DMA((2,))])
    return pl.pallas_call(
        functools.partial(_combine_kernel, tb=tb), grid_spec=gs,
        out_shape=jax.ShapeDtypeStruct((n, d), F32),
        compiler_params=_params(("arbitrary",), vmem),
        name="moe_combine_norm",
    )(dest1, dest2, y, x, info, gate, final_norm.reshape(1, d))


def _expert_mm_kernel(te_ref, used_ref, x_ref, *refs, n_w, swiglu):
    w_refs, o_ref, wbf_refs = refs[:n_w], refs[n_w], refs[n_w + 1:]
    i = pl.program_id(1)
    live = i < used_ref[0]
    fresh = jnp.logical_or(i == 0, te_ref[i] != te_ref[jnp.maximum(i - 1, 0)])

    @pl.when(jnp.logical_and(live, fresh))
    def _():
        for w, wb in zip(w_refs, wbf_refs):
            wb[...] = w[0].astype(wb.dtype)

    @pl.when(live)
    def _():
        x = x_ref[...]
        parts = [jnp.dot(x, wb[...], preferred_element_type=F32) for wb in wbf_refs]
        out = parts[0] * jax.nn.sigmoid(parts[0]) * parts[1] if swiglu else parts[0]
        o_ref[...] = out.astype(o_ref.dtype)

    @pl.when(jnp.logical_not(live))
    def _():
        o_ref[...] = jnp.zeros_like(o_ref)


def _expert_matmul(x, ws, tile_expert, tiles_used, *, out_dtype, tm, tn, swiglu=False, name):
    rows, kdim = x.shape
    n = ws[0].shape[-1]
    n_w = len(ws)
    assert rows % tm == 0 and n % tn == 0
    vmem = 2 * (_nbytes((tm, kdim), x.dtype) + n_w * _nbytes((kdim, tn), ws[0].dtype) + _nbytes((tm, tn), out_dtype))
    vmem += n_w * (_nbytes((kdim, tn), BF16) + 2 * _nbytes((tm, tn), F32))
    gs = pltpu.PrefetchScalarGridSpec(
        num_scalar_prefetch=2, grid=(n // tn, rows // tm),
        in_specs=[pl.BlockSpec((tm, kdim), lambda j, i, te, nu: (i, 0))]
        + [pl.BlockSpec((1, kdim, tn), lambda j, i, te, nu: (te[i], 0, j))] * n_w,
        out_specs=pl.BlockSpec((tm, tn), lambda j, i, te, nu: (i, j)),
        scratch_shapes=[pltpu.VMEM((kdim, tn), BF16)] * n_w)
    return pl.pallas_call(
        functools.partial(_expert_mm_kernel, n_w=n_w, swiglu=swiglu), grid_spec=gs,
        out_shape=jax.ShapeDtypeStruct((rows, n), out_dtype),
        compiler_params=_params(("arbitrary", "arbitrary"), vmem),
        name=name,
    )(tile_expert, tiles_used, x, *ws)


def _moe(h_packed, logits, x, gate, final_norm, w_gate, w_up, w_down, *, dims):
    n, d = x.shape
    n_exp = dims.n_experts
    tm = _pick(n, (512, 256, 128, 64, 32, 16))
    p = 2 * n + n_exp * tm
    info, counts = _route(logits, n_exp)
    cnt = counts[0, :n_exp].astype(jnp.int32)
    padded = ((cnt + tm - 1) // tm) * tm
    ends = jnp.cumsum(padded)
    starts = ends - padded
    e1, e2 = info[:, 0].astype(jnp.int32), info[:, 1].astype(jnp.int32)
    dest1 = starts[e1] + info[:, 2].astype(jnp.int32)
    dest2 = starts[e2] + info[:, 3].astype(jnp.int32)
    token = jnp.arange(n, dtype=jnp.int32)
    src = jnp.zeros((p,), jnp.int32).at[dest1].set(token).at[dest2].set(token)
    tile_start = jnp.arange(p // tm, dtype=jnp.int32) * tm
    tile_expert = jnp.minimum(jnp.sum(tile_start[:, None] >= ends[None, :], axis=1), n_exp - 1).astype(jnp.int32)
    tiles_used = (ends[-1:] // tm).astype(jnp.int32)

    xs = _gather_rows(src, h_packed, tm=_pick(tm, (256, 128, 64, 32, 16)))
    f = w_gate.shape[-1]
    act = _expert_matmul(xs, [w_gate, w_up], tile_expert, tiles_used, out_dtype=BF16, tm=tm,
                         tn=_pick(f, (512, 256, 128)), swiglu=True, name="moe_gate_up")
    y = _expert_matmul(act, [w_down], tile_expert, tiles_used, out_dtype=F32, tm=tm,
                       tn=_pick(d, (1024, 512, 256, 128)), name="moe_down")
    return _combine(dest1, dest2, y, x, info, gate, final_norm, seq=dims.seq, batch=dims.batch)


def _pad_cols(w, n):
    return w if w.shape[-1] == n else jnp.pad(w, [(0, 0)] * (w.ndim - 1) + [(0, n - w.shape[-1])])


def _forward(dims, x, c, ctx, c_ctx,
             ada_w0, ada_b0, norm_mix0, norm_ffn0, w_in0, a_q_norm, a_k_norm, b_q_norm, b_kv_norm,
             w_uq, w_ukv, w_o0, ffn_w_gate, ffn_w_up, ffn_w_down,
             ada_w1, ada_b1, norm_mix1, norm_ffn1, w_in1, c_sink, d_lam_q1, d_lam_k1, d_lam_q2, d_lam_k2,
             d_subln, w_o1, router_w, moe_w_gate, moe_w_up, moe_w_down, final_norm):
    dm, bsz, seq, ctx_len = dims.d_model, dims.batch, dims.seq, dims.ctx_len
    nl, nc = bsz * seq, bsz * ctx_len
    nt = nl + nc
    ha, hka, hb = dims.a_heads, dims.a_kv_heads, dims.b_heads
    hc, hkc, hd = dims.c_heads, dims.c_kv_heads, dims.d_heads
    big = lambda n: _pick(n, (1024, 512, 256, 128, 64, 32, 16))
    tm_all, tm_lat = big(math.gcd(nt, seq)), big(math.gcd(nl, seq))

    x_rows, ctx_rows = x.reshape(nl, dm), ctx.reshape(nc, dm)
    n_cond = -(-(bsz + 1) // SUBLANES) * SUBLANES
    cond = jnp.concatenate([c, c_ctx[None, :], jnp.zeros((n_cond - bsz - 1, dm), F32)], axis=0)

    def mods(ada_w, ada_b):
        m = _modulation(cond, ada_w, ada_b).reshape(n_cond, N_MOD, 1, dm)
        return [m[:, i] for i in range(N_MOD)]

    rope128 = _rope_tables(seq, dims.grid_w, 128, tm_all)
    rope64 = _rope_tables(seq, dims.grid_w, 64, tm_all)
    flash = functools.partial(_flash, batch=bsz, seq=seq, ctx_len=ctx_len, lat_rows=nl)

    def project(src, w, *, rows=nt, out_dtype=BF16, name, **head):
        n = w.shape[1]
        tm = tm_all if rows == nt else tm_lat
        if head:
            return _matmul(src, [w.astype(BF16)], out_dtype=out_dtype, tm=tm, tn=_pick(n, (1024, 512, 256, 128)),
                           rows=rows, epi="heads", head=dict(head, lat_rows=nl), seq=seq, name=name)
        return _matmul(src, [w.astype(BF16)], out_dtype=out_dtype, tm=tm, tn=_pick(n, (512, 256, 128)),
                       rows=rows, name=name)

    def values_t(src, col0, n_heads, name, chunk=_kv_chunk(seq)):
        return (_values_t(src, col0, n_heads, chunk=chunk, row0=0, rows=nl, name=name),
                _values_t(src, col0, n_heads, chunk=ctx_len, row0=nl, rows=nc, name=name + "_ctx"))

    sh1, sc1, g1, sh2, sc2, g2 = mods(ada_w0, ada_b0)
    h = _norm_mod((x_rows, ctx_rows), norm_mix0, sh1, sc1, rows=nt, seq=seq, batch=bsz)
    na_q, na_kv, rq, rkv = ha * HEAD_DIM, hka * HEAD_DIM, dims.b_q_rank, dims.b_kv_rank
    c_ak, c_av, c_cq, c_ckv, c_kr = na_q, na_q + na_kv, na_q + 2 * na_kv, na_q + 2 * na_kv + rq, na_q + 2 * na_kv + rq + rkv
    aq = project(h, w_in0[:, :c_ak], gain=a_q_norm, rope=rope128, rot_dim=128, scale=LOG2E * HEAD_DIM ** -0.5,
                 name="proj_aq")
    ak = project(h, w_in0[:, c_ak:c_av], gain=a_k_norm, rope=rope128, rot_dim=128, name="proj_ak")
    bk_rope = project(h, _pad_cols(w_in0[:, c_kr:], LANES), rope=rope64, rot_dim=64, name="proj_bk_rope")
    rest = project(h, jnp.concatenate([w_in0[:, c_cq:c_kr], w_in0[:, c_av:c_cq]], axis=1), out_dtype=F32,
                   name="proj_ranks_av")
    av = values_t(rest, (rq + rkv) // HEAD_DIM, hka, "vt_a")
    cq = _rank_norm(rest, 0, rq, b_q_norm)
    ckv = _rank_norm(rest, rq, rkv, b_kv_norm)
    b_scale = LOG2E * (HEAD_DIM + 64) ** -0.5
    uq = w_uq.reshape(rq, hb, HEAD_DIM + 64) * b_scale
    bq_nope = project(cq, uq[:, :, :HEAD_DIM].reshape(rq, hb * HEAD_DIM), name="b_up_q_nope")
    bq_rope = project(cq, _pad_cols(uq[:, :, HEAD_DIM:], LANES).reshape(rq, hb * LANES), rope=rope64, rot_dim=64,
                      name="b_up_q_rope")
    ukv = w_ukv.reshape(rkv, hb, 2 * HEAD_DIM)
    ukv = jnp.concatenate([ukv[:, :, :HEAD_DIM].reshape(rkv, hb * HEAD_DIM),
                           ukv[:, :, HEAD_DIM:].reshape(rkv, hb * HEAD_DIM)], axis=1)
    bkv = project(ckv, ukv, name="b_up_kv")

    a_lat = flash([(aq, 0)], [(ak, 0, 1)], av, n_heads=ha, group=ha // hka, name="attn_a")
    a_ctx = flash([(aq, 0)], [(ak, 0, 1)], av, n_heads=ha, group=ha // hka, ctx_queries=True, name="attn_a_ctx")
    b_args = ([(bq_nope, 0), (bq_rope, 0)], [(bkv, 0, 1), (bk_rope, 0, 0)], values_t(bkv, hb, hb, "vt_b"))
    b_lat = flash(*b_args, n_heads=hb, group=1, name="attn_b")
    b_ctx = flash(*b_args, n_heads=hb, group=1, ctx_queries=True, name="attn_b_ctx")
    xa = _matmul([(a_lat, a_ctx), (b_lat, b_ctx)], [w_o0], out_dtype=F32, tm=tm_all, tn=_pick(dm, (512, 256, 128)),
                 epi="gres", res=(x_rows, ctx_rows), gate=g1, seq=seq, batch=bsz, name="out_proj0")

    h = _norm_mod(xa, norm_ffn0, sh2, sc2, rows=nt, seq=seq, batch=bsz)
    ffn = dims.ffn_dim
    act = _matmul(h, [ffn_w_gate, ffn_w_up], out_dtype=BF16, tm=tm_all, tn=_pick(ffn, (256, 128)), epi="swiglu",
                  name="ffn_gate_up")
    xa = _matmul(act, [ffn_w_down.astype(BF16)], out_dtype=F32, tm=tm_all // 2 if ffn > 4096 else tm_all,
                 tn=_pick(dm, (512, 256, 128)), epi="gres", res=xa, gate=g2, seq=seq, batch=bsz, name="ffn_down")

    sh1, sc1, g1, sh2, sc2, g2 = mods(ada_w1, ada_b1)
    h = _norm_mod(xa, norm_mix1, sh1, sc1, rows=nt, seq=seq, batch=bsz)
    bounds = [0]
    for n_heads in (hc, hkc, hkc, hd, hd, hd):
        bounds.append(bounds[-1] + n_heads * HEAD_DIM)
    w_cq, w_ck, w_cv, w_dq, w_dk, w_dv = (w_in1[:, lo:hi] for lo, hi in zip(bounds[:-1], bounds[1:]))
    cq_ = project(h, w_cq, rows=nl, rope=rope128, rot_dim=128, scale=LOG2E * HEAD_DIM ** -0.5, name="proj_cq")
    ck_ = project(h, w_ck, rope=rope128, rot_dim=128, name="proj_ck")
    dq_ = project(h, w_dq, rows=nl, rope=rope64, rot_dim=64, scale=LOG2E * 64 ** -0.5, name="proj_dq")
    dk_ = project(h, w_dk, rope=rope64, rot_dim=64, name="proj_dk")
    values = project(h, jnp.concatenate([w_cv, w_dv], axis=1), name="proj_cv_dv")
    cv_ = values_t(values, 0, hkc, "vt_c", chunk=LANES)
    dv_ = values_t(values, hkc, hd, "vt_d")
    sink = jnp.broadcast_to(LOG2E * c_sink.astype(F32)[:, None, None], (hc, 1, LANES))
    lam = jnp.stack([_pad_cols(v.astype(F32)[None, :], LANES)[0] for v in (d_lam_q1, d_lam_k1, d_lam_q2, d_lam_k2)])
    c_out = flash([(cq_, 0)], [(ck_, 0, 1)], cv_, n_heads=hc, group=hc // hkc, mode="window", sink=sink,
                  window=dims.window, name="attn_c")
    d_out = flash([(dq_, 0)], [(dk_, 0, 1)], dv_, n_heads=hd, group=1, mode="diff", lam=lam,
                  subln=d_subln.reshape(1, LANES), name="attn_d")
    xl = _matmul([c_out, d_out], [w_o1], out_dtype=F32, tm=tm_lat, tn=_pick(dm, (512, 256, 128)),
                 epi="gres", res=xa, gate=g1, seq=seq, batch=bsz, rows=nl, name="out_proj1")

    rw = _pad_cols(router_w, LANES)
    rw_hi = rw.astype(BF16)
    rw_lo = (rw - rw_hi.astype(F32)).astype(BF16)
    h_packed, logits = _norm_mod(xl, norm_ffn1, sh2, sc2, rows=nl, seq=seq, batch=bsz, router=(rw_hi, rw_lo))
    out = _moe(h_packed, logits, xl, g2, final_norm, moe_w_gate, moe_w_up, moe_w_down, dims=dims)
    return out.reshape(bsz, seq, dm)


def _rank_norm_kernel(x_ref, g_ref, o_ref):
    o_ref[...] = _norm_rows(x_ref[...], g_ref[...]).astype(o_ref.dtype)


def _rank_norm(src, col0, width, gain):
    rows = src.shape[0]
    tm = _pick(rows, (256, 128, 64, 32, 16))
    assert col0 % width == 0
    vmem = 2 * (_nbytes((tm, width), F32) + _nbytes((tm, width), BF16))
    return pl.pallas_call(
        _rank_norm_kernel,
        grid=(rows // tm,),
        in_specs=[pl.BlockSpec((tm, width), lambda i: (i, col0 // width)),
                  pl.BlockSpec((1, width), lambda i: (0, 0))],
        out_specs=pl.BlockSpec((tm, width), lambda i: (i, 0)),
        out_shape=jax.ShapeDtypeStruct((rows, width), BF16),
        compiler_params=_params(("parallel",), vmem),
        name="rank_norm",
    )(src, gain.reshape(1, width))


_DIMS = Dims(d_model=4096, batch=4, seq=4096, ctx_len=256, grid_w=64, a_heads=16, a_kv_heads=4, b_heads=16,
             b_q_rank=1536, b_kv_rank=512, c_heads=16, c_kv_heads=4, window=128, d_heads=16, ffn_dim=11008,
             n_experts=8, expert_dim=3584)


def kernel(x, c, ctx, c_ctx, ada_w0, ada_b0, norm_mix0, norm_ffn0, w_in0, a_q_norm, a_k_norm, b_q_norm, b_kv_norm, w_uq, w_ukv, w_o0, ffn_w_gate, ffn_w_up, ffn_w_down, ada_w1, ada_b1, norm_mix1, norm_ffn1, w_in1, c_sink, d_lam_q1, d_lam_k1, d_lam_q2, d_lam_k2, d_subln, w_o1, router_w, moe_w_gate, moe_w_up, moe_w_down, final_norm):
    return _forward(_DIMS, x, c, ctx, c_ctx, ada_w0, ada_b0, norm_mix0, norm_ffn0, w_in0, a_q_norm, a_k_norm,
                    b_q_norm, b_kv_norm, w_uq, w_ukv, w_o0, ffn_w_gate, ffn_w_up, ffn_w_down, ada_w1, ada_b1,
                    norm_mix1, norm_ffn1, w_in1, c_sink, d_lam_q1, d_lam_k1, d_lam_q2, d_lam_k2, d_subln, w_o1,
                    router_w, moe_w_gate, moe_w_up, moe_w_down, final_norm)
```

```python
import functools
import math
from typing import NamedTuple

import jax
import jax.numpy as jnp
from jax import lax
from jax.experimental import pallas as pl
from jax.experimental.pallas import tpu as pltpu

F32 = jnp.float32
BF16 = jnp.bfloat16

LANES = 128
SUBLANES = 8
HEAD_DIM = 128
ROPE_THETA = 10000.0
NORM_EPS = 1e-6
MASK_VALUE = -1e30
LOG2E = math.log2(math.e)
N_MOD = 6
D_LAYER_INDEX = 1
D_LAMBDA_INIT = 0.8 - 0.6 * math.exp(-0.3 * D_LAYER_INDEX)
VMEM_HEADROOM = 6 << 20
DMA_LOOP_UNROLL = 8
HEAD_SUBBLOCK = 256


class Dims(NamedTuple):
    d_model: int
    batch: int
    seq: int
    ctx_len: int
    grid_w: int
    a_heads: int
    a_kv_heads: int
    b_heads: int
    b_q_rank: int
    b_kv_rank: int
    c_heads: int
    c_kv_heads: int
    window: int
    d_heads: int
    ffn_dim: int
    n_experts: int
    expert_dim: int


def _pick(n, prefs):
    for p in prefs:
        if n % p == 0:
            return p
    raise ValueError(f"no tile in {prefs} divides {n}")


def _params(sem, vmem_bytes):
    return pltpu.CompilerParams(dimension_semantics=sem, vmem_limit_bytes=int(vmem_bytes) + VMEM_HEADROOM)


def _nbytes(shape, dtype):
    return math.prod(shape) * jnp.dtype(dtype).itemsize


def _mod_kernel(c_ref, w_ref, b_ref, o_ref):
    c = c_ref[...]
    s = (c * jax.nn.sigmoid(c)).astype(BF16)
    o_ref[...] = jnp.dot(s, w_ref[...].astype(BF16), preferred_element_type=F32) + b_ref[...]


def _modulation(cond, ada_w, ada_b):
    rows, d = cond.shape
    n = ada_w.shape[1]
    tn = _pick(n, (512, 256, 128))
    vmem = 2 * (_nbytes((d, tn), F32) + _nbytes((rows, tn), F32) * 2) + _nbytes((rows, d), F32) * 2
    return pl.pallas_call(
        _mod_kernel,
        grid=(n // tn,),
        in_specs=[pl.BlockSpec((rows, d), lambda j: (0, 0)),
                  pl.BlockSpec((d, tn), lambda j: (0, j)),
                  pl.BlockSpec((1, tn), lambda j: (0, j))],
        out_specs=pl.BlockSpec((rows, tn), lambda j: (0, j)),
        out_shape=jax.ShapeDtypeStruct((rows, n), F32),
        compiler_params=_params(("parallel",), vmem),
        name="modulation",
    )(cond, ada_w, ada_b.reshape(1, n))


def _norm_rows(x, g):
    var = jnp.mean(x * x, axis=-1, keepdims=True)
    return x * lax.rsqrt(var + NORM_EPS) * g


def _pack_bf16_pairs(x):
    n = x.shape[1] // 2
    bits = lambda v: lax.bitcast_convert_type(v.astype(BF16).astype(F32), jnp.uint32)
    return lax.shift_right_logical(bits(x[:, :n]), jnp.uint32(16)) | bits(x[:, n:])


def _unpack_bf16_pairs(words):
    left = lax.bitcast_convert_type(lax.shift_left(words, jnp.uint32(16)), F32)
    right = lax.bitcast_convert_type(words & jnp.uint32(0xFFFF0000), F32)
    return left, right


def _norm_mod_kernel(x_ref, g_ref, sh_ref, sc_ref, o_ref):
    y = _norm_rows(x_ref[...], g_ref[...])
    o_ref[...] = (y * (1.0 + sc_ref[0]) + sh_ref[0]).astype(o_ref.dtype)


def _norm_mod_pair_kernel(lat_ref, ctx_ref, g_ref, sh_ref, sc_ref, o_ref, *, lat_tiles):
    x = jnp.where(pl.program_id(0) < lat_tiles, lat_ref[...], ctx_ref[...])
    y = _norm_rows(x, g_ref[...])
    o_ref[...] = (y * (1.0 + sc_ref[0]) + sh_ref[0]).astype(o_ref.dtype)


def _norm_mod_router_kernel(x_ref, g_ref, sh_ref, sc_ref, rhi_ref, rlo_ref, of_ref, lg_ref):
    y = _norm_rows(x_ref[...], g_ref[...])
    h = y * (1.0 + sc_ref[0]) + sh_ref[0]
    hi = h.astype(BF16)
    lo = (h - hi.astype(F32)).astype(BF16)
    of_ref[...] = _pack_bf16_pairs(h)
    lg_ref[...] = (jnp.dot(hi, rhi_ref[...], preferred_element_type=F32)
                   + jnp.dot(hi, rlo_ref[...], preferred_element_type=F32)
                   + jnp.dot(lo, rhi_ref[...], preferred_element_type=F32))


def _mod_index(tm, seq, batch):
    return lambda i: jnp.minimum((i * tm) // seq, batch)


def _norm_mod(x, g, shift, scale, *, rows, seq, batch, router=None):
    pair = isinstance(x, tuple)
    d = (x[0] if pair else x).shape[1]
    tm = _pick(math.gcd(rows, seq), (256, 128, 64, 32, 16))
    midx = _mod_index(tm, seq, batch)
    tail_specs = [pl.BlockSpec((1, d), lambda i: (0, 0)),
                  pl.BlockSpec((1, 1, d), lambda i: (midx(i), 0, 0)),
                  pl.BlockSpec((1, 1, d), lambda i: (midx(i), 0, 0))]
    vmem = 2 * (_nbytes((tm, d), F32) + _nbytes((tm, d), BF16)) + 6 * _nbytes((1, d), F32)
    if pair:
        lat_tiles = x[0].shape[0] // tm
        assert x[0].shape[0] % tm == 0 and x[1].shape[0] % tm == 0 and router is None
        x_specs = [pl.BlockSpec((tm, d), lambda i: (jnp.minimum(i, lat_tiles - 1), 0)),
                   pl.BlockSpec((tm, d), lambda i: (jnp.maximum(i - lat_tiles, 0), 0))]
        return pl.pallas_call(
            functools.partial(_norm_mod_pair_kernel, lat_tiles=lat_tiles),
            grid=(rows // tm,),
            in_specs=x_specs + tail_specs,
            out_specs=pl.BlockSpec((tm, d), lambda i: (i, 0)),
            out_shape=jax.ShapeDtypeStruct((rows, d), BF16),
            compiler_params=_params(("parallel",), vmem + 2 * _nbytes((tm, d), F32)),
            name="norm_modulate_inputs",
        )(*x, g.reshape(1, d), shift, scale)
    in_specs = [pl.BlockSpec((tm, d), lambda i: (i, 0))] + tail_specs
    if router is None:
        return pl.pallas_call(
            _norm_mod_kernel,
            grid=(rows // tm,),
            in_specs=in_specs,
            out_specs=pl.BlockSpec((tm, d), lambda i: (i, 0)),
            out_shape=jax.ShapeDtypeStruct((rows, d), BF16),
            compiler_params=_params(("parallel",), vmem),
            name="norm_modulate",
        )(x, g.reshape(1, d), shift, scale)
    rhi, rlo = router
    vmem += 2 * (_nbytes((tm, d), F32) + _nbytes((tm, LANES), F32)) + 4 * _nbytes((d, LANES), BF16)
    return pl.pallas_call(
        _norm_mod_router_kernel,
        grid=(rows // tm,),
        in_specs=in_specs + [pl.BlockSpec((d, LANES), lambda i: (0, 0)),
                             pl.BlockSpec((d, LANES), lambda i: (0, 0))],
        out_specs=[pl.BlockSpec((tm, d // 2), lambda i: (i, 0)),
                   pl.BlockSpec((tm, LANES), lambda i: (i, 0))],
        out_shape=[jax.ShapeDtypeStruct((rows, d // 2), jnp.uint32),
                   jax.ShapeDtypeStruct((rows, LANES), F32)],
        compiler_params=_params(("parallel",), vmem),
        name="norm_modulate_router",
    )(x, g.reshape(1, d), shift, scale, rhi, rlo)


def _head_op(x, gain, rope, rot_dim, scale):
    if gain is not None:
        x = _norm_rows(x, gain)
    if rot_dim:
        cos, up, down = rope
        half = rot_dim // 2
        x = x * cos + pltpu.roll(x, LANES - half, 1) * up + pltpu.roll(x, half, 1) * down
    return x if scale == 1.0 else x * scale


def _mm_kernel(*refs, n_w, nk, epi, x_parts=(False,), res_pair=False, lat_tiles=0, head_norm=False, rot_dim=0,
               scale=1.0):
    refs = list(refs)
    x_refs = []
    for has_ctx in x_parts:
        x_refs.append((refs.pop(0), refs.pop(0) if has_ctx else None))
    x_dtype = x_refs[0][0].dtype
    w_refs = refs[:n_w]

    def load_x():
        on_lat = pl.program_id(0) < lat_tiles
        parts = [lat[...] if ctx is None else jnp.where(on_lat, lat[...], ctx[...]) for lat, ctx in x_refs]
        return parts[0] if len(parts) == 1 else jnp.concatenate(parts, axis=-1)

    def weight(w_ref, cols=slice(None)):
        w = w_ref[:, cols]
        return w if w.dtype == x_dtype else w.astype(x_dtype)

    pos = n_w
    if epi == "gres":
        res_ref, res_ctx_ref = refs[pos], (refs[pos + 1] if res_pair else None)
        gate_ref = refs[pos + 1 + int(res_pair)]
        pos += 2 + int(res_pair)
    if epi == "heads":
        gain_ref = refs[pos] if head_norm else None
        pos += int(head_norm)
        rope_refs = refs[pos:pos + 3] if rot_dim else None
        pos += 3 if rot_dim else 0
    o_ref = refs[pos]
    acc_refs = refs[pos + 1:]
    k = pl.program_id(2)

    def epilogue(parts):
        if epi == "swiglu":
            g, u = parts
            out = g * jax.nn.sigmoid(g) * u
        elif epi == "gres":
            res = res_ref[...]
            if res_pair:
                res = jnp.where(pl.program_id(0) < lat_tiles, res, res_ctx_ref[...])
            out = res + gate_ref[0] * parts[0]
        else:
            out = parts[0]
        o_ref[...] = out.astype(o_ref.dtype)

    def compute_heads():
        x = load_x()
        gain = gain_ref[...] if head_norm else None
        rope = tuple(r[...] for r in rope_refs) if rot_dim else None
        tn = o_ref.shape[1]
        sub = HEAD_SUBBLOCK if tn % HEAD_SUBBLOCK == 0 else tn
        starts = list(range(0, tn, sub))
        dot = lambda c: jnp.dot(x, weight(w_refs[0], slice(c, c + sub)), preferred_element_type=F32)
        part = dot(starts[0])
        for idx, c in enumerate(starts):
            nxt = dot(starts[idx + 1]) if idx + 1 < len(starts) else None
            for hh in range(sub // LANES):
                cols = slice(hh * LANES, (hh + 1) * LANES)
                o_ref[:, c + hh * LANES:c + (hh + 1) * LANES] = _head_op(
                    part[:, cols], gain, rope, rot_dim, scale).astype(o_ref.dtype)
            part = nxt

    def compute():
        if epi == "heads":
            compute_heads()
            return
        x = load_x()
        parts = [jnp.dot(x, weight(w), preferred_element_type=F32) for w in w_refs]
        if nk == 1:
            epilogue(parts)
            return

        @pl.when(k == 0)
        def _():
            for a, p in zip(acc_refs, parts):
                a[...] = p

        @pl.when(k > 0)
        def _():
            for a, p in zip(acc_refs, parts):
                a[...] += p

        @pl.when(k == nk - 1)
        def _():
            epilogue([a[...] for a in acc_refs])

    compute()


def _matmul(x, ws, *, out_dtype, tm, tn, tk=None, epi="none", res=None, gate=None, seq=None, batch=None,
            rows=None, head=None, name="matmul"):
    parts = x if isinstance(x, list) else [x]
    parts = [p if isinstance(p, tuple) else (p, None) for p in parts]
    x_dtype = parts[0][0].dtype
    kdim = sum(lat.shape[1] for lat, _ in parts)
    lat_tiles = parts[0][0].shape[0] // tm
    m = parts[0][0].shape[0] + (0 if parts[0][1] is None else parts[0][1].shape[0])
    rows = m if rows is None else rows
    n = ws[0].shape[-1]
    tk = kdim if tk is None else tk
    nk = kdim // tk
    assert rows % tm == 0 and n % tn == 0 and kdim % tk == 0 and (len(parts) == 1 or nk == 1)
    n_w = len(ws)

    in_specs, args = [], []
    for lat, ctx in parts:
        width = tk if len(parts) == 1 else lat.shape[1]
        if ctx is None:
            in_specs.append(pl.BlockSpec((tm, width), lambda i, j, k: (i, k)))
            args.append(lat)
        else:
            in_specs += [pl.BlockSpec((tm, width), lambda i, j, k: (jnp.minimum(i, lat_tiles - 1), k)),
                         pl.BlockSpec((tm, width), lambda i, j, k: (jnp.maximum(i - lat_tiles, 0), k))]
            args += [lat, ctx]
    in_specs += [pl.BlockSpec((tk, tn), lambda i, j, k: (k, j))] * n_w
    args += list(ws)
    n_x = len(args) - n_w
    vmem = 2 * (n_x * _nbytes((tm, tk), x_dtype) // len(parts) + n_w * _nbytes((tk, tn), ws[0].dtype)
                + _nbytes((tm, tn), out_dtype))
    vmem += _nbytes((tm, tk), x_dtype) if n_x > 1 else 0
    vmem += 2 * n_w * _nbytes((tm, tn), F32)
    if ws[0].dtype != x_dtype:
        vmem += n_w * _nbytes((tk, tn), x_dtype)
    res_pair = isinstance(res, tuple)
    if epi == "gres":
        midx = _mod_index(tm, seq, batch)
        if res_pair:
            in_specs += [pl.BlockSpec((tm, tn), lambda i, j, k: (jnp.minimum(i, lat_tiles - 1), j)),
                         pl.BlockSpec((tm, tn), lambda i, j, k: (jnp.maximum(i - lat_tiles, 0), j))]
            args += list(res)
        else:
            in_specs.append(pl.BlockSpec((tm, tn), lambda i, j, k: (i, j)))
            args.append(res)
        in_specs.append(pl.BlockSpec((1, 1, tn), lambda i, j, k: (midx(i), 0, j)))
        args.append(gate)
        vmem += 2 * (1 + int(res_pair)) * _nbytes((tm, tn), F32)
    head_kw = {}
    if epi == "heads":
        gain, rope, rot_dim = head.get("gain"), head.get("rope"), head.get("rot_dim", 0)
        head_kw = dict(head_norm=gain is not None, rot_dim=rot_dim, scale=head.get("scale", 1.0))
        if gain is not None:
            in_specs.append(pl.BlockSpec((1, LANES), lambda i, j, k: (0, 0)))
            args.append(gain.reshape(1, LANES))
        if rot_dim:
            lat_tiles, seq_tiles = head["lat_rows"] // tm, seq // tm
            assert rope[0].shape[0] >= seq + tm
            tab = lambda i, j, k: (jnp.where(i < lat_tiles, i % seq_tiles, seq_tiles), 0)
            in_specs += [pl.BlockSpec((tm, LANES), tab)] * 3
            args += list(rope)
            vmem += 2 * 3 * _nbytes((tm, LANES), F32)
        vmem += 6 * _nbytes((tm, LANES), F32)
    scratch = [pltpu.VMEM((tm, tn), F32) for _ in range(n_w)] if nk > 1 else []
    vmem += len(scratch) * _nbytes((tm, tn), F32)
    return pl.pallas_call(
        functools.partial(_mm_kernel, n_w=n_w, nk=nk, epi=epi, x_parts=tuple(ctx is not None for _, ctx in parts),
                          res_pair=res_pair, lat_tiles=lat_tiles, **head_kw),
        grid=(rows // tm, n // tn, nk),
        in_specs=in_specs,
        out_specs=pl.BlockSpec((tm, tn), lambda i, j, k: (i, j)),
        out_shape=jax.ShapeDtypeStruct((rows, n), out_dtype),
        scratch_shapes=scratch,
        compiler_params=_params(("parallel", "parallel", "arbitrary"), vmem),
        name=name,
    )(*args)


def _rope_tables(seq, grid_w, rot_dim, pad_rows):
    n_rows = seq // grid_w
    rows = jnp.repeat(jnp.arange(n_rows, dtype=F32), grid_w)
    cols = jnp.tile(jnp.arange(grid_w, dtype=F32), n_rows)
    n_freq = rot_dim // 4
    inv_freq = jnp.power(ROPE_THETA, -jnp.arange(n_freq, dtype=F32) / n_freq)
    ang = jnp.concatenate([rows[:, None] * inv_freq, cols[:, None] * inv_freq], axis=-1)
    cos, sin = jnp.cos(ang), jnp.sin(ang)
    zero = jnp.zeros_like(sin)
    reps = LANES // rot_dim
    cos_t = jnp.tile(jnp.concatenate([cos, cos], axis=-1), (1, reps))
    up_t = jnp.tile(jnp.concatenate([-sin, zero], axis=-1), (1, reps))
    down_t = jnp.tile(jnp.concatenate([zero, sin], axis=-1), (1, reps))
    ident = jnp.ones((pad_rows, LANES), F32)
    zpad = jnp.zeros((pad_rows, LANES), F32)
    return (jnp.concatenate([cos_t, ident]), jnp.concatenate([up_t, zpad]), jnp.concatenate([down_t, zpad]))


def _values_t_kernel(x_ref, o_ref, *, n_heads):
    for h in range(n_heads):
        cols = slice(h * LANES, (h + 1) * LANES)
        o_ref[0, cols, :] = x_ref[:, cols].astype(F32).T.astype(o_ref.dtype)


def _values_t(src, col0, n_heads, *, chunk, row0, rows, name):
    hb = math.gcd(col0, n_heads)
    width = hb * LANES
    assert row0 % chunk == 0 and rows % chunk == 0
    vmem = 2 * (_nbytes((chunk, width), src.dtype) + _nbytes((chunk, width), BF16)) + 4 * _nbytes((chunk, LANES), F32)
    return pl.pallas_call(
        functools.partial(_values_t_kernel, n_heads=hb),
        grid=(rows // chunk, n_heads // hb),
        in_specs=[pl.BlockSpec((chunk, width), lambda i, j: (row0 // chunk + i, col0 // hb + j))],
        out_specs=pl.BlockSpec((1, width, chunk), lambda i, j: (i, j, 0)),
        out_shape=jax.ShapeDtypeStruct((rows // chunk, n_heads * LANES, chunk), BF16),
        compiler_params=_params(("parallel", "parallel"), vmem),
        name=name,
    )(src)


def _kv_chunk(seq):
    return _pick(seq // 2, (512, 256, 128))


def _qk(q, k):
    return lax.dot_general(q, k, (((1,), (1,)), ((), ())), preferred_element_type=F32)


def _cat(refs, rows=None):
    parts = [r[...] if rows is None else r[rows, :] for r in refs]
    return parts[0] if len(parts) == 1 else jnp.concatenate(parts, axis=-1)


def _flash_kernel(*refs, mode, n_q, has_lat, tq, tqs, tkc, seq, window):
    refs = list(refs)
    q_refs = [refs.pop(0) for _ in range(n_q)]
    kc_refs = [refs.pop(0) for _ in range(n_q)]
    vc_ref = refs.pop(0)
    if has_lat:
        kl_refs = [refs.pop(0) for _ in range(n_q)]
        vl_ref = refs.pop(0)
    if mode == "window":
        sink_ref = refs.pop(0)
    if mode == "diff":
        lam_ref, subln_ref = refs.pop(0), refs.pop(0)
    o_ref = refs.pop(0)
    n_state = 2 if mode == "diff" else 1
    n_sub = tq // tqs
    chain_ids = [(sub, st) for sub in range(n_sub) for st in range(n_state)]
    per_chain = 7
    state_refs = {cid: tuple(refs[per_chain * i:][:3]) for i, cid in enumerate(chain_ids)}
    score_refs = {cid: tuple(refs[per_chain * i + 3:][:2]) for i, cid in enumerate(chain_ids)}
    smax_refs = {cid: tuple(refs[per_chain * i + 5:][:2]) for i, cid in enumerate(chain_ids)}
    qi = pl.program_id(2)

    def scores(k, only_sub=None):
        out = {}
        for sub, st in chain_ids:
            if only_sub is not None and sub != only_sub:
                continue
            q = _cat(q_refs, pl.ds(sub * tqs, tqs))
            if mode == "diff":
                lane = lax.broadcasted_iota(jnp.int32, q.shape, 1)
                keep = (lane < LANES // 2) if st == 0 else (lane >= LANES // 2)
                q = jnp.where(keep, q, jnp.zeros_like(q))
            out[sub, st] = _qk(k, q)
        return out

    def absorb(s_of, vt, *, first=False, mask=None):
        for cid, s in s_of.items():
            m_ref, l_ref, acc_ref = state_refs[cid]
            if isinstance(s, tuple):
                s, s_max = s
            else:
                if mask is not None:
                    s = jnp.where(mask, s, MASK_VALUE)
                s_max = jnp.max(s, axis=0, keepdims=True)
            if first and mode == "window":
                m_old = jnp.broadcast_to(sink_ref[0][:, :1], (1, tqs))
                m_new = jnp.maximum(m_old, s_max)
                l_old = jnp.exp2(m_old - m_new)
            elif first:
                m_new, l_old = s_max, None
            else:
                m_old = m_ref[...]
                m_new = jnp.maximum(m_old, s_max)
                alpha = jnp.exp2(m_old - m_new)
                l_old = alpha * l_ref[...]
            p = jnp.exp2(s - m_new)
            l_new = jnp.sum(p, axis=0, keepdims=True)
            pv = jnp.dot(vt, p.astype(vt.dtype), preferred_element_type=F32)
            m_ref[...] = m_new
            l_ref[...] = l_new if l_old is None else l_old + l_new
            acc_ref[...] = pv if first else alpha * acc_ref[...] + pv

    def lat_keys(c):
        return _cat(kl_refs, pl.ds(pl.multiple_of(c * tkc, tkc), tkc))

    def stash(slot, s_of):
        for cid, s in s_of.items():
            score_refs[cid][slot][...] = s
            smax_refs[cid][slot][...] = jnp.max(s, axis=0, keepdims=True)

    def fetch(slot):
        return {cid: (score_refs[cid][slot][...], smax_refs[cid][slot][...]) for cid in chain_ids}

    s_ctx = scores(_cat(kc_refs))
    if has_lat and mode == "window":
        tkw = tqs + 2 * window
        blocks = []
        for sub in range(n_sub):
            q_pos0 = qi * tq + sub * tqs
            start = pl.multiple_of(jnp.clip(q_pos0 - window, 0, seq - tkw), LANES)
            kp = start + lax.broadcasted_iota(jnp.int32, (tkw, tqs), 0)
            qp = q_pos0 + lax.broadcasted_iota(jnp.int32, (tkw, tqs), 1)
            blocks.append((scores(_cat(kl_refs, pl.ds(start, tkw)), only_sub=sub), start // LANES,
                           jnp.abs(qp - kp) <= window))
        absorb(s_ctx, vc_ref[0], first=True)
        for s_of, first_chunk, mask in blocks:
            vt = jnp.concatenate([vl_ref[first_chunk + j] for j in range(tkw // LANES)], axis=-1)
            absorb(s_of, vt, mask=mask)
    elif has_lat:
        n_chunks = seq // tkc
        stash(0, scores(lat_keys(0)))
        absorb(s_ctx, vc_ref[0], first=True)

        def body(i, carry):
            c = 2 * i
            stash(1, scores(lat_keys(c + 1)))
            absorb(fetch(0), vl_ref[c])
            stash(0, scores(lat_keys(c + 2)))
            absorb(fetch(1), vl_ref[c + 1])
            return carry

        lax.fori_loop(0, n_chunks // 2 - 1, body, 0)
        stash(1, scores(lat_keys(n_chunks - 1)))
        absorb(fetch(0), vl_ref[n_chunks - 2])
        absorb(fetch(1), vl_ref[n_chunks - 1])
    else:
        absorb(s_ctx, vc_ref[0], first=True)

    for sub in range(n_sub):
        rows = pl.ds(sub * tqs, tqs)
        outs = [(state_refs[sub, st][2][...] / state_refs[sub, st][1][...]).T for st in range(n_state)]
        if mode == "diff":
            lam_rows = lam_ref[...]
            lam = (jnp.exp(jnp.sum(lam_rows[0:1] * lam_rows[1:2], axis=-1, keepdims=True))
                   - jnp.exp(jnp.sum(lam_rows[2:3] * lam_rows[3:4], axis=-1, keepdims=True)) + D_LAMBDA_INIT)
            out = _norm_rows(outs[0] - lam * outs[1], subln_ref[...]) * (1.0 - D_LAMBDA_INIT)
        else:
            out = outs[0]
        o_ref[rows, :] = out.astype(o_ref.dtype)


def _flash(q_parts, k_parts, v, *, n_heads, group, batch, seq, ctx_len, lat_rows, ctx_queries=False,
           mode="softmax", sink=None, lam=None, subln=None, window=0, name="flash"):
    has_lat = not ctx_queries
    ctx_blk0 = lat_rows // ctx_len
    if ctx_queries:
        tq, nq = ctx_len, 1
        q_row = lambda b, h, i: ctx_blk0 + b
        out_rows = batch * ctx_len
        o_row = lambda b, h, i: b
    else:
        tq = _pick(seq, (1024, 512, 256, 128))
        nq = seq // tq
        q_row = lambda b, h, i: b * nq + i
        out_rows = lat_rows
        o_row = q_row
    tqs = min(tq, 256)
    tkc = _kv_chunk(seq)
    n_state = 2 if mode == "diff" else 1
    if mode == "window":
        assert seq >= tqs + 2 * window and window % 16 == 0
    n_q = len(q_parts)
    in_specs, args = [], []
    vmem = 0

    def add(arr, block, imap):
        nonlocal vmem
        in_specs.append(pl.BlockSpec(block, imap))
        args.append(arr)
        vmem += 2 * _nbytes(block, arr.dtype)

    for arr, c0 in q_parts:
        add(arr, (tq, LANES), lambda b, h, i, c0=c0: (q_row(b, h, i), c0 + h))
    for arr, c0, per_head in k_parts:
        add(arr, (ctx_len, LANES), lambda b, h, i, c0=c0, ph=per_head: (ctx_blk0 + b, c0 + (h // group) * ph))
    vt_lat, vt_ctx = v
    add(vt_ctx, (1, LANES, ctx_len), lambda b, h, i: (b, h // group, 0))
    if has_lat:
        for arr, c0, per_head in k_parts:
            add(arr, (seq, LANES), lambda b, h, i, c0=c0, ph=per_head: (b, c0 + (h // group) * ph))
        tkv = vt_lat.shape[-1]
        assert tkv == (LANES if mode == "window" else tkc)
        add(vt_lat, (seq // tkv, LANES, tkv), lambda b, h, i: (b, h // group, 0))
    if mode == "window":
        add(sink, (1, 1, LANES), lambda b, h, i: (h, 0, 0))
    if mode == "diff":
        add(lam, (4, LANES), lambda b, h, i: (0, 0))
        add(subln, (1, LANES), lambda b, h, i: (0, 0))
    n_sub = tq // tqs
    scratch = [pltpu.VMEM((1, tqs), F32), pltpu.VMEM((1, tqs), F32), pltpu.VMEM((LANES, tqs), F32),
               pltpu.VMEM((tkc, tqs), F32), pltpu.VMEM((tkc, tqs), F32),
               pltpu.VMEM((1, tqs), F32), pltpu.VMEM((1, tqs), F32)] * (n_state * n_sub)
    vmem += 2 * n_state * n_sub * _nbytes((tkc, tqs), F32)
    vmem += 2 * _nbytes((tq, LANES), BF16) + 3 * _nbytes((n_state, tq, LANES), F32)
    vmem += 4 * n_state * (tq // tqs) * _nbytes((tqs, max(tkc, ctx_len, tqs + 2 * window)), F32)
    kern = functools.partial(_flash_kernel, mode=mode, n_q=n_q, has_lat=has_lat, tq=tq, tqs=tqs, tkc=tkc,
                             seq=seq, window=window)
    return pl.pallas_call(
        kern,
        grid=(batch, n_heads, nq),
        in_specs=in_specs,
        out_specs=pl.BlockSpec((tq, LANES), lambda b, h, i: (o_row(b, h, i), h)),
        out_shape=jax.ShapeDtypeStruct((out_rows, n_heads * LANES), BF16),
        scratch_shapes=scratch,
        compiler_params=_params(("parallel", "parallel", "parallel"), vmem),
        name=name,
    )(*args)


def _route_kernel(lg_ref, info_ref, cnt_ref, base_ref, *, n_experts, tb):
    step = pl.program_id(0)

    @pl.when(step == 0)
    def _():
        base_ref[...] = jnp.zeros_like(base_ref)

    lane = lax.broadcasted_iota(jnp.int32, (tb, LANES), 1).astype(F32)
    logits = jnp.where(lane < n_experts, lg_ref[...], -jnp.inf)
    v1 = jnp.max(logits, axis=-1, keepdims=True)
    i1 = jnp.min(jnp.where(logits == v1, lane, float(LANES)), axis=-1, keepdims=True)
    hot1 = lane == i1
    rest = jnp.where(hot1, -jnp.inf, logits)
    v2 = jnp.max(rest, axis=-1, keepdims=True)
    i2 = jnp.min(jnp.where(rest == v2, lane, float(LANES)), axis=-1, keepdims=True)
    hot2 = lane == i2
    e = jnp.exp(v2 - v1)
    w1 = 1.0 / (1.0 + e)
    w2 = e / (1.0 + e)
    sel = jnp.where(hot1 | hot2, 1.0, 0.0)
    r = lax.broadcasted_iota(jnp.int32, (tb, tb), 0)
    c = lax.broadcasted_iota(jnp.int32, (tb, tb), 1)
    tri = jnp.where(c < r, 1.0, 0.0).astype(BF16)
    before = jnp.dot(tri, sel.astype(BF16), preferred_element_type=F32) + base_ref[...]
    rank1 = jnp.sum(jnp.where(hot1, before, 0.0), axis=-1, keepdims=True)
    rank2 = jnp.sum(jnp.where(hot2, before, 0.0), axis=-1, keepdims=True)
    base_ref[...] += jnp.sum(sel, axis=0, keepdims=True)
    cnt_ref[...] = jnp.broadcast_to(base_ref[...], cnt_ref.shape)
    info = jnp.zeros((tb, LANES), F32)
    for slot, val in enumerate((i1, i2, rank1, rank2, w1, w2)):
        info = jnp.where(lane == float(slot), val, info)
    info_ref[...] = info


def _route(logits, n_experts):
    n = logits.shape[0]
    tb = _pick(n, (256, 128, 64, 32, 16, 8))
    vmem = 4 * _nbytes((tb, LANES), F32) + 16 * _nbytes((tb, max(tb, LANES)), F32)
    return pl.pallas_call(
        functools.partial(_route_kernel, n_experts=n_experts, tb=tb),
        grid=(n // tb,),
        in_specs=[pl.BlockSpec((tb, LANES), lambda i: (i, 0))],
        out_specs=[pl.BlockSpec((tb, LANES), lambda i: (i, 0)),
                   pl.BlockSpec((SUBLANES, LANES), lambda i: (0, 0))],
        out_shape=[jax.ShapeDtypeStruct((n, LANES), F32), jax.ShapeDtypeStruct((SUBLANES, LANES), F32)],
        scratch_shapes=[pltpu.VMEM((1, LANES), F32)],
        compiler_params=_params(("arbitrary",), vmem),
        name="moe_route",
    )(logits)


def _row_copy(src_hbm, row, dst_vmem, slot, sem):
    return pltpu.make_async_copy(src_hbm.at[pl.ds(row, 1)], dst_vmem.at[pl.ds(slot, 1)], sem)


def _gather_kernel(src_ref, h_hbm, o_ref, buf, sem, *, tm, n_tiles):
    i = pl.program_id(0)

    def issue_tile(tile, slot):
        def issue(r, carry):
            _row_copy(h_hbm, src_ref[tile * tm + r], buf.at[slot], r, sem.at[slot]).start()
            return carry

        lax.fori_loop(0, tm, issue, 0, unroll=DMA_LOOP_UNROLL)

    @pl.when(i == 0)
    def _():
        issue_tile(0, 0)

    @pl.when(i + 1 < n_tiles)
    def _():
        issue_tile(i + 1, (i + 1) % 2)

    slot = i % 2

    def drain(r, carry):
        _row_copy(h_hbm, 0, buf.at[slot], r, sem.at[slot]).wait()
        return carry

    lax.fori_loop(0, tm, drain, 0, unroll=DMA_LOOP_UNROLL)
    left, right = _unpack_bf16_pairs(buf[slot])
    half = left.shape[1]
    o_ref[:, :half] = left.astype(o_ref.dtype)
    o_ref[:, half:] = right.astype(o_ref.dtype)


def _gather_rows(src_rows, h, *, tm):
    p = src_rows.shape[0]
    d = 2 * h.shape[1]
    vmem = 4 * _nbytes((tm, d // 2), jnp.uint32) + 2 * _nbytes((tm, d), BF16)
    gs = pltpu.PrefetchScalarGridSpec(
        num_scalar_prefetch=1, grid=(p // tm,),
        in_specs=[pl.BlockSpec(memory_space=pl.ANY)],
        out_specs=pl.BlockSpec((tm, d), lambda i, src: (i, 0)),
        scratch_shapes=[pltpu.VMEM((2, tm, d // 2), jnp.uint32), pltpu.SemaphoreType.DMA((2,))])
    return pl.pallas_call(
        functools.partial(_gather_kernel, tm=tm, n_tiles=p // tm), grid_spec=gs,
        out_shape=jax.ShapeDtypeStruct((p, d), BF16),
        compiler_params=_params(("arbitrary",), vmem),
        name="moe_gather",
    )(src_rows, h)


def _combine_kernel(d1_ref, d2_ref, y_hbm, x_ref, info_ref, gate_ref, g_ref, o_ref, buf1, buf2, sem, *, tb,
                    pair_block):
    base = pl.program_id(0) * tb

    def issue(r, carry):
        _row_copy(y_hbm, d1_ref[base + r], buf1, r, sem.at[0]).start()
        _row_copy(y_hbm, d2_ref[base + r], buf2, r, sem.at[1]).start()
        return carry

    lax.fori_loop(0, tb, issue, 0, unroll=DMA_LOOP_UNROLL)

    def drain(r, carry):
        _row_copy(y_hbm, 0, buf1, r, sem.at[0]).wait()
        _row_copy(y_hbm, 0, buf2, r, sem.at[1]).wait()
        return carry

    lax.fori_loop(0, tb, drain, 0, unroll=DMA_LOOP_UNROLL)
    def rows_of(buf):
        words = pair_block // 2
        pieces = []
        for c in range(0, buf.shape[1], words):
            pieces += list(_unpack_bf16_pairs(buf[:, c:c + words]))
        return jnp.concatenate(pieces, axis=-1)

    info = info_ref[...]
    w1, w2 = info[:, 4:5], info[:, 5:6]
    x = x_ref[...] + gate_ref[0] * (w1 * rows_of(buf1) + w2 * rows_of(buf2))
    o_ref[...] = _norm_rows(x, g_ref[...])


def _combine(dest1, dest2, y, x, info, gate, final_norm, *, seq, batch, pair_block):
    n, d = x.shape
    tb = _pick(seq, (256, 128, 64, 32, 16, 8))
    midx = _mod_index(tb, seq, batch)
    vmem = 2 * _nbytes((tb, d), F32) * 3 + 2 * _nbytes((tb, d), F32) + 2 * _nbytes((tb, LANES), F32)
    gs = pltpu.PrefetchScalarGridSpec(
        num_scalar_prefetch=2, grid=(n // tb,),
        in_specs=[pl.BlockSpec(memory_space=pl.ANY),
                  pl.BlockSpec((tb, d), lambda i, a, b: (i, 0)),
                  pl.BlockSpec((tb, LANES), lambda i, a, b: (i, 0)),
                  pl.BlockSpec((1, 1, d), lambda i, a, b: (midx(i), 0, 0)),
                  pl.BlockSpec((1, d), lambda i, a, b: (0, 0))],
        out_specs=pl.BlockSpec((tb, d), lambda i, a, b: (i, 0)),
        scratch_shapes=[pltpu.VMEM((tb, d // 2), jnp.uint32), pltpu.VMEM((tb, d // 2), jnp.uint32),
                        pltpu.SemaphoreType.DMA((2,))])
    return pl.pallas_call(
        functools.partial(_combine_kernel, tb=tb, pair_block=pair_block), grid_spec=gs,
        out_shape=jax.ShapeDtypeStruct((n, d), F32),
        compiler_params=_params(("arbitrary",), vmem),
        name="moe_combine_norm",
    )(dest1, dest2, y, x, info, gate, final_norm.reshape(1, d))


def _expert_mm_kernel(te_ref, used_ref, x_ref, *refs, n_w, swiglu):
    w_refs, o_ref, wbf_refs = refs[:n_w], refs[n_w], refs[n_w + 1:]
    i = pl.program_id(1)
    live = i < used_ref[0]
    fresh = jnp.logical_or(i == 0, te_ref[i] != te_ref[jnp.maximum(i - 1, 0)])

    @pl.when(jnp.logical_and(live, fresh))
    def _():
        for w, wb in zip(w_refs, wbf_refs):
            wb[...] = w[0].astype(wb.dtype)

    @pl.when(live)
    def _():
        x = x_ref[...]
        parts = [jnp.dot(x, wb[...], preferred_element_type=F32) for wb in wbf_refs]
        out = parts[0] * jax.nn.sigmoid(parts[0]) * parts[1] if swiglu else parts[0]
        o_ref[...] = _pack_bf16_pairs(out) if o_ref.dtype == jnp.uint32 else out.astype(o_ref.dtype)

    @pl.when(jnp.logical_not(live))
    def _():
        o_ref[...] = jnp.zeros_like(o_ref)


def _expert_matmul(x, ws, tile_expert, tiles_used, *, out_dtype, tm, tn, swiglu=False, name):
    rows, kdim = x.shape
    n = ws[0].shape[-1]
    n_w = len(ws)
    assert rows % tm == 0 and n % tn == 0
    shrink = 2 if out_dtype == jnp.uint32 else 1
    vmem = 2 * (_nbytes((tm, kdim), x.dtype) + n_w * _nbytes((kdim, tn), ws[0].dtype) + _nbytes((tm, tn), out_dtype))
    vmem += n_w * (_nbytes((kdim, tn), BF16) + 2 * _nbytes((tm, tn), F32))
    gs = pltpu.PrefetchScalarGridSpec(
        num_scalar_prefetch=2, grid=(n // tn, rows // tm),
        in_specs=[pl.BlockSpec((tm, kdim), lambda j, i, te, nu: (i, 0))]
        + [pl.BlockSpec((1, kdim, tn), lambda j, i, te, nu: (te[i], 0, j))] * n_w,
        out_specs=pl.BlockSpec((tm, tn // shrink), lambda j, i, te, nu: (i, j)),
        scratch_shapes=[pltpu.VMEM((kdim, tn), BF16)] * n_w)
    return pl.pallas_call(
        functools.partial(_expert_mm_kernel, n_w=n_w, swiglu=swiglu), grid_spec=gs,
        out_shape=jax.ShapeDtypeStruct((rows, n // shrink), out_dtype),
        compiler_params=_params(("arbitrary", "arbitrary"), vmem),
        name=name,
    )(tile_expert, tiles_used, x, *ws)


def _moe(h_packed, logits, x, gate, final_norm, w_gate, w_up, w_down, *, dims):
    n, d = x.shape
    n_exp = dims.n_experts
    tm = _pick(n, (512, 256, 128, 64, 32, 16))
    p = 2 * n + n_exp * tm
    info, counts = _route(logits, n_exp)
    cnt = counts[0, :n_exp].astype(jnp.int32)
    padded = ((cnt + tm - 1) // tm) * tm
    ends = jnp.cumsum(padded)
    starts = ends - padded
    e1, e2 = info[:, 0].astype(jnp.int32), info[:, 1].astype(jnp.int32)
    dest1 = starts[e1] + info[:, 2].astype(jnp.int32)
    dest2 = starts[e2] + info[:, 3].astype(jnp.int32)
    token = jnp.arange(n, dtype=jnp.int32)
    src = jnp.zeros((p,), jnp.int32).at[dest1].set(token).at[dest2].set(token)
    tile_start = jnp.arange(p // tm, dtype=jnp.int32) * tm
    tile_expert = jnp.minimum(jnp.sum(tile_start[:, None] >= ends[None, :], axis=1), n_exp - 1).astype(jnp.int32)
    tiles_used = (ends[-1:] // tm).astype(jnp.int32)

    xs = _gather_rows(src, h_packed, tm=_pick(tm, (256, 128, 64, 32, 16)))
    f = w_gate.shape[-1]
    act = _expert_matmul(xs, [w_gate, w_up], tile_expert, tiles_used, out_dtype=BF16, tm=tm,
                         tn=_pick(f, (512, 256, 128)), swiglu=True, name="moe_gate_up")
    tn_down = _pick(d, (1024, 512, 256))
    y = _expert_matmul(act, [w_down], tile_expert, tiles_used, out_dtype=jnp.uint32, tm=tm, tn=tn_down,
                       name="moe_down")
    return _combine(dest1, dest2, y, x, info, gate, final_norm, seq=dims.seq, batch=dims.batch, pair_block=tn_down)


def _pad_cols(w, n):
    return w if w.shape[-1] == n else jnp.pad(w, [(0, 0)] * (w.ndim - 1) + [(0, n - w.shape[-1])])


def _forward(dims, x, c, ctx, c_ctx,
             ada_w0, ada_b0, norm_mix0, norm_ffn0, w_in0, a_q_norm, a_k_norm, b_q_norm, b_kv_norm,
             w_uq, w_ukv, w_o0, ffn_w_gate, ffn_w_up, ffn_w_down,
             ada_w1, ada_b1, norm_mix1, norm_ffn1, w_in1, c_sink, d_lam_q1, d_lam_k1, d_lam_q2, d_lam_k2,
             d_subln, w_o1, router_w, moe_w_gate, moe_w_up, moe_w_down, final_norm):
    dm, bsz, seq, ctx_len = dims.d_model, dims.batch, dims.seq, dims.ctx_len
    nl, nc = bsz * seq, bsz * ctx_len
    nt = nl + nc
    ha, hka, hb = dims.a_heads, dims.a_kv_heads, dims.b_heads
    hc, hkc, hd = dims.c_heads, dims.c_kv_heads, dims.d_heads
    big = lambda n: _pick(n, (1024, 512, 256, 128, 64, 32, 16))
    tm_all, tm_lat = big(math.gcd(nt, seq)), big(math.gcd(nl, seq))

    x_rows, ctx_rows = x.reshape(nl, dm), ctx.reshape(nc, dm)
    n_cond = -(-(bsz + 1) // SUBLANES) * SUBLANES
    cond = jnp.concatenate([c, c_ctx[None, :], jnp.zeros((n_cond - bsz - 1, dm), F32)], axis=0)

    def mods(ada_w, ada_b):
        m = _modulation(cond, ada_w, ada_b).reshape(n_cond, N_MOD, 1, dm)
        return [m[:, i] for i in range(N_MOD)]

    rope128 = _rope_tables(seq, dims.grid_w, 128, tm_all)
    rope64 = _rope_tables(seq, dims.grid_w, 64, tm_all)
    flash = functools.partial(_flash, batch=bsz, seq=seq, ctx_len=ctx_len, lat_rows=nl)

    def project(src, w, *, rows=nt, out_dtype=BF16, name, **head):
        n = w.shape[1]
        tm = tm_all if rows == nt else tm_lat
        if head:
            return _matmul(src, [w.astype(BF16)], out_dtype=out_dtype, tm=tm, tn=_pick(n, (1024, 512, 256, 128)),
                           rows=rows, epi="heads", head=dict(head, lat_rows=nl), seq=seq, name=name)
        return _matmul(src, [w.astype(BF16)], out_dtype=out_dtype, tm=tm, tn=_pick(n, (512, 256, 128)),
                       rows=rows, name=name)

    def values_t(src, col0, n_heads, name, chunk=_kv_chunk(seq)):
        return (_values_t(src, col0, n_heads, chunk=chunk, row0=0, rows=nl, name=name),
                _values_t(src, col0, n_heads, chunk=ctx_len, row0=nl, rows=nc, name=name + "_ctx"))

    sh1, sc1, g1, sh2, sc2, g2 = mods(ada_w0, ada_b0)
    h = _norm_mod((x_rows, ctx_rows), norm_mix0, sh1, sc1, rows=nt, seq=seq, batch=bsz)
    na_q, na_kv, rq, rkv = ha * HEAD_DIM, hka * HEAD_DIM, dims.b_q_rank, dims.b_kv_rank
    c_ak, c_av, c_cq, c_ckv, c_kr = na_q, na_q + na_kv, na_q + 2 * na_kv, na_q + 2 * na_kv + rq, na_q + 2 * na_kv + rq + rkv
    aq = project(h, w_in0[:, :c_ak], gain=a_q_norm, rope=rope128, rot_dim=128, scale=LOG2E * HEAD_DIM ** -0.5,
                 name="proj_aq")
    ak = project(h, w_in0[:, c_ak:c_av], gain=a_k_norm, rope=rope128, rot_dim=128, name="proj_ak")
    bk_rope = project(h, _pad_cols(w_in0[:, c_kr:], LANES), rope=rope64, rot_dim=64, name="proj_bk_rope")
    rest = project(h, jnp.concatenate([w_in0[:, c_cq:c_kr], w_in0[:, c_av:c_cq]], axis=1), out_dtype=F32,
                   name="proj_ranks_av")
    av = values_t(rest, (rq + rkv) // HEAD_DIM, hka, "vt_a")
    cq = _rank_norm(rest, 0, rq, b_q_norm)
    ckv = _rank_norm(rest, rq, rkv, b_kv_norm)
    b_scale = LOG2E * (HEAD_DIM + 64) ** -0.5
    uq = w_uq.reshape(rq, hb, HEAD_DIM + 64) * b_scale
    bq_nope = project(cq, uq[:, :, :HEAD_DIM].reshape(rq, hb * HEAD_DIM), name="b_up_q_nope")
    bq_rope = project(cq, _pad_cols(uq[:, :, HEAD_DIM:], LANES).reshape(rq, hb * LANES), rope=rope64, rot_dim=64,
                      name="b_up_q_rope")
    ukv = w_ukv.reshape(rkv, hb, 2 * HEAD_DIM)
    ukv = jnp.concatenate([ukv[:, :, :HEAD_DIM].reshape(rkv, hb * HEAD_DIM),
                           ukv[:, :, HEAD_DIM:].reshape(rkv, hb * HEAD_DIM)], axis=1)
    bkv = project(ckv, ukv, name="b_up_kv")

    a_lat = flash([(aq, 0)], [(ak, 0, 1)], av, n_heads=ha, group=ha // hka, name="attn_a")
    a_ctx = flash([(aq, 0)], [(ak, 0, 1)], av, n_heads=ha, group=ha // hka, ctx_queries=True, name="attn_a_ctx")
    b_args = ([(bq_nope, 0), (bq_rope, 0)], [(bkv, 0, 1), (bk_rope, 0, 0)], values_t(bkv, hb, hb, "vt_b"))
    b_lat = flash(*b_args, n_heads=hb, group=1, name="attn_b")
    b_ctx = flash(*b_args, n_heads=hb, group=1, ctx_queries=True, name="attn_b_ctx")
    xa = _matmul([(a_lat, a_ctx), (b_lat, b_ctx)], [w_o0], out_dtype=F32, tm=tm_all, tn=_pick(dm, (512, 256, 128)),
                 epi="gres", res=(x_rows, ctx_rows), gate=g1, seq=seq, batch=bsz, name="out_proj0")

    h = _norm_mod(xa, norm_ffn0, sh2, sc2, rows=nt, seq=seq, batch=bsz)
    ffn = dims.ffn_dim
    act = _matmul(h, [ffn_w_gate, ffn_w_up], out_dtype=BF16, tm=tm_all, tn=_pick(ffn, (256, 128)), epi="swiglu",
                  name="ffn_gate_up")
    xa = _matmul(act, [ffn_w_down.astype(BF16)], out_dtype=F32, tm=tm_all // 2 if ffn > 4096 else tm_all,
                 tn=_pick(dm, (512, 256, 128)), epi="gres", res=xa, gate=g2, seq=seq, batch=bsz, name="ffn_down")

    sh1, sc1, g1, sh2, sc2, g2 = mods(ada_w1, ada_b1)
    h = _norm_mod(xa, norm_mix1, sh1, sc1, rows=nt, seq=seq, batch=bsz)
    bounds = [0]
    for n_heads in (hc, hkc, hkc, hd, hd, hd):
        bounds.append(bounds[-1] + n_heads * HEAD_DIM)
    w_cq, w_ck, w_cv, w_dq, w_dk, w_dv = (w_in1[:, lo:hi] for lo, hi in zip(bounds[:-1], bounds[1:]))
    cq_ = project(h, w_cq, rows=nl, rope=rope128, rot_dim=128, scale=LOG2E * HEAD_DIM ** -0.5, name="proj_cq")
    ck_ = project(h, w_ck, rope=rope128, rot_dim=128, name="proj_ck")
    dq_ = project(h, w_dq, rows=nl, rope=rope64, rot_dim=64, scale=LOG2E * 64 ** -0.5, name="proj_dq")
    dk_ = project(h, w_dk, rope=rope64, rot_dim=64, name="proj_dk")
    values = project(h, jnp.concatenate([w_cv, w_dv], axis=1), name="proj_cv_dv")
    cv_ = values_t(values, 0, hkc, "vt_c", chunk=LANES)
    dv_ = values_t(values, hkc, hd, "vt_d")
    sink = jnp.broadcast_to(LOG2E * c_sink.astype(F32)[:, None, None], (hc, 1, LANES))
    lam = jnp.stack([_pad_cols(v.astype(F32)[None, :], LANES)[0] for v in (d_lam_q1, d_lam_k1, d_lam_q2, d_lam_k2)])
    c_out = flash([(cq_, 0)], [(ck_, 0, 1)], cv_, n_heads=hc, group=hc // hkc, mode="window", sink=sink,
                  window=dims.window, name="attn_c")
    d_out = flash([(dq_, 0)], [(dk_, 0, 1)], dv_, n_heads=hd, group=1, mode="diff", lam=lam,
                  subln=d_subln.reshape(1, LANES), name="attn_d")
    xl = _matmul([c_out, d_out], [w_o1], out_dtype=F32, tm=tm_lat, tn=_pick(dm, (512, 256, 128)),
                 epi="gres", res=xa, gate=g1, seq=seq, batch=bsz, rows=nl, name="out_proj1")

    rw = _pad_cols(router_w, LANES)
    rw_hi = rw.astype(BF16)
    rw_lo = (rw - rw_hi.astype(F32)).astype(BF16)
    h_packed, logits = _norm_mod(xl, norm_ffn1, sh2, sc2, rows=nl, seq=seq, batch=bsz, router=(rw_hi, rw_lo))
    out = _moe(h_packed, logits, xl, g2, final_norm, moe_w_gate, moe_w_up, moe_w_down, dims=dims)
    return out.reshape(bsz, seq, dm)


def _rank_norm_kernel(x_ref, g_ref, o_ref):
    o_ref[...] = _norm_rows(x_ref[...], g_ref[...]).astype(o_ref.dtype)


def _rank_norm(src, col0, width, gain):
    rows = src.shape[0]
    tm = _pick(rows, (256, 128, 64, 32, 16))
    assert col0 % width == 0
    vmem = 2 * (_nbytes((tm, width), F32) + _nbytes((tm, width), BF16))
    return pl.pallas_call(
        _rank_norm_kernel,
        grid=(rows // tm,),
        in_specs=[pl.BlockSpec((tm, width), lambda i: (i, col0 // width)),
                  pl.BlockSpec((1, width), lambda i: (0, 0))],
        out_specs=pl.BlockSpec((tm, width), lambda i: (i, 0)),
        out_shape=jax.ShapeDtypeStruct((rows, width), BF16),
        compiler_params=_params(("parallel",), vmem),
        name="rank_norm",
    )(src, gain.reshape(1, width))


_DIMS = Dims(d_model=4096, batch=4, seq=4096, ctx_len=256, grid_w=64, a_heads=16, a_kv_heads=4, b_heads=16,
             b_q_rank=1536, b_kv_rank=512, c_heads=16, c_kv_heads=4, window=128, d_heads=16, ffn_dim=11008,
             n_experts=8, expert_dim=3584)


def kernel(x, c, ctx, c_ctx, ada_w0, ada_b0, norm_mix0, norm_ffn0, w_in0, a_q_norm, a_k_norm, b_q_norm, b_kv_norm, w_uq, w_ukv, w_o0, ffn_w_gate, ffn_w_up, ffn_w_down, ada_w1, ada_b1, norm_mix1, norm_ffn1, w_in1, c_sink, d_lam_q1, d_lam_k1, d_lam_q2, d_lam_k2, d_subln, w_o1, router_w, moe_w_gate, moe_w_up, moe_w_down, final_norm):
    return _forward(_DIMS, x, c, ctx, c_ctx, ada_w0, ada_b0, norm_mix0, norm_ffn0, w_in0, a_q_norm, a_k_norm,
                    b_q_norm, b_kv_norm, w_uq, w_ukv, w_o0, ffn_w_gate, ffn_w_up, ffn_w_down, ada_w1, ada_b1,
                    norm_mix1, norm_ffn1, w_in1, c_sink, d_lam_q1, d_lam_k1, d_lam_q2, d_lam_k2, d_subln, w_o1,
                    router_w, moe_w_gate, moe_w_up, moe_w_down, final_norm)
```

```python
import functools
import math
from typing import NamedTuple

import jax
import jax.numpy as jnp
from jax import lax
from jax.experimental import pallas as pl
from jax.experimental.pallas import tpu as pltpu

F32 = jnp.float32
BF16 = jnp.bfloat16

LANES = 128
SUBLANES = 8
HEAD_DIM = 128
ROPE_THETA = 10000.0
NORM_EPS = 1e-6
MASK_VALUE = -1e30
LOG2E = math.log2(math.e)
N_MOD = 6
D_LAYER_INDEX = 1
D_LAMBDA_INIT = 0.8 - 0.6 * math.exp(-0.3 * D_LAYER_INDEX)
VMEM_HEADROOM = 6 << 20
DMA_LOOP_UNROLL = 8
HEAD_SUBBLOCK = 256


class Dims(NamedTuple):
    d_model: int
    batch: int
    seq: int
    ctx_len: int
    grid_w: int
    a_heads: int
    a_kv_heads: int
    b_heads: int
    b_q_rank: int
    b_kv_rank: int
    c_heads: int
    c_kv_heads: int
    window: int
    d_heads: int
    ffn_dim: int
    n_experts: int
    expert_dim: int


def _pick(n, prefs):
    for p in prefs:
        if n % p == 0:
            return p
    raise ValueError(f"no tile in {prefs} divides {n}")


def _params(sem, vmem_bytes):
    return pltpu.CompilerParams(dimension_semantics=sem, vmem_limit_bytes=int(vmem_bytes) + VMEM_HEADROOM)


def _nbytes(shape, dtype):
    return math.prod(shape) * jnp.dtype(dtype).itemsize


def _mod_kernel(c_ref, w_ref, b_ref, o_ref):
    c = c_ref[...]
    s = (c * jax.nn.sigmoid(c)).astype(BF16)
    o_ref[...] = jnp.dot(s, w_ref[...].astype(BF16), preferred_element_type=F32) + b_ref[...]


def _modulation(cond, ada_w, ada_b):
    rows, d = cond.shape
    n = ada_w.shape[1]
    tn = _pick(n, (512, 256, 128))
    vmem = 2 * (_nbytes((d, tn), F32) + _nbytes((rows, tn), F32) * 2) + _nbytes((rows, d), F32) * 2
    return pl.pallas_call(
        _mod_kernel,
        grid=(n // tn,),
        in_specs=[pl.BlockSpec((rows, d), lambda j: (0, 0)),
                  pl.BlockSpec((d, tn), lambda j: (0, j)),
                  pl.BlockSpec((1, tn), lambda j: (0, j))],
        out_specs=pl.BlockSpec((rows, tn), lambda j: (0, j)),
        out_shape=jax.ShapeDtypeStruct((rows, n), F32),
        compiler_params=_params(("parallel",), vmem),
        name="modulation",
    )(cond, ada_w, ada_b.reshape(1, n))


def _norm_rows(x, g):
    var = jnp.mean(x * x, axis=-1, keepdims=True)
    return x * lax.rsqrt(var + NORM_EPS) * g


def _pack_bf16_pairs(x):
    n = x.shape[1] // 2
    bits = lambda v: lax.bitcast_convert_type(v.astype(BF16).astype(F32), jnp.uint32)
    return lax.shift_right_logical(bits(x[:, :n]), jnp.uint32(16)) | bits(x[:, n:])


def _unpack_bf16_pairs(words):
    left = lax.bitcast_convert_type(lax.shift_left(words, jnp.uint32(16)), F32)
    right = lax.bitcast_convert_type(words & jnp.uint32(0xFFFF0000), F32)
    return left, right


def _norm_mod_kernel(x_ref, g_ref, sh_ref, sc_ref, o_ref):
    y = _norm_rows(x_ref[...], g_ref[...])
    o_ref[...] = (y * (1.0 + sc_ref[0]) + sh_ref[0]).astype(o_ref.dtype)


def _norm_mod_pair_kernel(lat_ref, ctx_ref, g_ref, sh_ref, sc_ref, o_ref, *, lat_tiles):
    x = jnp.where(pl.program_id(0) < lat_tiles, lat_ref[...], ctx_ref[...])
    y = _norm_rows(x, g_ref[...])
    o_ref[...] = (y * (1.0 + sc_ref[0]) + sh_ref[0]).astype(o_ref.dtype)


def _norm_mod_router_kernel(x_ref, g_ref, sh_ref, sc_ref, rhi_ref, rlo_ref, of_ref, lg_ref):
    y = _norm_rows(x_ref[...], g_ref[...])
    h = y * (1.0 + sc_ref[0]) + sh_ref[0]
    hi = h.astype(BF16)
    lo = (h - hi.astype(F32)).astype(BF16)
    of_ref[...] = _pack_bf16_pairs(h)
    lg_ref[...] = (jnp.dot(hi, rhi_ref[...], preferred_element_type=F32)
                   + jnp.dot(hi, rlo_ref[...], preferred_element_type=F32)
                   + jnp.dot(lo, rhi_ref[...], preferred_element_type=F32))


def _mod_index(tm, seq, batch):
    return lambda i: jnp.minimum((i * tm) // seq, batch)


def _norm_mod(x, g, shift, scale, *, rows, seq, batch, router=None):
    pair = isinstance(x, tuple)
    d = (x[0] if pair else x).shape[1]
    tm = _pick(math.gcd(rows, seq), (256, 128, 64, 32, 16))
    midx = _mod_index(tm, seq, batch)
    tail_specs = [pl.BlockSpec((1, d), lambda i: (0, 0)),
                  pl.BlockSpec((1, 1, d), lambda i: (midx(i), 0, 0)),
                  pl.BlockSpec((1, 1, d), lambda i: (midx(i), 0, 0))]
    vmem = 2 * (_nbytes((tm, d), F32) + _nbytes((tm, d), BF16)) + 6 * _nbytes((1, d), F32)
    if pair:
        lat_tiles = x[0].shape[0] // tm
        assert x[0].shape[0] % tm == 0 and x[1].shape[0] % tm == 0 and router is None
        x_specs = [pl.BlockSpec((tm, d), lambda i: (jnp.minimum(i, lat_tiles - 1), 0)),
                   pl.BlockSpec((tm, d), lambda i: (jnp.maximum(i - lat_tiles, 0), 0))]
        return pl.pallas_call(
            functools.partial(_norm_mod_pair_kernel, lat_tiles=lat_tiles),
            grid=(rows // tm,),
            in_specs=x_specs + tail_specs,
            out_specs=pl.BlockSpec((tm, d), lambda i: (i, 0)),
            out_shape=jax.ShapeDtypeStruct((rows, d), BF16),
            compiler_params=_params(("parallel",), vmem + 2 * _nbytes((tm, d), F32)),
            name="norm_modulate_inputs",
        )(*x, g.reshape(1, d), shift, scale)
    in_specs = [pl.BlockSpec((tm, d), lambda i: (i, 0))] + tail_specs
    if router is None:
        return pl.pallas_call(
            _norm_mod_kernel,
            grid=(rows // tm,),
            in_specs=in_specs,
            out_specs=pl.BlockSpec((tm, d), lambda i: (i, 0)),
            out_shape=jax.ShapeDtypeStruct((rows, d), BF16),
            compiler_params=_params(("parallel",), vmem),
            name="norm_modulate",
        )(x, g.reshape(1, d), shift, scale)
    rhi, rlo = router
    vmem += 2 * (_nbytes((tm, d), F32) + _nbytes((tm, LANES), F32)) + 4 * _nbytes((d, LANES), BF16)
    return pl.pallas_call(
        _norm_mod_router_kernel,
        grid=(rows // tm,),
        in_specs=in_specs + [pl.BlockSpec((d, LANES), lambda i: (0, 0)),
                             pl.BlockSpec((d, LANES), lambda i: (0, 0))],
        out_specs=[pl.BlockSpec((tm, d // 2), lambda i: (i, 0)),
                   pl.BlockSpec((tm, LANES), lambda i: (i, 0))],
        out_shape=[jax.ShapeDtypeStruct((rows, d // 2), jnp.uint32),
                   jax.ShapeDtypeStruct((rows, LANES), F32)],
        compiler_params=_params(("parallel",), vmem),
        name="norm_modulate_router",
    )(x, g.reshape(1, d), shift, scale, rhi, rlo)


def _head_op(x, gain, rope, rot_dim, scale):
    if gain is not None:
        x = _norm_rows(x, gain)
    if rot_dim:
        cos, up, down = rope
        half = rot_dim // 2
        x = x * cos + pltpu.roll(x, LANES - half, 1) * up + pltpu.roll(x, half, 1) * down
    return x if scale == 1.0 else x * scale


def _mm_kernel(*refs, n_w, nk, epi, x_parts=(False,), res_pair=False, lat_tiles=0, head_norm=False, rot_dim=0,
               scale=1.0):
    refs = list(refs)
    x_refs = []
    for has_ctx in x_parts:
        x_refs.append((refs.pop(0), refs.pop(0) if has_ctx else None))
    x_dtype = x_refs[0][0].dtype
    w_refs = refs[:n_w]

    def load_x():
        on_lat = pl.program_id(0) < lat_tiles
        parts = [lat[...] if ctx is None else jnp.where(on_lat, lat[...], ctx[...]) for lat, ctx in x_refs]
        return parts[0] if len(parts) == 1 else jnp.concatenate(parts, axis=-1)

    def weight(w_ref, cols=slice(None)):
        w = w_ref[:, cols]
        return w if w.dtype == x_dtype else w.astype(x_dtype)

    pos = n_w
    if epi == "gres":
        res_ref, res_ctx_ref = refs[pos], (refs[pos + 1] if res_pair else None)
        gate_ref = refs[pos + 1 + int(res_pair)]
        pos += 2 + int(res_pair)
    if epi == "heads":
        gain_ref = refs[pos] if head_norm else None
        pos += int(head_norm)
        rope_refs = refs[pos:pos + 3] if rot_dim else None
        pos += 3 if rot_dim else 0
    o_ref = refs[pos]
    acc_refs = refs[pos + 1:]
    k = pl.program_id(2)

    def epilogue(parts):
        if epi == "swiglu":
            g, u = parts
            out = g * jax.nn.sigmoid(g) * u
        elif epi == "gres":
            res = res_ref[...]
            if res_pair:
                res = jnp.where(pl.program_id(0) < lat_tiles, res, res_ctx_ref[...])
            out = res + gate_ref[0] * parts[0]
        else:
            out = parts[0]
        o_ref[...] = out.astype(o_ref.dtype)

    def compute_heads():
        x = load_x()
        gain = gain_ref[...] if head_norm else None
        rope = tuple(r[...] for r in rope_refs) if rot_dim else None
        tn = o_ref.shape[1]
        sub = HEAD_SUBBLOCK if tn % HEAD_SUBBLOCK == 0 else tn
        starts = list(range(0, tn, sub))
        dot = lambda c: jnp.dot(x, weight(w_refs[0], slice(c, c + sub)), preferred_element_type=F32)
        part = dot(starts[0])
        for idx, c in enumerate(starts):
            nxt = dot(starts[idx + 1]) if idx + 1 < len(starts) else None
            for hh in range(sub // LANES):
                cols = slice(hh * LANES, (hh + 1) * LANES)
                o_ref[:, c + hh * LANES:c + (hh + 1) * LANES] = _head_op(
                    part[:, cols], gain, rope, rot_dim, scale).astype(o_ref.dtype)
            part = nxt

    def compute():
        if epi == "heads":
            compute_heads()
            return
        x = load_x()
        parts = [jnp.dot(x, weight(w), preferred_element_type=F32) for w in w_refs]
        if nk == 1:
            epilogue(parts)
            return

        @pl.when(k == 0)
        def _():
            for a, p in zip(acc_refs, parts):
                a[...] = p

        @pl.when(k > 0)
        def _():
            for a, p in zip(acc_refs, parts):
                a[...] += p

        @pl.when(k == nk - 1)
        def _():
            epilogue([a[...] for a in acc_refs])

    compute()


def _matmul(x, ws, *, out_dtype, tm, tn, tk=None, epi="none", res=None, gate=None, seq=None, batch=None,
            rows=None, head=None, name="matmul"):
    parts = x if isinstance(x, list) else [x]
    parts = [p if isinstance(p, tuple) else (p, None) for p in parts]
    x_dtype = parts[0][0].dtype
    kdim = sum(lat.shape[1] for lat, _ in parts)
    lat_tiles = parts[0][0].shape[0] // tm
    m = parts[0][0].shape[0] + (0 if parts[0][1] is None else parts[0][1].shape[0])
    rows = m if rows is None else rows
    n = ws[0].shape[-1]
    tk = kdim if tk is None else tk
    nk = kdim // tk
    assert rows % tm == 0 and n % tn == 0 and kdim % tk == 0 and (len(parts) == 1 or nk == 1)
    n_w = len(ws)

    in_specs, args = [], []
    for lat, ctx in parts:
        width = tk if len(parts) == 1 else lat.shape[1]
        if ctx is None:
            in_specs.append(pl.BlockSpec((tm, width), lambda i, j, k: (i, k)))
            args.append(lat)
        else:
            in_specs += [pl.BlockSpec((tm, width), lambda i, j, k: (jnp.minimum(i, lat_tiles - 1), k)),
                         pl.BlockSpec((tm, width), lambda i, j, k: (jnp.maximum(i - lat_tiles, 0), k))]
            args += [lat, ctx]
    in_specs += [pl.BlockSpec((tk, tn), lambda i, j, k: (k, j))] * n_w
    args += list(ws)
    n_x = len(args) - n_w
    vmem = 2 * (n_x * _nbytes((tm, tk), x_dtype) // len(parts) + n_w * _nbytes((tk, tn), ws[0].dtype)
                + _nbytes((tm, tn), out_dtype))
    vmem += _nbytes((tm, tk), x_dtype) if n_x > 1 else 0
    vmem += 2 * n_w * _nbytes((tm, tn), F32)
    if ws[0].dtype != x_dtype:
        vmem += n_w * _nbytes((tk, tn), x_dtype)
    res_pair = isinstance(res, tuple)
    if epi == "gres":
        midx = _mod_index(tm, seq, batch)
        if res_pair:
            in_specs += [pl.BlockSpec((tm, tn), lambda i, j, k: (jnp.minimum(i, lat_tiles - 1), j)),
                         pl.BlockSpec((tm, tn), lambda i, j, k: (jnp.maximum(i - lat_tiles, 0), j))]
            args += list(res)
        else:
            in_specs.append(pl.BlockSpec((tm, tn), lambda i, j, k: (i, j)))
            args.append(res)
        in_specs.append(pl.BlockSpec((1, 1, tn), lambda i, j, k: (midx(i), 0, j)))
        args.append(gate)
        vmem += 2 * (1 + int(res_pair)) * _nbytes((tm, tn), F32)
    head_kw = {}
    if epi == "heads":
        gain, rope, rot_dim = head.get("gain"), head.get("rope"), head.get("rot_dim", 0)
        head_kw = dict(head_norm=gain is not None, rot_dim=rot_dim, scale=head.get("scale", 1.0))
        if gain is not None:
            in_specs.append(pl.BlockSpec((1, LANES), lambda i, j, k: (0, 0)))
            args.append(gain.reshape(1, LANES))
        if rot_dim:
            lat_tiles, seq_tiles = head["lat_rows"] // tm, seq // tm
            assert rope[0].shape[0] >= seq + tm
            tab = lambda i, j, k: (jnp.where(i < lat_tiles, i % seq_tiles, seq_tiles), 0)
            in_specs += [pl.BlockSpec((tm, LANES), tab)] * 3
            args += list(rope)
            vmem += 2 * 3 * _nbytes((tm, LANES), F32)
        vmem += 6 * _nbytes((tm, LANES), F32)
    scratch = [pltpu.VMEM((tm, tn), F32) for _ in range(n_w)] if nk > 1 else []
    vmem += len(scratch) * _nbytes((tm, tn), F32)
    return pl.pallas_call(
        functools.partial(_mm_kernel, n_w=n_w, nk=nk, epi=epi, x_parts=tuple(ctx is not None for _, ctx in parts),
                          res_pair=res_pair, lat_tiles=lat_tiles, **head_kw),
        grid=(rows // tm, n // tn, nk),
        in_specs=in_specs,
        out_specs=pl.BlockSpec((tm, tn), lambda i, j, k: (i, j)),
        out_shape=jax.ShapeDtypeStruct((rows, n), out_dtype),
        scratch_shapes=scratch,
        compiler_params=_params(("parallel", "parallel", "arbitrary"), vmem),
        name=name,
    )(*args)


def _rope_tables(seq, grid_w, rot_dim, pad_rows):
    n_rows = seq // grid_w
    rows = jnp.repeat(jnp.arange(n_rows, dtype=F32), grid_w)
    cols = jnp.tile(jnp.arange(grid_w, dtype=F32), n_rows)
    n_freq = rot_dim // 4
    inv_freq = jnp.power(ROPE_THETA, -jnp.arange(n_freq, dtype=F32) / n_freq)
    ang = jnp.concatenate([rows[:, None] * inv_freq, cols[:, None] * inv_freq], axis=-1)
    cos, sin = jnp.cos(ang), jnp.sin(ang)
    zero = jnp.zeros_like(sin)
    reps = LANES // rot_dim
    cos_t = jnp.tile(jnp.concatenate([cos, cos], axis=-1), (1, reps))
    up_t = jnp.tile(jnp.concatenate([-sin, zero], axis=-1), (1, reps))
    down_t = jnp.tile(jnp.concatenate([zero, sin], axis=-1), (1, reps))
    ident = jnp.ones((pad_rows, LANES), F32)
    zpad = jnp.zeros((pad_rows, LANES), F32)
    return (jnp.concatenate([cos_t, ident]), jnp.concatenate([up_t, zpad]), jnp.concatenate([down_t, zpad]))


def _values_t_kernel(x_ref, o_ref, *, n_heads):
    for h in range(n_heads):
        cols = slice(h * LANES, (h + 1) * LANES)
        o_ref[0, cols, :] = x_ref[:, cols].astype(F32).T.astype(o_ref.dtype)


def _values_t(src, col0, n_heads, *, chunk, row0, rows, name):
    hb = math.gcd(col0, n_heads)
    width = hb * LANES
    assert row0 % chunk == 0 and rows % chunk == 0
    vmem = 2 * (_nbytes((chunk, width), src.dtype) + _nbytes((chunk, width), BF16)) + 4 * _nbytes((chunk, LANES), F32)
    return pl.pallas_call(
        functools.partial(_values_t_kernel, n_heads=hb),
        grid=(rows // chunk, n_heads // hb),
        in_specs=[pl.BlockSpec((chunk, width), lambda i, j: (row0 // chunk + i, col0 // hb + j))],
        out_specs=pl.BlockSpec((1, width, chunk), lambda i, j: (i, j, 0)),
        out_shape=jax.ShapeDtypeStruct((rows // chunk, n_heads * LANES, chunk), BF16),
        compiler_params=_params(("parallel", "parallel"), vmem),
        name=name,
    )(src)


def _kv_chunk(seq):
    return _pick(seq // 2, (512, 256, 128))


def _qk(q, k):
    return lax.dot_general(q, k, (((1,), (1,)), ((), ())), preferred_element_type=F32)


def _cat(refs, rows=None):
    parts = [r[...] if rows is None else r[rows, :] for r in refs]
    return parts[0] if len(parts) == 1 else jnp.concatenate(parts, axis=-1)


def _flash_kernel(*refs, mode, n_q, has_lat, tq, tqs, tkc, seq, window):
    refs = list(refs)
    q_refs = [refs.pop(0) for _ in range(n_q)]
    kc_refs = [refs.pop(0) for _ in range(n_q)]
    vc_ref = refs.pop(0)
    if has_lat:
        kl_refs = [refs.pop(0) for _ in range(n_q)]
        vl_ref = refs.pop(0)
    if mode == "window":
        sink_ref = refs.pop(0)
    if mode == "diff":
        lam_ref, subln_ref = refs.pop(0), refs.pop(0)
    o_ref = refs.pop(0)
    n_state = 2 if mode == "diff" else 1
    n_sub = tq // tqs
    chain_ids = [(sub, st) for sub in range(n_sub) for st in range(n_state)]
    per_chain = 7
    state_refs = {cid: tuple(refs[per_chain * i:][:3]) for i, cid in enumerate(chain_ids)}
    score_refs = {cid: tuple(refs[per_chain * i + 3:][:2]) for i, cid in enumerate(chain_ids)}
    smax_refs = {cid: tuple(refs[per_chain * i + 5:][:2]) for i, cid in enumerate(chain_ids)}
    qi = pl.program_id(2)

    def scores(k, only_sub=None):
        out = {}
        for sub, st in chain_ids:
            if only_sub is not None and sub != only_sub:
                continue
            q = _cat(q_refs, pl.ds(sub * tqs, tqs))
            if mode == "diff":
                lane = lax.broadcasted_iota(jnp.int32, q.shape, 1)
                keep = (lane < LANES // 2) if st == 0 else (lane >= LANES // 2)
                q = jnp.where(keep, q, jnp.zeros_like(q))
            out[sub, st] = _qk(k, q)
        return out

    def absorb(s_of, vt, *, first=False, mask=None):
        for cid, s in s_of.items():
            m_ref, l_ref, acc_ref = state_refs[cid]
            if isinstance(s, tuple):
                s, s_max = s
            else:
                if mask is not None:
                    s = jnp.where(mask, s, MASK_VALUE)
                s_max = jnp.max(s, axis=0, keepdims=True)
            if first and mode == "window":
                m_old = jnp.broadcast_to(sink_ref[0][:, :1], (1, tqs))
                m_new = jnp.maximum(m_old, s_max)
                l_old = jnp.exp2(m_old - m_new)
            elif first:
                m_new, l_old = s_max, None
            else:
                m_old = m_ref[...]
                m_new = jnp.maximum(m_old, s_max)
                alpha = jnp.exp2(m_old - m_new)
                l_old = alpha * l_ref[...]
            p = jnp.exp2(s - m_new)
            l_new = jnp.sum(p, axis=0, keepdims=True)
            pv = jnp.dot(vt, p.astype(vt.dtype), preferred_element_type=F32)
            m_ref[...] = m_new
            l_ref[...] = l_new if l_old is None else l_old + l_new
            acc_ref[...] = pv if first else alpha * acc_ref[...] + pv

    def lat_keys(c):
        return _cat(kl_refs, pl.ds(pl.multiple_of(c * tkc, tkc), tkc))

    def stash(slot, s_of):
        for cid, s in s_of.items():
            score_refs[cid][slot][...] = s
            smax_refs[cid][slot][...] = jnp.max(s, axis=0, keepdims=True)

    def fetch(slot):
        return {cid: (score_refs[cid][slot][...], smax_refs[cid][slot][...]) for cid in chain_ids}

    s_ctx = scores(_cat(kc_refs))
    if has_lat and mode == "window":
        tkw = tqs + 2 * window
        blocks = []
        for sub in range(n_sub):
            q_pos0 = qi * tq + sub * tqs
            start = pl.multiple_of(jnp.clip(q_pos0 - window, 0, seq - tkw), LANES)
            kp = start + lax.broadcasted_iota(jnp.int32, (tkw, tqs), 0)
            qp = q_pos0 + lax.broadcasted_iota(jnp.int32, (tkw, tqs), 1)
            blocks.append((scores(_cat(kl_refs, pl.ds(start, tkw)), only_sub=sub), start // LANES,
                           jnp.abs(qp - kp) <= window))
        absorb(s_ctx, vc_ref[0], first=True)
        for s_of, first_chunk, mask in blocks:
            vt = jnp.concatenate([vl_ref[first_chunk + j] for j in range(tkw // LANES)], axis=-1)
            absorb(s_of, vt, mask=mask)
    elif has_lat:
        n_chunks = seq // tkc
        stash(0, scores(lat_keys(0)))
        absorb(s_ctx, vc_ref[0], first=True)

        def body(i, carry):
            c = 2 * i
            stash(1, scores(lat_keys(c + 1)))
            absorb(fetch(0), vl_ref[c])
            stash(0, scores(lat_keys(c + 2)))
            absorb(fetch(1), vl_ref[c + 1])
            return carry

        lax.fori_loop(0, n_chunks // 2 - 1, body, 0)
        stash(1, scores(lat_keys(n_chunks - 1)))
        absorb(fetch(0), vl_ref[n_chunks - 2])
        absorb(fetch(1), vl_ref[n_chunks - 1])
    else:
        absorb(s_ctx, vc_ref[0], first=True)

    for sub in range(n_sub):
        rows = pl.ds(sub * tqs, tqs)
        outs = [(state_refs[sub, st][2][...] / state_refs[sub, st][1][...]).T for st in range(n_state)]
        if mode == "diff":
            lam_rows = lam_ref[...]
            lam = (jnp.exp(jnp.sum(lam_rows[0:1] * lam_rows[1:2], axis=-1, keepdims=True))
                   - jnp.exp(jnp.sum(lam_rows[2:3] * lam_rows[3:4], axis=-1, keepdims=True)) + D_LAMBDA_INIT)
            out = _norm_rows(outs[0] - lam * outs[1], subln_ref[...]) * (1.0 - D_LAMBDA_INIT)
        else:
            out = outs[0]
        o_ref[rows, :] = out.astype(o_ref.dtype)


def _flash(q_parts, k_parts, v, *, n_heads, group, batch, seq, ctx_len, lat_rows, ctx_queries=False,
           mode="softmax", sink=None, lam=None, subln=None, window=0, name="flash"):
    has_lat = not ctx_queries
    ctx_blk0 = lat_rows // ctx_len
    if ctx_queries:
        tq, nq = ctx_len, 1
        q_row = lambda b, h, i: ctx_blk0 + b
        out_rows = batch * ctx_len
        o_row = lambda b, h, i: b
    else:
        tq = _pick(seq, (1024, 512, 256, 128))
        nq = seq // tq
        q_row = lambda b, h, i: b * nq + i
        out_rows = lat_rows
        o_row = q_row
    tqs = min(tq, 256)
    tkc = _kv_chunk(seq)
    n_state = 2 if mode == "diff" else 1
    if mode == "window":
        assert seq >= tqs + 2 * window and window % 16 == 0
    n_q = len(q_parts)
    in_specs, args = [], []
    vmem = 0

    def add(arr, block, imap):
        nonlocal vmem
        in_specs.append(pl.BlockSpec(block, imap))
        args.append(arr)
        vmem += 2 * _nbytes(block, arr.dtype)

    for arr, c0 in q_parts:
        add(arr, (tq, LANES), lambda b, h, i, c0=c0: (q_row(b, h, i), c0 + h))
    for arr, c0, per_head in k_parts:
        add(arr, (ctx_len, LANES), lambda b, h, i, c0=c0, ph=per_head: (ctx_blk0 + b, c0 + (h // group) * ph))
    vt_lat, vt_ctx = v
    add(vt_ctx, (1, LANES, ctx_len), lambda b, h, i: (b, h // group, 0))
    if has_lat:
        for arr, c0, per_head in k_parts:
            add(arr, (seq, LANES), lambda b, h, i, c0=c0, ph=per_head: (b, c0 + (h // group) * ph))
        tkv = vt_lat.shape[-1]
        assert tkv == (LANES if mode == "window" else tkc)
        add(vt_lat, (seq // tkv, LANES, tkv), lambda b, h, i: (b, h // group, 0))
    if mode == "window":
        add(sink, (1, 1, LANES), lambda b, h, i: (h, 0, 0))
    if mode == "diff":
        add(lam, (4, LANES), lambda b, h, i: (0, 0))
        add(subln, (1, LANES), lambda b, h, i: (0, 0))
    n_sub = tq // tqs
    scratch = [pltpu.VMEM((1, tqs), F32), pltpu.VMEM((1, tqs), F32), pltpu.VMEM((LANES, tqs), F32),
               pltpu.VMEM((tkc, tqs), F32), pltpu.VMEM((tkc, tqs), F32),
               pltpu.VMEM((1, tqs), F32), pltpu.VMEM((1, tqs), F32)] * (n_state * n_sub)
    vmem += 2 * n_state * n_sub * _nbytes((tkc, tqs), F32)
    vmem += 2 * _nbytes((tq, LANES), BF16) + 3 * _nbytes((n_state, tq, LANES), F32)
    vmem += 4 * n_state * (tq // tqs) * _nbytes((tqs, max(tkc, ctx_len, tqs + 2 * window)), F32)
    kern = functools.partial(_flash_kernel, mode=mode, n_q=n_q, has_lat=has_lat, tq=tq, tqs=tqs, tkc=tkc,
                             seq=seq, window=window)
    return pl.pallas_call(
        kern,
        grid=(batch, n_heads, nq),
        in_specs=in_specs,
        out_specs=pl.BlockSpec((tq, LANES), lambda b, h, i: (o_row(b, h, i), h)),
        out_shape=jax.ShapeDtypeStruct((out_rows, n_heads * LANES), BF16),
        scratch_shapes=scratch,
        compiler_params=_params(("parallel", "parallel", "parallel"), vmem),
        name=name,
    )(*args)


def _route_kernel(lg_ref, info_ref, cnt_ref, base_ref, *, n_experts, tb):
    step = pl.program_id(0)

    @pl.when(step == 0)
    def _():
        base_ref[...] = jnp.zeros_like(base_ref)

    lane = lax.broadcasted_iota(jnp.int32, (tb, LANES), 1).astype(F32)
    logits = jnp.where(lane < n_experts, lg_ref[...], -jnp.inf)
    v1 = jnp.max(logits, axis=-1, keepdims=True)
    i1 = jnp.min(jnp.where(logits == v1, lane, float(LANES)), axis=-1, keepdims=True)
    hot1 = lane == i1
    rest = jnp.where(hot1, -jnp.inf, logits)
    v2 = jnp.max(rest, axis=-1, keepdims=True)
    i2 = jnp.min(jnp.where(rest == v2, lane, float(LANES)), axis=-1, keepdims=True)
    hot2 = lane == i2
    e = jnp.exp(v2 - v1)
    w1 = 1.0 / (1.0 + e)
    w2 = e / (1.0 + e)
    sel = jnp.where(hot1 | hot2, 1.0, 0.0)
    r = lax.broadcasted_iota(jnp.int32, (tb, tb), 0)
    c = lax.broadcasted_iota(jnp.int32, (tb, tb), 1)
    tri = jnp.where(c < r, 1.0, 0.0).astype(BF16)
    before = jnp.dot(tri, sel.astype(BF16), preferred_element_type=F32) + base_ref[...]
    rank1 = jnp.sum(jnp.where(hot1, before, 0.0), axis=-1, keepdims=True)
    rank2 = jnp.sum(jnp.where(hot2, before, 0.0), axis=-1, keepdims=True)
    base_ref[...] += jnp.sum(sel, axis=0, keepdims=True)
    cnt_ref[...] = jnp.broadcast_to(base_ref[...], cnt_ref.shape)
    info = jnp.zeros((tb, LANES), F32)
    for slot, val in enumerate((i1, i2, rank1, rank2, w1, w2)):
        info = jnp.where(lane == float(slot), val, info)
    info_ref[...] = info


def _route(logits, n_experts):
    n = logits.shape[0]
    tb = _pick(n, (256, 128, 64, 32, 16, 8))
    vmem = 4 * _nbytes((tb, LANES), F32) + 16 * _nbytes((tb, max(tb, LANES)), F32)
    return pl.pallas_call(
        functools.partial(_route_kernel, n_experts=n_experts, tb=tb),
        grid=(n // tb,),
        in_specs=[pl.BlockSpec((tb, LANES), lambda i: (i, 0))],
        out_specs=[pl.BlockSpec((tb, LANES), lambda i: (i, 0)),
                   pl.BlockSpec((SUBLANES, LANES), lambda i: (0, 0))],
        out_shape=[jax.ShapeDtypeStruct((n, LANES), F32), jax.ShapeDtypeStruct((SUBLANES, LANES), F32)],
        scratch_shapes=[pltpu.VMEM((1, LANES), F32)],
        compiler_params=_params(("arbitrary",), vmem),
        name="moe_route",
    )(logits)


def _row_copy(src_hbm, row, dst_vmem, slot, sem):
    return pltpu.make_async_copy(src_hbm.at[pl.ds(row, 1)], dst_vmem.at[pl.ds(slot, 1)], sem)


def _gather_kernel(src_ref, h_hbm, o_ref, buf, sem, *, tm, n_tiles):
    i = pl.program_id(0)

    def issue_tile(tile, slot):
        def issue(pair, carry):
            for lane in range(2):
                r = 2 * pair + lane
                _row_copy(h_hbm, src_ref[tile * tm + r], buf.at[slot], r, sem.at[slot]).start(priority=lane)
            return carry

        lax.fori_loop(0, tm // 2, issue, 0, unroll=DMA_LOOP_UNROLL // 2)

    @pl.when(i == 0)
    def _():
        issue_tile(0, 0)

    @pl.when(i + 1 < n_tiles)
    def _():
        issue_tile(i + 1, (i + 1) % 2)

    slot = i % 2

    def drain(r, carry):
        _row_copy(h_hbm, 0, buf.at[slot], r, sem.at[slot]).wait()
        return carry

    lax.fori_loop(0, tm, drain, 0, unroll=DMA_LOOP_UNROLL)
    left, right = _unpack_bf16_pairs(buf[slot])
    half = left.shape[1]
    o_ref[:, :half] = left.astype(o_ref.dtype)
    o_ref[:, half:] = right.astype(o_ref.dtype)


def _gather_rows(src_rows, h, *, tm):
    p = src_rows.shape[0]
    d = 2 * h.shape[1]
    vmem = 4 * _nbytes((tm, d // 2), jnp.uint32) + 2 * _nbytes((tm, d), BF16)
    gs = pltpu.PrefetchScalarGridSpec(
        num_scalar_prefetch=1, grid=(p // tm,),
        in_specs=[pl.BlockSpec(memory_space=pl.ANY)],
        out_specs=pl.BlockSpec((tm, d), lambda i, src: (i, 0)),
        scratch_shapes=[pltpu.VMEM((2, tm, d // 2), jnp.uint32), pltpu.SemaphoreType.DMA((2,))])
    return pl.pallas_call(
        functools.partial(_gather_kernel, tm=tm, n_tiles=p // tm), grid_spec=gs,
        out_shape=jax.ShapeDtypeStruct((p, d), BF16),
        compiler_params=_params(("arbitrary",), vmem),
        name="moe_gather",
    )(src_rows, h)


def _combine_kernel(d1_ref, d2_ref, y_hbm, x_ref, info_ref, gate_ref, g_ref, o_ref, buf1, buf2, sem, *, tb,
                    pair_block):
    base = pl.program_id(0) * tb

    def issue(r, carry):
        _row_copy(y_hbm, d1_ref[base + r], buf1, r, sem.at[0]).start(priority=0)
        _row_copy(y_hbm, d2_ref[base + r], buf2, r, sem.at[1]).start(priority=1)
        return carry

    lax.fori_loop(0, tb, issue, 0, unroll=DMA_LOOP_UNROLL)

    def drain(r, carry):
        _row_copy(y_hbm, 0, buf1, r, sem.at[0]).wait()
        _row_copy(y_hbm, 0, buf2, r, sem.at[1]).wait()
        return carry

    lax.fori_loop(0, tb, drain, 0, unroll=DMA_LOOP_UNROLL)
    def rows_of(buf):
        words = pair_block // 2
        pieces = []
        for c in range(0, buf.shape[1], words):
            pieces += list(_unpack_bf16_pairs(buf[:, c:c + words]))
        return jnp.concatenate(pieces, axis=-1)

    info = info_ref[...]
    w1, w2 = info[:, 4:5], info[:, 5:6]
    x = x_ref[...] + gate_ref[0] * (w1 * rows_of(buf1) + w2 * rows_of(buf2))
    o_ref[...] = _norm_rows(x, g_ref[...])


def _combine(dest1, dest2, y, x, info, gate, final_norm, *, seq, batch, pair_block):
    n, d = x.shape
    tb = _pick(seq, (256, 128, 64, 32, 16, 8))
    midx = _mod_index(tb, seq, batch)
    vmem = 2 * _nbytes((tb, d), F32) * 3 + 2 * _nbytes((tb, d), F32) + 2 * _nbytes((tb, LANES), F32)
    gs = pltpu.PrefetchScalarGridSpec(
        num_scalar_prefetch=2, grid=(n // tb,),
        in_specs=[pl.BlockSpec(memory_space=pl.ANY),
                  pl.BlockSpec((tb, d), lambda i, a, b: (i, 0)),
                  pl.BlockSpec((tb, LANES), lambda i, a, b: (i, 0)),
                  pl.BlockSpec((1, 1, d), lambda i, a, b: (midx(i), 0, 0)),
                  pl.BlockSpec((1, d), lambda i, a, b: (0, 0))],
        out_specs=pl.BlockSpec((tb, d), lambda i, a, b: (i, 0)),
        scratch_shapes=[pltpu.VMEM((tb, d // 2), jnp.uint32), pltpu.VMEM((tb, d // 2), jnp.uint32),
                        pltpu.SemaphoreType.DMA((2,))])
    return pl.pallas_call(
        functools.partial(_combine_kernel, tb=tb, pair_block=pair_block), grid_spec=gs,
        out_shape=jax.ShapeDtypeStruct((n, d), F32),
        compiler_params=_params(("arbitrary",), vmem),
        name="moe_combine_norm",
    )(dest1, dest2, y, x, info, gate, final_norm.reshape(1, d))


def _expert_mm_kernel(te_ref, used_ref, x_ref, *refs, n_w, swiglu):
    w_refs, o_ref, wbf_refs = refs[:n_w], refs[n_w], refs[n_w + 1:]
    i = pl.program_id(1)
    live = i < used_ref[0]
    fresh = jnp.logical_or(i == 0, te_ref[i] != te_ref[jnp.maximum(i - 1, 0)])

    @pl.when(jnp.logical_and(live, fresh))
    def _():
        for w, wb in zip(w_refs, wbf_refs):
            wb[...] = w[0].astype(wb.dtype)

    @pl.when(live)
    def _():
        x = x_ref[...]
        parts = [jnp.dot(x, wb[...], preferred_element_type=F32) for wb in wbf_refs]
        out = parts[0] * jax.nn.sigmoid(parts[0]) * parts[1] if swiglu else parts[0]
        o_ref[...] = _pack_bf16_pairs(out) if o_ref.dtype == jnp.uint32 else out.astype(o_ref.dtype)

    @pl.when(jnp.logical_not(live))
    def _():
        o_ref[...] = jnp.zeros_like(o_ref)


def _expert_matmul(x, ws, tile_expert, tiles_used, *, out_dtype, tm, tn, swiglu=False, name):
    rows, kdim = x.shape
    n = ws[0].shape[-1]
    n_w = len(ws)
    assert rows % tm == 0 and n % tn == 0
    shrink = 2 if out_dtype == jnp.uint32 else 1
    vmem = 2 * (_nbytes((tm, kdim), x.dtype) + n_w * _nbytes((kdim, tn), ws[0].dtype) + _nbytes((tm, tn), out_dtype))
    vmem += n_w * (_nbytes((kdim, tn), BF16) + 2 * _nbytes((tm, tn), F32))
    gs = pltpu.PrefetchScalarGridSpec(
        num_scalar_prefetch=2, grid=(n // tn, rows // tm),
        in_specs=[pl.BlockSpec((tm, kdim), lambda j, i, te, nu: (i, 0))]
        + [pl.BlockSpec((1, kdim, tn), lambda j, i, te, nu: (te[i], 0, j))] * n_w,
        out_specs=pl.BlockSpec((tm, tn // shrink), lambda j, i, te, nu: (i, j)),
        scratch_shapes=[pltpu.VMEM((kdim, tn), BF16)] * n_w)
    return pl.pallas_call(
        functools.partial(_expert_mm_kernel, n_w=n_w, swiglu=swiglu), grid_spec=gs,
        out_shape=jax.ShapeDtypeStruct((rows, n // shrink), out_dtype),
        compiler_params=_params(("arbitrary", "arbitrary"), vmem),
        name=name,
    )(tile_expert, tiles_used, x, *ws)


def _moe(h_packed, logits, x, gate, final_norm, w_gate, w_up, w_down, *, dims):
    n, d = x.shape
    n_exp = dims.n_experts
    tm = _pick(n, (512, 256, 128, 64, 32, 16))
    p = 2 * n + n_exp * tm
    info, counts = _route(logits, n_exp)
    cnt = counts[0, :n_exp].astype(jnp.int32)
    padded = ((cnt + tm - 1) // tm) * tm
    ends = jnp.cumsum(padded)
    starts = ends - padded
    e1, e2 = info[:, 0].astype(jnp.int32), info[:, 1].astype(jnp.int32)
    dest1 = starts[e1] + info[:, 2].astype(jnp.int32)
    dest2 = starts[e2] + info[:, 3].astype(jnp.int32)
    token = jnp.arange(n, dtype=jnp.int32)
    src = jnp.zeros((p,), jnp.int32).at[dest1].set(token).at[dest2].set(token)
    tile_start = jnp.arange(p // tm, dtype=jnp.int32) * tm
    tile_expert = jnp.minimum(jnp.sum(tile_start[:, None] >= ends[None, :], axis=1), n_exp - 1).astype(jnp.int32)
    tiles_used = (ends[-1:] // tm).astype(jnp.int32)

    xs = _gather_rows(src, h_packed, tm=_pick(tm, (256, 128, 64, 32, 16)))
    f = w_gate.shape[-1]
    act = _expert_matmul(xs, [w_gate, w_up], tile_expert, tiles_used, out_dtype=BF16, tm=tm,
                         tn=_pick(f, (512, 256, 128)), swiglu=True, name="moe_gate_up")
    tn_down = _pick(d, (1024, 512, 256))
    y = _expert_matmul(act, [w_down], tile_expert, tiles_used, out_dtype=jnp.uint32, tm=tm, tn=tn_down,
                       name="moe_down")
    return _combine(dest1, dest2, y, x, info, gate, final_norm, seq=dims.seq, batch=dims.batch, pair_block=tn_down)


def _pad_cols(w, n):
    return w if w.shape[-1] == n else jnp.pad(w, [(0, 0)] * (w.ndim - 1) + [(0, n - w.shape[-1])])


def _forward(dims, x, c, ctx, c_ctx,
             ada_w0, ada_b0, norm_mix0, norm_ffn0, w_in0, a_q_norm, a_k_norm, b_q_norm, b_kv_norm,
             w_uq, w_ukv, w_o0, ffn_w_gate, ffn_w_up, ffn_w_down,
             ada_w1, ada_b1, norm_mix1, norm_ffn1, w_in1, c_sink, d_lam_q1, d_lam_k1, d_lam_q2, d_lam_k2,
             d_subln, w_o1, router_w, moe_w_gate, moe_w_up, moe_w_down, final_norm):
    dm, bsz, seq, ctx_len = dims.d_model, dims.batch, dims.seq, dims.ctx_len
    nl, nc = bsz * seq, bsz * ctx_len
    nt = nl + nc
    ha, hka, hb = dims.a_heads, dims.a_kv_heads, dims.b_heads
    hc, hkc, hd = dims.c_heads, dims.c_kv_heads, dims.d_heads
    big = lambda n: _pick(n, (1024, 512, 256, 128, 64, 32, 16))
    tm_all, tm_lat = big(math.gcd(nt, seq)), big(math.gcd(nl, seq))

    x_rows, ctx_rows = x.reshape(nl, dm), ctx.reshape(nc, dm)
    n_cond = -(-(bsz + 1) // SUBLANES) * SUBLANES
    cond = jnp.concatenate([c, c_ctx[None, :], jnp.zeros((n_cond - bsz - 1, dm), F32)], axis=0)

    def mods(ada_w, ada_b):
        m = _modulation(cond, ada_w, ada_b).reshape(n_cond, N_MOD, 1, dm)
        return [m[:, i] for i in range(N_MOD)]

    rope128 = _rope_tables(seq, dims.grid_w, 128, tm_all)
    rope64 = _rope_tables(seq, dims.grid_w, 64, tm_all)
    flash = functools.partial(_flash, batch=bsz, seq=seq, ctx_len=ctx_len, lat_rows=nl)

    def project(src, w, *, rows=nt, out_dtype=BF16, name, **head):
        n = w.shape[1]
        tm = tm_all if rows == nt else tm_lat
        if head:
            return _matmul(src, [w.astype(BF16)], out_dtype=out_dtype, tm=tm, tn=_pick(n, (1024, 512, 256, 128)),
                           rows=rows, epi="heads", head=dict(head, lat_rows=nl), seq=seq, name=name)
        return _matmul(src, [w.astype(BF16)], out_dtype=out_dtype, tm=tm, tn=_pick(n, (512, 256, 128)),
                       rows=rows, name=name)

    def values_t(src, col0, n_heads, name, chunk=_kv_chunk(seq)):
        return (_values_t(src, col0, n_heads, chunk=chunk, row0=0, rows=nl, name=name),
                _values_t(src, col0, n_heads, chunk=ctx_len, row0=nl, rows=nc, name=name + "_ctx"))

    sh1, sc1, g1, sh2, sc2, g2 = mods(ada_w0, ada_b0)
    h = _norm_mod((x_rows, ctx_rows), norm_mix0, sh1, sc1, rows=nt, seq=seq, batch=bsz)
    na_q, na_kv, rq, rkv = ha * HEAD_DIM, hka * HEAD_DIM, dims.b_q_rank, dims.b_kv_rank
    c_ak, c_av, c_cq, c_ckv, c_kr = na_q, na_q + na_kv, na_q + 2 * na_kv, na_q + 2 * na_kv + rq, na_q + 2 * na_kv + rq + rkv
    aq = project(h, w_in0[:, :c_ak], gain=a_q_norm, rope=rope128, rot_dim=128, scale=LOG2E * HEAD_DIM ** -0.5,
                 name="proj_aq")
    ak = project(h, w_in0[:, c_ak:c_av], gain=a_k_norm, rope=rope128, rot_dim=128, name="proj_ak")
    bk_rope = project(h, _pad_cols(w_in0[:, c_kr:], LANES), rope=rope64, rot_dim=64, name="proj_bk_rope")
    rest = project(h, jnp.concatenate([w_in0[:, c_cq:c_kr], w_in0[:, c_av:c_cq]], axis=1), out_dtype=F32,
                   name="proj_ranks_av")
    av = values_t(rest, (rq + rkv) // HEAD_DIM, hka, "vt_a")
    cq = _rank_norm(rest, 0, rq, b_q_norm)
    ckv = _rank_norm(rest, rq, rkv, b_kv_norm)
    b_scale = LOG2E * (HEAD_DIM + 64) ** -0.5
    uq = w_uq.reshape(rq, hb, HEAD_DIM + 64) * b_scale
    bq_nope = project(cq, uq[:, :, :HEAD_DIM].reshape(rq, hb * HEAD_DIM), name="b_up_q_nope")
    bq_rope = project(cq, _pad_cols(uq[:, :, HEAD_DIM:], LANES).reshape(rq, hb * LANES), rope=rope64, rot_dim=64,
                      name="b_up_q_rope")
    ukv = w_ukv.reshape(rkv, hb, 2 * HEAD_DIM)
    ukv = jnp.concatenate([ukv[:, :, :HEAD_DIM].reshape(rkv, hb * HEAD_DIM),
                           ukv[:, :, HEAD_DIM:].reshape(rkv, hb * HEAD_DIM)], axis=1)
    bkv = project(ckv, ukv, name="b_up_kv")

    a_lat = flash([(aq, 0)], [(ak, 0, 1)], av, n_heads=ha, group=ha // hka, name="attn_a")
    a_ctx = flash([(aq, 0)], [(ak, 0, 1)], av, n_heads=ha, group=ha // hka, ctx_queries=True, name="attn_a_ctx")
    b_args = ([(bq_nope, 0), (bq_rope, 0)], [(bkv, 0, 1), (bk_rope, 0, 0)], values_t(bkv, hb, hb, "vt_b"))
    b_lat = flash(*b_args, n_heads=hb, group=1, name="attn_b")
    b_ctx = flash(*b_args, n_heads=hb, group=1, ctx_queries=True, name="attn_b_ctx")
    xa = _matmul([(a_lat, a_ctx), (b_lat, b_ctx)], [w_o0], out_dtype=F32, tm=tm_all, tn=_pick(dm, (512, 256, 128)),
                 epi="gres", res=(x_rows, ctx_rows), gate=g1, seq=seq, batch=bsz, name="out_proj0")

    h = _norm_mod(xa, norm_ffn0, sh2, sc2, rows=nt, seq=seq, batch=bsz)
    ffn = dims.ffn_dim
    act = _matmul(h, [ffn_w_gate, ffn_w_up], out_dtype=BF16, tm=tm_all, tn=_pick(ffn, (256, 128)), epi="swiglu",
                  name="ffn_gate_up")
    xa = _matmul(act, [ffn_w_down.astype(BF16)], out_dtype=F32, tm=tm_all // 2 if ffn > 4096 else tm_all,
                 tn=_pick(dm, (512, 256, 128)), epi="gres", res=xa, gate=g2, seq=seq, batch=bsz, name="ffn_down")

    sh1, sc1, g1, sh2, sc2, g2 = mods(ada_w1, ada_b1)
    h = _norm_mod(xa, norm_mix1, sh1, sc1, rows=nt, seq=seq, batch=bsz)
    bounds = [0]
    for n_heads in (hc, hkc, hkc, hd, hd, hd):
        bounds.append(bounds[-1] + n_heads * HEAD_DIM)
    w_cq, w_ck, w_cv, w_dq, w_dk, w_dv = (w_in1[:, lo:hi] for lo, hi in zip(bounds[:-1], bounds[1:]))
    cq_ = project(h, w_cq, rows=nl, rope=rope128, rot_dim=128, scale=LOG2E * HEAD_DIM ** -0.5, name="proj_cq")
    ck_ = project(h, w_ck, rope=rope128, rot_dim=128, name="proj_ck")
    dq_ = project(h, w_dq, rows=nl, rope=rope64, rot_dim=64, scale=LOG2E * 64 ** -0.5, name="proj_dq")
    dk_ = project(h, w_dk, rope=rope64, rot_dim=64, name="proj_dk")
    values = project(h, jnp.concatenate([w_cv, w_dv], axis=1), name="proj_cv_dv")
    cv_ = values_t(values, 0, hkc, "vt_c", chunk=LANES)
    dv_ = values_t(values, hkc, hd, "vt_d")
    sink = jnp.broadcast_to(LOG2E * c_sink.astype(F32)[:, None, None], (hc, 1, LANES))
    lam = jnp.stack([_pad_cols(v.astype(F32)[None, :], LANES)[0] for v in (d_lam_q1, d_lam_k1, d_lam_q2, d_lam_k2)])
    c_out = flash([(cq_, 0)], [(ck_, 0, 1)], cv_, n_heads=hc, group=hc // hkc, mode="window", sink=sink,
                  window=dims.window, name="attn_c")
    d_out = flash([(dq_, 0)], [(dk_, 0, 1)], dv_, n_heads=hd, group=1, mode="diff", lam=lam,
                  subln=d_subln.reshape(1, LANES), name="attn_d")
    xl = _matmul([c_out, d_out], [w_o1], out_dtype=F32, tm=tm_lat, tn=_pick(dm, (512, 256, 128)),
                 epi="gres", res=xa, gate=g1, seq=seq, batch=bsz, rows=nl, name="out_proj1")

    rw = _pad_cols(router_w, LANES)
    rw_hi = rw.astype(BF16)
    rw_lo = (rw - rw_hi.astype(F32)).astype(BF16)
    h_packed, logits = _norm_mod(xl, norm_ffn1, sh2, sc2, rows=nl, seq=seq, batch=bsz, router=(rw_hi, rw_lo))
    out = _moe(h_packed, logits, xl, g2, final_norm, moe_w_gate, moe_w_up, moe_w_down, dims=dims)
    return out.reshape(bsz, seq, dm)


def _rank_norm_kernel(x_ref, g_ref, o_ref):
    o_ref[...] = _norm_rows(x_ref[...], g_ref[...]).astype(o_ref.dtype)


def _rank_norm(src, col0, width, gain):
    rows = src.shape[0]
    tm = _pick(rows, (256, 128, 64, 32, 16))
    assert col0 % width == 0
    vmem = 2 * (_nbytes((tm, width), F32) + _nbytes((tm, width), BF16))
    return pl.pallas_call(
        _rank_norm_kernel,
        grid=(rows // tm,),
        in_specs=[pl.BlockSpec((tm, width), lambda i: (i, col0 // width)),
                  pl.BlockSpec((1, width), lambda i: (0, 0))],
        out_specs=pl.BlockSpec((tm, width), lambda i: (i, 0)),
        out_shape=jax.ShapeDtypeStruct((rows, width), BF16),
        compiler_params=_params(("parallel",), vmem),
        name="rank_norm",
    )(src, gain.reshape(1, width))


_DIMS = Dims(d_model=4096, batch=4, seq=4096, ctx_len=256, grid_w=64, a_heads=16, a_kv_heads=4, b_heads=16,
             b_q_rank=1536, b_kv_rank=512, c_heads=16, c_kv_heads=4, window=128, d_heads=16, ffn_dim=11008,
             n_experts=8, expert_dim=3584)


def kernel(x, c, ctx, c_ctx, ada_w0, ada_b0, norm_mix0, norm_ffn0, w_in0, a_q_norm, a_k_norm, b_q_norm, b_kv_norm, w_uq, w_ukv, w_o0, ffn_w_gate, ffn_w_up, ffn_w_down, ada_w1, ada_b1, norm_mix1, norm_ffn1, w_in1, c_sink, d_lam_q1, d_lam_k1, d_lam_q2, d_lam_k2, d_subln, w_o1, router_w, moe_w_gate, moe_w_up, moe_w_down, final_norm):
    return _forward(_DIMS, x, c, ctx, c_ctx, ada_w0, ada_b0, norm_mix0, norm_ffn0, w_in0, a_q_norm, a_k_norm,
                    b_q_norm, b_kv_norm, w_uq, w_ukv, w_o0, ffn_w_gate, ffn_w_up, ffn_w_down, ada_w1, ada_b1,
                    norm_mix1, norm_ffn1, w_in1, c_sink, d_lam_q1, d_lam_k1, d_lam_q2, d_lam_k2, d_subln, w_o1,
                    router_w, moe_w_gate, moe_w_up, moe_w_down, final_norm)
```
